```python
import math
import jax
import jax.numpy as jnp
from jax import lax
import numpy as np

D_MODEL = 4096
BATCH = 1
SEQ = 16384
DEPTH = 2

D_LRU = D_MODEL // 4
D_HYENA = D_MODEL // 4
D_RET = D_MODEL // 2
MIX_WIDTH = D_LRU + D_HYENA + D_RET

LRU_HEADS = 8
LRU_HEAD_DIM = D_LRU // LRU_HEADS
LRU_CONV = 4
LRU_C = 8.0

HYENA_ORDER = 2
HYENA_GROUPS = 8
HYENA_CONV = 3
HYENA_EMB = 33
HYENA_FILTER_HIDDEN = 64
HYENA_N_FILTERS = 2 * HYENA_ORDER
HYENA_MIN_DECAY = math.log(1e-2) / 0.3
HYENA_MAX_DECAY = math.log(1e-2) / 1.5

RET_HEADS = 8
RET_KEY_DIM = 128
RET_VAL_DIM = D_RET // RET_HEADS
RET_CHUNK = 128
ROPE_BASE = 10000.0

D_FF = ((8 * D_MODEL + 3 * 256 - 1) // (3 * 256)) * 256

EPS = 1e-6
QK_WIDTH = RET_HEADS * RET_KEY_DIM
IN_WIDTH = 2 * D_LRU + 3 * D_HYENA + 2 * QK_WIDTH + 2 * D_RET
IN_SPLITS = (
    D_LRU,
    2 * D_LRU,
    2 * D_LRU + 3 * D_HYENA,
    2 * D_LRU + 3 * D_HYENA + QK_WIDTH,
    2 * D_LRU + 3 * D_HYENA + 2 * QK_WIDTH,
    2 * D_LRU + 3 * D_HYENA + 2 * QK_WIDTH + D_RET,
)

kernel_name = "hybrid_lru_hyena_retention_encoder"


def rms_norm(x, gain):
    xf = x.astype(jnp.float32)
    y = xf * lax.rsqrt(jnp.mean(xf * xf, axis=-1, keepdims=True) + EPS)
    return (y * gain.astype(jnp.float32)).astype(x.dtype)


def group_rms_norm(y, gain, groups):
    b, s, w = y.shape
    yg = y.reshape(b, s, groups, w // groups)
    yg = yg * lax.rsqrt(jnp.mean(yg * yg, axis=-1, keepdims=True) + EPS)
    return yg.reshape(b, s, w) * gain.astype(jnp.float32)


def centred_dwconv(x, w, bias):
    width = w.shape[0]
    left = width // 2
    s = x.shape[1]
    xp = jnp.pad(x, ((0, 0), (left, width - 1 - left), (0, 0)))
    w = w.astype(jnp.float32)
    y = bias.astype(jnp.float32)
    for j in range(width):
        y = y + xp[:, j:j + s] * w[j]
    return y


def _linear_recurrence(left, right):
    a1, b1 = left
    a2, b2 = right
    return a1 * a2, a2 * b1 + b2


def rg_lru_bidir(xc, wr, br, wi, bi, lam):
    b, s, c = xc.shape
    xh = xc.reshape(b, s, LRU_HEADS, LRU_HEAD_DIM)
    f32 = jnp.float32
    r = jax.nn.sigmoid(jnp.einsum('bshi,dhij->dbshj', xh, wr.astype(f32)).reshape(2, b, s, c)
                       + br.astype(f32)[:, None, None, :])
    i = jax.nn.sigmoid(jnp.einsum('bshi,dhij->dbshj', xh, wi.astype(f32)).reshape(2, b, s, c)
                       + bi.astype(f32)[:, None, None, :])
    log_a = -LRU_C * r * jax.nn.softplus(-lam.astype(f32))[:, None, None, :]
    a = jnp.exp(log_a)
    inp = jnp.sqrt(-jnp.expm1(2.0 * log_a)) * (i * xc[None])
    _, h_fwd = lax.associative_scan(_linear_recurrence, (a[0], inp[0]), axis=1)
    _, h_bwd = lax.associative_scan(_linear_recurrence, (a[1], inp[1]), axis=1, reverse=True)
    return h_fwd + h_bwd


def implicit_filters(length, w1, b1, w2, b2, w3, freq, decay):
    f32 = jnp.float32
    bands = (HYENA_EMB - 1) // 2
    t = jnp.linspace(0.0, 1.0, length, dtype=f32)[:, None]
    omega = (2.0 * math.pi / length) * jnp.arange(length, dtype=f32)[:, None]
    f = jnp.linspace(1e-4, bands - 1, bands, dtype=f32)[None, :]
    feats = jnp.concatenate([t, jnp.cos(f * omega), -jnp.sin(f * omega)], axis=-1)
    freq = freq.astype(f32)
    hdn = jnp.sin(freq * (feats @ w1.astype(f32) + b1.astype(f32)))
    hdn = jnp.sin(freq * (hdn @ w2.astype(f32) + b2.astype(f32)))
    filt = (hdn @ w3.astype(f32)).reshape(length, HYENA_ORDER, 2, D_HYENA)
    rates = jnp.abs(decay.astype(f32)).reshape(HYENA_ORDER, 2, D_HYENA)
    window = jnp.exp(-t[:, :, None, None] * rates[None])
    return filt * window


def two_sided(h_fwd, h_bwd):
    zero = jnp.zeros_like(h_fwd[:1])
    return jnp.concatenate([h_fwd[:1] + h_bwd[:1], h_fwd[1:], zero, h_bwd[:0:-1]], axis=0)


def long_conv(z, g):
    length = z.shape[1]
    zf = jnp.fft.rfft(z, n=2 * length, axis=1)
    gf = jnp.fft.rfft(g, n=2 * length, axis=0)
    return jnp.fft.irfft(zf * gf[None], n=2 * length, axis=1)[:, :length]


def hyena_bidir(u, conv_w, conv_b, f_w1, f_b1, f_w2, f_b2, f_w3, freq, decay, skip):
    length = u.shape[1]
    u = centred_dwconv(u, conv_w, conv_b)
    v, *gates = jnp.split(u, HYENA_ORDER + 1, axis=-1)
    filt = implicit_filters(length, f_w1, f_b1, f_w2, f_b2, f_w3, freq, decay)
    skip = skip.astype(jnp.float32)
    z = v
    for o in range(HYENA_ORDER):
        g = two_sided(filt[:, o, 0], filt[:, o, 1])
        z = gates[o] * (long_conv(z, g) + skip[o] * z)
    return z


def rotary(x, pos):
    half = x.shape[-1] // 2
    inv = ROPE_BASE ** (-jnp.arange(half, dtype=jnp.float32) / half)
    ang = pos[:, None] * inv[None, :]
    cos = jnp.cos(ang)[None, :, None, :]
    sin = jnp.sin(ang)[None, :, None, :]
    x1, x2 = x[..., :half], x[..., half:]
    return jnp.concatenate([x1 * cos - x2 * sin, x1 * sin + x2 * cos], axis=-1)


def retention_bidir(q, k, v):
    b, s, nh, dk = q.shape
    dv = v.shape[-1]
    c = RET_CHUNK
    nc = s // c
    f32 = jnp.float32
    log_g = jnp.log1p(-jnp.exp2(-5.0 - jnp.arange(nh, dtype=f32)))
    qc = q.reshape(b, nc, c, nh, dk).transpose(0, 3, 1, 2, 4)
    kc = k.reshape(b, nc, c, nh, dk).transpose(0, 3, 1, 2, 4)
    vc = v.reshape(b, nc, c, nh, dv).transpose(0, 3, 1, 2, 4)
    idx = jnp.arange(c, dtype=f32)
    lg = log_g[:, None]
    intra_decay = jnp.exp(lg[:, :, None] * jnp.abs(idx[:, None] - idx[None, :]))
    scores = jnp.einsum('bhnid,bhnjd->bhnij', qc, kc) * intra_decay[None, :, None]
    y = jnp.einsum('bhnij,bhnje->bhnie', scores, vc)
    k_to_end = kc * jnp.exp(lg * (c - 1 - idx))[None, :, None, :, None]
    k_to_start = kc * jnp.exp(lg * idx)[None, :, None, :, None]
    kv_fwd = jnp.einsum('bhncd,bhnce->nbhde', k_to_end, vc)
    kv_bwd = jnp.einsum('bhncd,bhnce->nbhde', k_to_start, vc)
    chunk_decay = jnp.exp(log_g * c)[None, :, None, None]

    def step(state, kv):
        return chunk_decay * state + kv, state

    init = jnp.zeros((b, nh, dk, dv), f32)
    _, s_prev = lax.scan(step, init, kv_fwd)
    _, s_next = lax.scan(step, init, kv_bwd, reverse=True)
    q_fwd = qc * jnp.exp(lg * (idx + 1.0))[None, :, None, :, None]
    q_bwd = qc * jnp.exp(lg * (c - idx))[None, :, None, :, None]
    y = y + jnp.einsum('bhnid,nbhde->bhnie', q_fwd, s_prev) \
          + jnp.einsum('bhnid,nbhde->bhnie', q_bwd, s_next)
    return y.transpose(0, 2, 3, 1, 4).reshape(b, s, nh * dv)


def hybrid_mixer(h, w_in, lru_conv_w, lru_conv_b, lru_wr, lru_br, lru_wi, lru_bi, lru_lambda,
                 lru_norm, hy_conv_w, hy_conv_b, hy_f_w1, hy_f_b1, hy_f_w2, hy_f_b2, hy_f_w3,
                 hy_freq, hy_decay, hy_skip, hy_norm, ret_norm, w_out):
    b, s, _ = h.shape
    f32 = jnp.float32
    proj = (h @ w_in).astype(f32)
    lru_gate, lru_x, hy_u, q, k, v, g = jnp.split(proj, IN_SPLITS, axis=-1)
    xc = centred_dwconv(lru_x, lru_conv_w, lru_conv_b)
    h_lru = rg_lru_bidir(xc, lru_wr, lru_br, lru_wi, lru_bi, lru_lambda)
    y_a = group_rms_norm(jax.nn.gelu(lru_gate, approximate=True) * h_lru, lru_norm, LRU_HEADS)
    y_b = group_rms_norm(
        hyena_bidir(hy_u, hy_conv_w, hy_conv_b, hy_f_w1, hy_f_b1, hy_f_w2, hy_f_b2, hy_f_w3,
                    hy_freq, hy_decay, hy_skip),
        hy_norm, HYENA_GROUPS)
    pos = jnp.arange(s, dtype=f32)
    q = rotary(q.reshape(b, s, RET_HEADS, RET_KEY_DIM), pos)
    k = rotary(k.reshape(b, s, RET_HEADS, RET_KEY_DIM), pos) * (RET_KEY_DIM ** -0.5)
    y_ret = retention_bidir(q, k, v.reshape(b, s, RET_HEADS, RET_VAL_DIM))
    y_c = jax.nn.silu(g) * group_rms_norm(y_ret, ret_norm, RET_HEADS)
    mix = jnp.concatenate([y_a, y_b, y_c], axis=-1).astype(h.dtype)
    return mix @ w_out


def swiglu(h, w_gate, w_up, w_down):
    return (jax.nn.silu(h @ w_gate) * (h @ w_up)) @ w_down


def setup_inputs(seed: int = 0) -> dict:
    f32 = jnp.float32
    key = jax.random.key(seed)
    keys = list(jax.random.split(key, 32))
    L = DEPTH

    def nrm(shape, scale):
        return jax.random.normal(keys.pop(), shape, f32) * scale

    def gain(shape):
        return 1.0 + nrm(shape, 0.01)

    def lru_lambda_init():
        u = jax.random.uniform(keys.pop(), (L, 2, D_LRU), f32, 0.9, 0.999)
        a = u ** (1.0 / LRU_C)
        return jnp.log(a) - jnp.log1p(-a)

    base_decay = jnp.linspace(HYENA_MIN_DECAY, HYENA_MAX_DECAY, D_HYENA, dtype=f32)
    return {
        "x": nrm((BATCH, SEQ, D_MODEL), 1.0),
        "norm_mix": gain((L, D_MODEL)),
        "w_in": nrm((L, D_MODEL, IN_WIDTH), D_MODEL ** -0.5),
        "lru_conv_w": nrm((L, LRU_CONV, D_LRU), LRU_CONV ** -0.5),
        "lru_conv_b": nrm((L, D_LRU), 0.01),
        "lru_wr": nrm((L, 2, LRU_HEADS, LRU_HEAD_DIM, LRU_HEAD_DIM), LRU_HEAD_DIM ** -0.5),
        "lru_br": nrm((L, 2, D_LRU), 0.01),
        "lru_wi": nrm((L, 2, LRU_HEADS, LRU_HEAD_DIM, LRU_HEAD_DIM), LRU_HEAD_DIM ** -0.5),
        "lru_bi": nrm((L, 2, D_LRU), 0.01),
        "lru_lambda": lru_lambda_init(),
        "lru_norm": gain((L, D_LRU)),
        "hy_conv_w": nrm((L, HYENA_CONV, 3 * D_HYENA), HYENA_CONV ** -0.5),
        "hy_conv_b": nrm((L, 3 * D_HYENA), 0.01),
        "hy_f_w1": nrm((L, HYENA_EMB, HYENA_FILTER_HIDDEN), HYENA_EMB ** -0.5),
        "hy_f_b1": nrm((L, HYENA_FILTER_HIDDEN), 0.02),
        "hy_f_w2": nrm((L, HYENA_FILTER_HIDDEN, HYENA_FILTER_HIDDEN), HYENA_FILTER_HIDDEN ** -0.5),
        "hy_f_b2": nrm((L, HYENA_FILTER_HIDDEN), 0.02),
        "hy_f_w3": nrm((L, HYENA_FILTER_HIDDEN, HYENA_N_FILTERS * D_HYENA), HYENA_FILTER_HIDDEN ** -0.5),
        "hy_freq": gain((L, HYENA_FILTER_HIDDEN)),
        "hy_decay": base_decay[None, None, :] * (1.0 + nrm((L, HYENA_N_FILTERS, D_HYENA), 0.05)),
        "hy_skip": nrm((L, HYENA_ORDER, D_HYENA), 1.0),
        "hy_norm": gain((L, D_HYENA)),
        "ret_norm": gain((L, D_RET)),
        "w_out": nrm((L, MIX_WIDTH, D_MODEL), MIX_WIDTH ** -0.5),
        "norm_ffn": gain((L, D_MODEL)),
        "w_gate": nrm((L, D_MODEL, D_FF), D_MODEL ** -0.5),
        "w_up": nrm((L, D_MODEL, D_FF), D_MODEL ** -0.5),
        "w_down": nrm((L, D_FF, D_MODEL), D_FF ** -0.5),
        "norm_final": gain((D_MODEL,)),
    }


def reference(x, norm_mix, w_in, lru_conv_w, lru_conv_b, lru_wr, lru_br, lru_wi, lru_bi,
              lru_lambda, lru_norm, hy_conv_w, hy_conv_b, hy_f_w1, hy_f_b1, hy_f_w2, hy_f_b2,
              hy_f_w3, hy_freq, hy_decay, hy_skip, hy_norm, ret_norm, w_out, norm_ffn,
              w_gate, w_up, w_down, norm_final):
    for l in range(DEPTH):
        h = rms_norm(x, norm_mix[l])
        x = x + hybrid_mixer(h, w_in[l], lru_conv_w[l], lru_conv_b[l], lru_wr[l], lru_br[l],
                             lru_wi[l], lru_bi[l], lru_lambda[l], lru_norm[l], hy_conv_w[l],
                             hy_conv_b[l], hy_f_w1[l], hy_f_b1[l], hy_f_w2[l], hy_f_b2[l],
                             hy_f_w3[l], hy_freq[l], hy_decay[l], hy_skip[l], hy_norm[l],
                             ret_norm[l], w_out[l])
        h = rms_norm(x, norm_ffn[l])
        x = x + swiglu(h, w_gate[l], w_up[l], w_down[l])
    return rms_norm(x, norm_final)
```

```python
import functools
import math

import numpy as np
import jax
import jax.numpy as jnp
from jax import lax
from jax.experimental import pallas as pl
from jax.experimental.pallas import tpu as pltpu

EPS = 1e-6
LRU_HEADS = 8
LRU_C = 8.0
HYENA_GROUPS = 8
HYENA_BANDS = 16
RET_HEADS = 8
RET_KEY_DIM = 128
ROPE_BASE = 10000.0

V7X_SUBLANES = 8
V7X_LANES = 128
VMEM_LIMIT_BYTES = 56 * 1024 * 1024
FFT_N2 = 256
F32 = jnp.float32
BF16 = jnp.bfloat16


def _params(*sem):
    return pltpu.CompilerParams(dimension_semantics=sem, vmem_limit_bytes=VMEM_LIMIT_BYTES)


def _dot(a, b):
    return jnp.dot(a, b, preferred_element_type=F32)


def _rmsnorm_kernel(x_ref, g_ref, o_ref):
    x = x_ref[...]
    y = x * lax.rsqrt(jnp.mean(x * x, axis=-1, keepdims=True) + EPS)
    o_ref[...] = (y * g_ref[...]).astype(o_ref.dtype)


def rmsnorm(x, gain, out_dtype, tm=256):
    s, d = x.shape
    tm = min(tm, s)
    return pl.pallas_call(
        _rmsnorm_kernel,
        grid=(s // tm,),
        in_specs=[pl.BlockSpec((tm, d), lambda i: (i, 0)), pl.BlockSpec((1, d), lambda i: (0, 0))],
        out_specs=pl.BlockSpec((tm, d), lambda i: (i, 0)),
        out_shape=jax.ShapeDtypeStruct((s, d), out_dtype),
        compiler_params=_params("parallel"),
        name="rmsnorm",
    )(x, gain.reshape(1, d).astype(F32))


def _group_norm_kernel(x_ref, g_ref, o_ref, *, groups):
    x = x_ref[...]
    w = x.shape[-1] // groups
    for h in range(groups):
        xh = x[:, h * w:(h + 1) * w]
        yh = xh * lax.rsqrt(jnp.mean(xh * xh, axis=-1, keepdims=True) + EPS)
        o_ref[:, h * w:(h + 1) * w] = (yh * g_ref[:, h * w:(h + 1) * w]).astype(o_ref.dtype)


def group_norm(x, gain, groups, out_dtype, tm=512):
    s, d = x.shape
    tm = min(tm, s)
    return pl.pallas_call(
        functools.partial(_group_norm_kernel, groups=groups),
        grid=(s // tm,),
        in_specs=[pl.BlockSpec((tm, d), lambda i: (i, 0)), pl.BlockSpec((1, d), lambda i: (0, 0))],
        out_specs=pl.BlockSpec((tm, d), lambda i: (i, 0)),
        out_shape=jax.ShapeDtypeStruct((s, d), out_dtype),
        compiler_params=_params("parallel"),
        name="group_norm",
    )(x, gain.reshape(1, d).astype(F32))


def _mm_kernel(a_ref, b_ref, *rest, has_res):
    if has_res:
        r_ref, o_ref, acc_ref = rest
    else:
        o_ref, acc_ref = rest
    k = pl.program_id(2)

    @pl.when(k == 0)
    def _():
        acc_ref[...] = jnp.zeros_like(acc_ref)

    acc_ref[...] += _dot(a_ref[...], b_ref[...])

    @pl.when(k == pl.num_programs(2) - 1)
    def _():
        out = acc_ref[...]
        if has_res:
            out = r_ref[...] + out
        o_ref[...] = out.astype(o_ref.dtype)


def matmul(a, b, residual=None, out_dtype=F32, tm=1024, tn=1024, tk=512):
    m, kd = a.shape
    n = b.shape[1]
    tm, tn, tk = min(tm, m), min(tn, n), min(tk, kd)
    has_res = residual is not None
    in_specs = [pl.BlockSpec((tm, tk), lambda i, j, k: (i, k)), pl.BlockSpec((tk, tn), lambda i, j, k: (k, j))]
    args = [a, b]
    if has_res:
        in_specs.append(pl.BlockSpec((tm, tn), lambda i, j, k: (i, j)))
        args.append(residual)
    return pl.pallas_call(
        functools.partial(_mm_kernel, has_res=has_res),
        grid=(m // tm, n // tn, kd // tk),
        in_specs=in_specs,
        out_specs=pl.BlockSpec((tm, tn), lambda i, j, k: (i, j)),
        out_shape=jax.ShapeDtypeStruct((m, n), out_dtype),
        scratch_shapes=[pltpu.VMEM((tm, tn), F32)],
        compiler_params=_params("parallel", "parallel", "arbitrary"),
        name="matmul_res" if has_res else "matmul",
    )(*args)


def _ffn_up_kernel(h_ref, wg_ref, wu_ref, o_ref, accg_ref, accu_ref):
    k = pl.program_id(2)

    @pl.when(k == 0)
    def _():
        accg_ref[...] = jnp.zeros_like(accg_ref)
        accu_ref[...] = jnp.zeros_like(accu_ref)

    h = h_ref[...]
    accg_ref[...] += _dot(h, wg_ref[...])
    accu_ref[...] += _dot(h, wu_ref[...])

    @pl.when(k == pl.num_programs(2) - 1)
    def _():
        g = accg_ref[...]
        o_ref[...] = (jax.nn.silu(g) * accu_ref[...]).astype(o_ref.dtype)


def ffn_up(h, wg, wu, tm=1024, tn=1024, tk=512):
    m, kd = h.shape
    n = wg.shape[1]
    tm, tn, tk = min(tm, m), min(tn, n), min(tk, kd)
    return pl.pallas_call(
        _ffn_up_kernel,
        grid=(m // tm, n // tn, kd // tk),
        in_specs=[
            pl.BlockSpec((tm, tk), lambda i, j, k: (i, k)),
            pl.BlockSpec((tk, tn), lambda i, j, k: (k, j)),
            pl.BlockSpec((tk, tn), lambda i, j, k: (k, j)),
        ],
        out_specs=pl.BlockSpec((tm, tn), lambda i, j, k: (i, j)),
        out_shape=jax.ShapeDtypeStruct((m, n), BF16),
        scratch_shapes=[pltpu.VMEM((tm, tn), F32), pltpu.VMEM((tm, tn), F32)],
        compiler_params=_params("parallel", "parallel", "arbitrary"),
        name="ffn_up",
    )(h, wg, wu)


def _halo_specs(tb, width, col, nb, rev=False):
    r = tb // V7X_SUBLANES
    last = nb * r - 1

    def ti(i):
        return (nb - 1 - i) if rev else i

    return [
        pl.BlockSpec((V7X_SUBLANES, width), lambda i: (jnp.maximum(ti(i) * r - 1, 0), col)),
        pl.BlockSpec((tb, width), lambda i: (ti(i), col)),
        pl.BlockSpec((V7X_SUBLANES, width), lambda i: (jnp.minimum((ti(i) + 1) * r, last), col)),
    ]


def _dwconv_block(prev_ref, x_ref, next_ref, w_ref, b_ref, ext_ref, ti, nb, tb, width):
    left = width // 2
    prev = jnp.where(ti == 0, 0.0, prev_ref[...])
    nxt = jnp.where(ti == nb - 1, 0.0, next_ref[...])
    ext_ref[0:V7X_SUBLANES, :] = prev
    ext_ref[V7X_SUBLANES:V7X_SUBLANES + tb, :] = x_ref[...]
    ext_ref[V7X_SUBLANES + tb:2 * V7X_SUBLANES + tb, :] = nxt
    y = b_ref[...]
    for j in range(width):
        y = y + ext_ref[pl.ds(V7X_SUBLANES - left + j, tb), :] * w_ref[j:j + 1, :]
    return y


def _softplus(x):
    return jnp.maximum(x, 0.0) + jnp.log1p(jnp.exp(-jnp.abs(x)))


def _lru_kernel(*refs, reverse, final, nb, tb):
    if final:
        (xp_ref, x_ref, xn_ref, cw_ref, cb_ref, w_ref, gb_ref, lam_ref, hf_ref, gate_ref, norm_ref,
         o_ref, ext_ref, a_ref, b_ref, carry_ref) = refs
    else:
        (xp_ref, x_ref, xn_ref, cw_ref, cb_ref, w_ref, gb_ref, lam_ref,
         o_ref, ext_ref, a_ref, b_ref, carry_ref) = refs
    i = pl.program_id(0)
    ti = (nb - 1 - i) if reverse else i
    hd = V7X_LANES
    width = x_ref.shape[-1]

    @pl.when(i == 0)
    def _():
        carry_ref[...] = jnp.zeros_like(carry_ref)

    xc = _dwconv_block(xp_ref, x_ref, xn_ref, cw_ref, cb_ref, ext_ref, ti, nb, tb, 4)
    sp = _softplus(-lam_ref[...])
    for h in range(LRU_HEADS):
        sl = slice(h * hd, (h + 1) * hd)
        xh = xc[:, sl]
        z = _dot(xh.astype(BF16), w_ref[h]) + gb_ref[h]
        r = jax.nn.sigmoid(z[:, :hd])
        ig = jax.nn.sigmoid(z[:, hd:])
        log_a = -LRU_C * r * sp[:, sl]
        a = jnp.exp(log_a)
        a_ref[:, sl] = a
        b_ref[:, sl] = jnp.sqrt(-jnp.tanh(log_a) * (a * a + 1.0)) * (ig * xh)

    ng = tb // V7X_SUBLANES
    row = lax.broadcasted_iota(jnp.int32, (V7X_SUBLANES, width), 0)

    def body(g, carry):
        gi = (ng - 1 - g) if reverse else g
        off = pl.multiple_of(gi * V7X_SUBLANES, V7X_SUBLANES)
        a = a_ref[pl.ds(off, V7X_SUBLANES), :]
        b = b_ref[pl.ds(off, V7X_SUBLANES), :]
        for s in (1, 2, 4):
            if reverse:
                shift, m = V7X_SUBLANES - s, row < V7X_SUBLANES - s
            else:
                shift, m = s, row >= s
            b = jnp.where(m, a * pltpu.roll(b, shift, 0) + b, b)
            a = jnp.where(m, a * pltpu.roll(a, shift, 0), a)
        hcur = a * carry + b
        b_ref[pl.ds(off, V7X_SUBLANES), :] = hcur
        return hcur[0:1, :] if reverse else hcur[V7X_SUBLANES - 1:V7X_SUBLANES, :]

    carry_ref[...] = lax.fori_loop(0, ng, body, carry_ref[...])

    if not final:
        o_ref[...] = b_ref[...]
    else:
        y = jax.nn.gelu(gate_ref[...], approximate=True) * (hf_ref[...] + b_ref[...])
        for h in range(LRU_HEADS):
            sl = slice(h * hd, (h + 1) * hd)
            yh = y[:, sl]
            yh = yh * lax.rsqrt(jnp.mean(yh * yh, axis=-1, keepdims=True) + EPS)
            o_ref[:, sl] = (yh * norm_ref[:, sl]).astype(o_ref.dtype)


def lru_branch(proj, conv_w, conv_b, wr, br, wi, bi, lam, norm, tb=512):
    s = proj.shape[0]
    c = conv_w.shape[1]
    hd = c // LRU_HEADS
    tb = min(tb, s)
    nb = s // tb
    w = jnp.concatenate([wr, wi], axis=-1).astype(BF16)
    gb = jnp.concatenate([br.reshape(2, LRU_HEADS, 1, hd), bi.reshape(2, LRU_HEADS, 1, hd)], axis=-1).astype(F32)
    full = lambda shape: pl.BlockSpec(shape, lambda i: (0,) * len(shape))
    scratch = [pltpu.VMEM((tb + 2 * V7X_SUBLANES, c), F32), pltpu.VMEM((tb, c), F32), pltpu.VMEM((tb, c), F32),
               pltpu.VMEM((1, c), F32)]
    common = [full((4, c)), full((1, c)), full((LRU_HEADS, hd, 2 * hd)), full((LRU_HEADS, 1, 2 * hd)), full((1, c))]
    cw, cb = conv_w.astype(F32), conv_b.reshape(1, c).astype(F32)
    h_fwd = pl.pallas_call(
        functools.partial(_lru_kernel, reverse=False, final=False, nb=nb, tb=tb),
        grid=(nb,),
        in_specs=_halo_specs(tb, c, 1, nb) + common,
        out_specs=pl.BlockSpec((tb, c), lambda i: (i, 0)),
        out_shape=jax.ShapeDtypeStruct((s, c), F32),
        scratch_shapes=scratch,
        compiler_params=_params("arbitrary"),
        name="lru_fwd",
    )(proj, proj, proj, cw, cb, w[0], gb[0], lam[0].reshape(1, c).astype(F32))
    rev = lambda i: (nb - 1 - i, 0)
    return pl.pallas_call(
        functools.partial(_lru_kernel, reverse=True, final=True, nb=nb, tb=tb),
        grid=(nb,),
        in_specs=_halo_specs(tb, c, 1, nb, rev=True) + common
        + [pl.BlockSpec((tb, c), rev), pl.BlockSpec((tb, c), rev), full((1, c))],
        out_specs=pl.BlockSpec((tb, c), rev),
        out_shape=jax.ShapeDtypeStruct((s, c), BF16),
        scratch_shapes=scratch,
        compiler_params=_params("arbitrary"),
        name="lru_bwd",
    )(proj, proj, proj, cw, cb, w[1], gb[1], lam[1].reshape(1, c).astype(F32), h_fwd, proj,
      norm.reshape(1, c).astype(F32))


def _ret_log_gamma():
    return [float(np.log1p(-np.exp2(np.float32(-5.0 - h)), dtype=np.float32)) for h in range(RET_HEADS)]


def _ret_kernel(*refs, reverse, nc, c):
    if reverse:
        (q_ref, k_ref, v0_ref, v1_ref, inv_ref, y1_ref, g0_ref, g1_ref, norm_ref,
         o_ref, state_ref, dm_ref) = refs
    else:
        q_ref, k_ref, v0_ref, v1_ref, inv_ref, o_ref, state_ref, dm_ref = refs
    i = pl.program_id(0)
    ti = (nc - 1 - i) if reverse else i
    dk = RET_KEY_DIM
    dv = v0_ref.shape[-1] * 2 // RET_HEADS
    log_g = _ret_log_gamma()
    idx = lax.broadcasted_iota(jnp.int32, (c, 1), 0).astype(F32)

    @pl.when(i == 0)
    def _():
        state_ref[...] = jnp.zeros_like(state_ref)
        if not reverse:
            d = jnp.abs(lax.broadcasted_iota(jnp.int32, (c, c), 0)
                        - lax.broadcasted_iota(jnp.int32, (c, c), 1)).astype(F32)
            for h in range(RET_HEADS):
                dm_ref[h] = jnp.exp(log_g[h] * d)

    pos = (ti * c).astype(F32) + idx
    ang = pos * inv_ref[...]
    lane = lax.broadcasted_iota(jnp.int32, (c, dk), 1)
    cos_t = jnp.cos(ang)
    sin_t = jnp.where(lane < dk // 2, -1.0, 1.0) * jnp.sin(ang)

    def rot(x):
        return x * cos_t + pltpu.roll(x, dk // 2, 1) * sin_t

    hpb = RET_HEADS // 2
    for h in range(RET_HEADS):
        qh = rot(q_ref[:, h * dk:(h + 1) * dk])
        kh = rot(k_ref[:, h * dk:(h + 1) * dk]) * (dk ** -0.5)
        v_ref = v0_ref if h < hpb else v1_ref
        vs = slice((h % hpb) * dv, (h % hpb + 1) * dv)
        vh = v_ref[:, vs].astype(BF16)
        lg = log_g[h]
        if reverse:
            q_dec = qh * jnp.exp(lg * (c - idx))
            k_dec = kh * jnp.exp(lg * idx)
            y = y1_ref[:, h * dv:(h + 1) * dv]
        else:
            q_dec = qh * jnp.exp(lg * (idx + 1.0))
            k_dec = kh * jnp.exp(lg * (c - 1.0 - idx))
            scores = lax.dot_general(qh.astype(BF16), kh.astype(BF16), (((1,), (1,)), ((), ())),
                                     preferred_element_type=F32) * dm_ref[h]
            y = _dot(scores.astype(BF16), vh)
        st = state_ref[h]
        y = y + _dot(q_dec.astype(BF16), st.astype(BF16))
        kv = lax.dot_general(k_dec.astype(BF16), vh, (((0,), (0,)), ((), ())), preferred_element_type=F32)
        state_ref[h] = math.exp(lg * c) * st + kv
        if reverse:
            g_ref = g0_ref if h < hpb else g1_ref
            yn = y * lax.rsqrt(jnp.mean(y * y, axis=-1, keepdims=True) + EPS) * norm_ref[:, h * dv:(h + 1) * dv]
            o_ref[:, h * dv:(h + 1) * dv] = (jax.nn.silu(g_ref[:, vs]) * yn).astype(o_ref.dtype)
        else:
            o_ref[:, h * dv:(h + 1) * dv] = y


def retention_branch(proj, norm, col0, c=256):
    s = proj.shape[0]
    d_ret = norm.shape[0]
    qk = RET_HEADS * RET_KEY_DIM
    c = min(c, s)
    nc = s // c
    half = RET_KEY_DIM // 2
    inv = ROPE_BASE ** (-jnp.arange(half, dtype=F32) / half)
    inv = jnp.concatenate([inv, inv]).reshape(1, RET_KEY_DIM)
    vw = d_ret // 2
    qb, kb = col0 // qk, (col0 + qk) // qk
    vb = (col0 + 2 * qk) // vw
    gb = (col0 + 2 * qk + d_ret) // vw
    assert col0 % qk == 0 and (col0 + 2 * qk) % vw == 0
    scratch = [pltpu.VMEM((RET_HEADS, RET_KEY_DIM, d_ret // RET_HEADS), F32), pltpu.VMEM((RET_HEADS, c, c), F32)]

    def specs(ti):
        return [pl.BlockSpec((c, qk), lambda i: (ti(i), qb)), pl.BlockSpec((c, qk), lambda i: (ti(i), kb)),
                pl.BlockSpec((c, vw), lambda i: (ti(i), vb)), pl.BlockSpec((c, vw), lambda i: (ti(i), vb + 1)),
                pl.BlockSpec((1, RET_KEY_DIM), lambda i: (0, 0))]

    fwd = lambda i: i
    y1 = pl.pallas_call(
        functools.partial(_ret_kernel, reverse=False, nc=nc, c=c),
        grid=(nc,),
        in_specs=specs(fwd),
        out_specs=pl.BlockSpec((c, d_ret), lambda i: (i, 0)),
        out_shape=jax.ShapeDtypeStruct((s, d_ret), F32),
        scratch_shapes=scratch,
        compiler_params=_params("arbitrary"),
        name="ret_fwd",
    )(proj, proj, proj, proj, inv)
    rev = lambda i: nc - 1 - i
    return pl.pallas_call(
        functools.partial(_ret_kernel, reverse=True, nc=nc, c=c),
        grid=(nc,),
        in_specs=specs(rev) + [pl.BlockSpec((c, d_ret), lambda i: (rev(i), 0)),
                               pl.BlockSpec((c, vw), lambda i: (rev(i), gb)),
                               pl.BlockSpec((c, vw), lambda i: (rev(i), gb + 1)),
                               pl.BlockSpec((1, d_ret), lambda i: (0, 0))],
        out_specs=pl.BlockSpec((c, d_ret), lambda i: (rev(i), 0)),
        out_shape=jax.ShapeDtypeStruct((s, d_ret), BF16),
        scratch_shapes=scratch,
        compiler_params=_params("arbitrary"),
        name="ret_bwd",
    )(proj, proj, proj, proj, inv, y1, proj, proj, norm.reshape(1, d_ret).astype(F32))


def _hy_conv_kernel(xp_ref, x_ref, xn_ref, w_ref, b_ref, o_ref, ext_ref, *, nb, tb):
    ti = pl.program_id(1)
    o_ref[...] = _dwconv_block(xp_ref, x_ref, xn_ref, w_ref, b_ref, ext_ref, ti, nb, tb, 3)


def hyena_short_conv(proj, conv_w, conv_b, col_block, c, tb=512):
    s = proj.shape[0]
    tb = min(tb, s)
    nb = s // tb
    r = tb // V7X_SUBLANES
    last = nb * r - 1
    return pl.pallas_call(
        functools.partial(_hy_conv_kernel, nb=nb, tb=tb),
        grid=(3, nb),
        in_specs=[
            pl.BlockSpec((V7X_SUBLANES, c), lambda j, i: (jnp.maximum(i * r - 1, 0), col_block + j)),
            pl.BlockSpec((tb, c), lambda j, i: (i, col_block + j)),
            pl.BlockSpec((V7X_SUBLANES, c), lambda j, i: (jnp.minimum((i + 1) * r, last), col_block + j)),
            pl.BlockSpec((3, c), lambda j, i: (0, j)),
            pl.BlockSpec((1, c), lambda j, i: (0, j)),
        ],
        out_specs=pl.BlockSpec((None, tb, c), lambda j, i: (j, i, 0)),
        out_shape=jax.ShapeDtypeStruct((3, s, c), F32),
        scratch_shapes=[pltpu.VMEM((tb + 2 * V7X_SUBLANES, c), F32)],
        compiler_params=_params("parallel", "parallel"),
        name="hyena_short_conv",
    )(proj, proj, proj, conv_w.astype(F32), conv_b.reshape(1, -1).astype(F32))


def _hy_filter_kernel(fb_ref, w1_ref, b1_ref, w2_ref, b2_ref, w3_ref, freq_ref, decay_ref, o_ref, *, length, tb):
    i = pl.program_id(0)
    hi = lax.Precision.HIGHEST
    idx = (i * tb).astype(F32) + lax.broadcasted_iota(jnp.int32, (tb, 1), 0).astype(F32)
    t = idx / (length - 1.0)
    omega = (2.0 * math.pi / length) * idx
    lane = lax.broadcasted_iota(jnp.int32, (tb, V7X_LANES), 1)
    phase = fb_ref[...] * omega
    feats = jnp.where(lane == 0, t, jnp.where(lane <= HYENA_BANDS, jnp.cos(phase),
                                              jnp.where(lane <= 2 * HYENA_BANDS, -jnp.sin(phase), 0.0)))
    freq = freq_ref[...]
    hdn = jnp.sin(freq * (jnp.dot(feats, w1_ref[...], precision=hi, preferred_element_type=F32) + b1_ref[...]))
    hdn = jnp.sin(freq * (jnp.dot(hdn, w2_ref[...], precision=hi, preferred_element_type=F32) + b2_ref[...]))
    filt = jnp.dot(hdn, w3_ref[...], precision=hi, preferred_element_type=F32)
    o_ref[...] = filt * jnp.exp(-t * jnp.abs(decay_ref[...]))


def hyena_filters(length, w1, b1, w2, b2, w3, freq, decay, tb=512):
    emb, hid = w1.shape
    bands = (emb - 1) // 2
    assert bands == HYENA_BANDS
    n_out = w3.shape[1]
    tb = min(tb, length)
    f = jnp.linspace(1e-4, bands - 1, bands, dtype=F32)
    fb = jnp.zeros((1, V7X_LANES), F32).at[0, 1:1 + bands].set(f).at[0, 1 + bands:1 + 2 * bands].set(f)
    w1p = jnp.zeros((V7X_LANES, hid), F32).at[:emb].set(w1.astype(F32))
    full = lambda shape: pl.BlockSpec(shape, lambda i: (0,) * len(shape))
    return pl.pallas_call(
        functools.partial(_hy_filter_kernel, length=length, tb=tb),
        grid=(length // tb,),
        in_specs=[full((1, V7X_LANES)), full((V7X_LANES, hid)), full((1, hid)), full((hid, hid)), full((1, hid)),
                  full((hid, n_out)), full((1, hid)), full((1, n_out))],
        out_specs=pl.BlockSpec((tb, n_out), lambda i: (i, 0)),
        out_shape=jax.ShapeDtypeStruct((length, n_out), F32),
        compiler_params=_params("parallel"),
        name="hyena_filters",
    )(fb, w1p, b1.reshape(1, hid).astype(F32), w2.astype(F32), b2.reshape(1, hid).astype(F32), w3.astype(F32),
      freq.reshape(1, hid).astype(F32), decay.reshape(1, n_out).astype(F32))


def _fft_sizes(length):
    n = 2 * length
    n2 = min(FFT_N2, n // 4)
    n1 = n // n2
    return n, n1, n2, n1 // 2, n1 // 2 + 1


def _fft_tables(length):
    n, n1, n2, n1h, k1n = _fft_sizes(length)
    eye = np.eye(V7X_SUBLANES)
    k1 = np.arange(k1n)[:, None]
    a1 = np.arange(n1h)[None, :]
    ang1 = 2.0 * np.pi * ((k1 * a1) % n1) / n1
    f1 = np.concatenate([np.kron(np.cos(ang1), eye), np.kron(-np.sin(ang1), eye)], axis=0)
    wgt = np.full((k1n,), 2.0)
    wgt[0] = wgt[-1] = 1.0
    cw = (np.cos(ang1) * wgt[:, None] / n).T
    sw = (np.sin(ang1) * wgt[:, None] / n).T
    b3 = np.concatenate([np.kron(cw, eye), np.kron(-sw, eye)], axis=1)
    ik1 = jnp.arange(k1n, dtype=jnp.int32)[:, None, None]
    ik2 = jnp.arange(n2, dtype=jnp.int32)[None, :, None]
    in2 = jnp.arange(n2, dtype=jnp.int32)[None, None, :]
    m = (in2 * (ik1 + n1 * ik2)) & (n - 1)
    ang = m.astype(F32) * (2.0 * math.pi / n)
    cs, sn = jnp.cos(ang), jnp.sin(ang)
    t2 = jnp.concatenate([jnp.concatenate([cs, sn], -1), jnp.concatenate([-sn, cs], -1)], -2).astype(BF16)
    ct, st = jnp.swapaxes(cs, 1, 2), jnp.swapaxes(sn, 1, 2)
    t2i = jnp.concatenate([jnp.concatenate([ct, -st], -1), jnp.concatenate([st, ct], -1)], -2).astype(BF16)
    return jnp.asarray(f1, BF16), t2, t2i, jnp.asarray(b3, BF16)


def _fft1_kernel(z_ref, f1_ref, ar_ref, ai_ref):
    n1h, sub, cb = z_ref.shape
    k1n = ar_ref.shape[0]
    z = z_ref[...].reshape(n1h * sub, cb).astype(BF16)
    a = _dot(f1_ref[...], z)
    ar_ref[...] = a[:k1n * sub].reshape(k1n, sub, cb)
    ai_ref[...] = a[k1n * sub:].reshape(k1n, sub, cb)


def fft_stage1(z, f1, length, cb=1024):
    _, _, n2, n1h, k1n = _fft_sizes(length)
    c = z.shape[-1]
    cb = min(cb, c)
    lead = z.shape[:-2]
    z4 = z.reshape(lead + (n1h, n2, c))
    nl = len(lead)
    sub = V7X_SUBLANES
    out = jax.ShapeDtypeStruct((k1n, n2, c), F32)
    return pl.pallas_call(
        _fft1_kernel,
        grid=(n2 // sub, c // cb),
        in_specs=[pl.BlockSpec((None,) * nl + (n1h, sub, cb), lambda j, ci: (0,) * nl + (0, j, ci)),
                  pl.BlockSpec(f1.shape, lambda j, ci: (0, 0))],
        out_specs=[pl.BlockSpec((k1n, sub, cb), lambda j, ci: (0, j, ci))] * 2,
        out_shape=[out, out],
        compiler_params=_params("parallel", "parallel"),
        name="fft_stage1",
    )(z4, f1)


def _slab_dft(t_ref, xr, xi):
    n2 = xr.shape[0]
    x = jnp.concatenate([xr.astype(BF16), xi.astype(BF16)], axis=0)
    y = _dot(t_ref[...], x)
    return y[:n2], y[n2:]


def _filter_spec_kernel(afr_ref, afi_ref, abr_ref, abi_ref, t2_ref, gr_ref, gi_ref):
    fr, fi = _slab_dft(t2_ref, afr_ref[...], afi_ref[...])
    br, bi = _slab_dft(t2_ref, abr_ref[...], abi_ref[...])
    gr_ref[...] = fr + br
    gi_ref[...] = fi - bi


def filter_spectrum(ar, ai, t2, c, cb=512):
    k1n, n2, cf = ar.shape
    orders = cf // (2 * c)
    cb = min(cb, c)
    per = c // cb
    fcol = lambda k, j: (k, 0, (j // per) * 2 * per + j % per)
    bcol = lambda k, j: (k, 0, (j // per) * 2 * per + per + j % per)
    blk = (None, n2, cb)
    out = jax.ShapeDtypeStruct((k1n, n2, orders * c), F32)
    return pl.pallas_call(
        _filter_spec_kernel,
        grid=(k1n, orders * per),
        in_specs=[pl.BlockSpec(blk, fcol), pl.BlockSpec(blk, fcol), pl.BlockSpec(blk, bcol), pl.BlockSpec(blk, bcol),
                  pl.BlockSpec((None, 2 * n2, 2 * n2), lambda k, j: (k, 0, 0))],
        out_specs=[pl.BlockSpec(blk, lambda k, j: (k, 0, j))] * 2,
        out_shape=[out, out],
        compiler_params=_params("parallel", "parallel"),
        name="hyena_filter_spectrum",
    )(ar, ai, ar, ai, t2)


def _slab_conv_kernel(ar_ref, ai_ref, gr_ref, gi_ref, t2_ref, t2i_ref, pr_ref, pi_ref):
    xr, xi = _slab_dft(t2_ref, ar_ref[...], ai_ref[...])
    gr, gi = gr_ref[...], gi_ref[...]
    pr, pi = _slab_dft(t2i_ref, xr * gr - xi * gi, xr * gi + xi * gr)
    pr_ref[...] = pr
    pi_ref[...] = pi


def slab_conv(ar, ai, gr, gi, t2, t2i, order, cb=512):
    k1n, n2, c = ar.shape
    cb = min(cb, c)
    per = c // cb
    blk = (None, n2, cb)
    dcol = lambda k, j: (k, 0, j)
    gcol = lambda k, j: (k, 0, order * per + j)
    tcol = lambda k, j: (k, 0, 0)
    out = jax.ShapeDtypeStruct((k1n, n2, c), F32)
    return pl.pallas_call(
        _slab_conv_kernel,
        grid=(k1n, per),
        in_specs=[pl.BlockSpec(blk, dcol), pl.BlockSpec(blk, dcol), pl.BlockSpec(blk, gcol), pl.BlockSpec(blk, gcol),
                  pl.BlockSpec((None, 2 * n2, 2 * n2), tcol), pl.BlockSpec((None, 2 * n2, 2 * n2), tcol)],
        out_specs=[pl.BlockSpec(blk, dcol)] * 2,
        out_shape=[out, out],
        compiler_params=_params("parallel", "parallel"),
        name="hyena_slab_conv",
    )(ar, ai, gr, gi, t2, t2i)


def _ifft3_kernel(pr_ref, pi_ref, b3_ref, z_ref, gate_ref, skip_ref, o_ref):
    k1n, sub, cb = pr_ref.shape
    n1h = o_ref.shape[0]
    p = jnp.concatenate([pr_ref[...].reshape(k1n * sub, cb).astype(BF16),
                         pi_ref[...].reshape(k1n * sub, cb).astype(BF16)], axis=0)
    y = _dot(b3_ref[...], p).reshape(n1h, sub, cb)
    z = z_ref[...]
    o_ref[...] = gate_ref[...] * (y + skip_ref[...] * z)


def ifft_stage3_gate(pr, pi, b3, u, z_arr, gate_idx, skip, length, cb=1024):
    _, _, n2, n1h, k1n = _fft_sizes(length)
    c = pr.shape[-1]
    cb = min(cb, c)
    sub = V7X_SUBLANES
    u4 = u.reshape(u.shape[0], n1h, n2, c)
    z4 = z_arr.reshape((-1, n1h, n2, c))
    pblk = pl.BlockSpec((k1n, sub, cb), lambda j, ci: (0, j, ci))
    out = pl.pallas_call(
        _ifft3_kernel,
        grid=(n2 // sub, c // cb),
        in_specs=[pblk, pblk, pl.BlockSpec(b3.shape, lambda j, ci: (0, 0)),
                  pl.BlockSpec((None, n1h, sub, cb), lambda j, ci: (0, 0, j, ci)),
                  pl.BlockSpec((None, n1h, sub, cb), lambda j, ci: (gate_idx, 0, j, ci)),
                  pl.BlockSpec((1, cb), lambda j, ci: (0, ci))],
        out_specs=pl.BlockSpec((n1h, sub, cb), lambda j, ci: (0, j, ci)),
        out_shape=jax.ShapeDtypeStruct((n1h, n2, c), F32),
        compiler_params=_params("parallel", "parallel"),
        name="ifft_stage3_gate",
    )(pr, pi, b3, z4, u4, skip.reshape(1, c).astype(F32))
    return out.reshape(1, length, c)


def hyena_branch(proj, tables, conv_w, conv_b, f_w1, f_b1, f_w2, f_b2, f_w3, freq, decay, skip, norm, col_block):
    length = proj.shape[0]
    c = norm.shape[0]
    f1, t2, t2i, b3 = tables
    u = hyena_short_conv(proj, conv_w, conv_b, col_block, c)
    filt = hyena_filters(length, f_w1, f_b1, f_w2, f_b2, f_w3, freq, decay)
    far, fai = fft_stage1(filt, f1, length)
    gr, gi = filter_spectrum(far, fai, t2, c)
    z = u
    for o in range(skip.shape[0]):
        ar, ai = fft_stage1(z, f1, length)
        pr, pi = slab_conv(ar, ai, gr, gi, t2, t2i, o)
        z = ifft_stage3_gate(pr, pi, b3, u, z, 1 + o, skip[o], length)
    return group_norm(z.reshape(length, c), norm, HYENA_GROUPS, BF16)


def _pad_cols(w, n):
    return jnp.pad(w, ((0, 0), (0, n - w.shape[1])))


def kernel(x, norm_mix, w_in, lru_conv_w, lru_conv_b, lru_wr, lru_br, lru_wi, lru_bi, lru_lambda, lru_norm, hy_conv_w, hy_conv_b, hy_f_w1, hy_f_b1, hy_f_w2, hy_f_b2, hy_f_w3, hy_freq, hy_decay, hy_skip, hy_norm, ret_norm, w_out, norm_ffn, w_gate, w_up, w_down, norm_final):
    b, s, d = x.shape
    assert b == 1
    depth = w_in.shape[0]
    d_lru = lru_conv_w.shape[-1]
    d_hy = hy_norm.shape[-1]
    d_ff = w_gate.shape[-1]
    d_ff_pad = -(-d_ff // 1024) * 1024
    tables = _fft_tables(s)
    xs = x.reshape(s, d)
    for l in range(depth):
        h = rmsnorm(xs, norm_mix[l], BF16)
        proj = matmul(h, w_in[l].astype(BF16))
        y_a = lru_branch(proj, lru_conv_w[l], lru_conv_b[l], lru_wr[l], lru_br[l], lru_wi[l], lru_bi[l],
                         lru_lambda[l], lru_norm[l])
        y_b = hyena_branch(proj, tables, hy_conv_w[l], hy_conv_b[l], hy_f_w1[l], hy_f_b1[l], hy_f_w2[l],
                           hy_f_b2[l], hy_f_w3[l], hy_freq[l], hy_decay[l], hy_skip[l], hy_norm[l],
                           (2 * d_lru) // d_hy)
        y_c = retention_branch(proj, ret_norm[l], 2 * d_lru + 3 * d_hy)
        mix = jnp.concatenate([y_a, y_b, y_c], axis=-1)
        xs = matmul(mix, w_out[l].astype(BF16), residual=xs)
        h = rmsnorm(xs, norm_ffn[l], BF16)
        act = ffn_up(h, _pad_cols(w_gate[l].astype(BF16), d_ff_pad), _pad_cols(w_up[l].astype(BF16), d_ff_pad))
        w_dn = jnp.pad(w_down[l].astype(BF16), ((0, d_ff_pad - d_ff), (0, 0)))
        xs = matmul(act, w_dn, residual=xs)
    return rmsnorm(xs, norm_final, x.dtype).reshape(b, s, d)
```

```python
import functools
import math

import numpy as np
import jax
import jax.numpy as jnp
from jax import lax
from jax.experimental import pallas as pl
from jax.experimental.pallas import tpu as pltpu

EPS = 1e-6
LRU_HEADS = 8
LRU_C = 8.0
HYENA_GROUPS = 8
HYENA_BANDS = 16
RET_HEADS = 8
RET_KEY_DIM = 128
ROPE_BASE = 10000.0

V7X_SUBLANES = 8
V7X_LANES = 128
VMEM_LIMIT_BYTES = 56 * 1024 * 1024
FFT_N2 = 256
F32 = jnp.float32
BF16 = jnp.bfloat16


def _params(*sem):
    return pltpu.CompilerParams(dimension_semantics=sem, vmem_limit_bytes=VMEM_LIMIT_BYTES)


def _dot(a, b):
    return jnp.dot(a, b, preferred_element_type=F32)


def _rmsnorm_kernel(x_ref, g_ref, o_ref):
    x = x_ref[...]
    y = x * lax.rsqrt(jnp.mean(x * x, axis=-1, keepdims=True) + EPS)
    o_ref[...] = (y * g_ref[...]).astype(o_ref.dtype)


def rmsnorm(x, gain, out_dtype, tm=256):
    s, d = x.shape
    tm = min(tm, s)
    return pl.pallas_call(
        _rmsnorm_kernel,
        grid=(s // tm,),
        in_specs=[pl.BlockSpec((tm, d), lambda i: (i, 0)), pl.BlockSpec((1, d), lambda i: (0, 0))],
        out_specs=pl.BlockSpec((tm, d), lambda i: (i, 0)),
        out_shape=jax.ShapeDtypeStruct((s, d), out_dtype),
        compiler_params=_params("parallel"),
        name="rmsnorm",
    )(x, gain.reshape(1, d).astype(F32))


def _group_norm_kernel(x_ref, g_ref, o_ref, *, groups):
    x = x_ref[...]
    w = x.shape[-1] // groups
    for h in range(groups):
        xh = x[:, h * w:(h + 1) * w]
        yh = xh * lax.rsqrt(jnp.mean(xh * xh, axis=-1, keepdims=True) + EPS)
        o_ref[:, h * w:(h + 1) * w] = (yh * g_ref[:, h * w:(h + 1) * w]).astype(o_ref.dtype)


def group_norm(x, gain, groups, out_dtype, tm=512):
    s, d = x.shape
    tm = min(tm, s)
    return pl.pallas_call(
        functools.partial(_group_norm_kernel, groups=groups),
        grid=(s // tm,),
        in_specs=[pl.BlockSpec((tm, d), lambda i: (i, 0)), pl.BlockSpec((1, d), lambda i: (0, 0))],
        out_specs=pl.BlockSpec((tm, d), lambda i: (i, 0)),
        out_shape=jax.ShapeDtypeStruct((s, d), out_dtype),
        compiler_params=_params("parallel"),
        name="group_norm",
    )(x, gain.reshape(1, d).astype(F32))


def _mm_fullk_kernel(*refs, n_a, has_res):
    a_refs, b_ref, o_ref = refs[:n_a], refs[n_a], refs[-1]
    acc, off = None, 0
    for a_ref in a_refs:
        kw = a_ref.shape[1]
        part = _dot(a_ref[...], b_ref[off:off + kw, :])
        acc = part if acc is None else acc + part
        off += kw
    if has_res:
        acc = refs[n_a + 1][...] + acc
    o_ref[...] = acc.astype(o_ref.dtype)


def matmul_fullk(a_parts, b, residual=None, out_dtype=F32, tm=1024, tn=1024):
    m = a_parts[0].shape[0]
    kd, n = b.shape
    assert sum(a.shape[1] for a in a_parts) == kd
    tm, tn = min(tm, m), min(tn, n)
    has_res = residual is not None
    in_specs = [pl.BlockSpec((tm, a.shape[1]), lambda i, j: (i, 0)) for a in a_parts]
    in_specs.append(pl.BlockSpec((kd, tn), lambda i, j: (0, j)))
    args = list(a_parts) + [b]
    if has_res:
        in_specs.append(pl.BlockSpec((tm, tn), lambda i, j: (i, j)))
        args.append(residual)
    return pl.pallas_call(
        functools.partial(_mm_fullk_kernel, n_a=len(a_parts), has_res=has_res),
        grid=(m // tm, n // tn),
        in_specs=in_specs,
        out_specs=pl.BlockSpec((tm, tn), lambda i, j: (i, j)),
        out_shape=jax.ShapeDtypeStruct((m, n), out_dtype),
        compiler_params=_params("parallel", "parallel"),
        name="matmul_fullk_res" if has_res else "matmul_fullk",
    )(*args)


def _mm_kgrid_kernel(a_ref, b_ref, r_ref, o_ref, acc_ref):
    k = pl.program_id(2)

    @pl.when(k == 0)
    def _():
        acc_ref[...] = jnp.zeros_like(acc_ref)

    acc_ref[...] += _dot(a_ref[...], b_ref[...])

    @pl.when(k == pl.num_programs(2) - 1)
    def _():
        o_ref[...] = (r_ref[...] + acc_ref[...]).astype(o_ref.dtype)


def matmul_kgrid(a, b, residual, out_dtype=F32, tm=1024, tn=1024, tk=2816):
    m, kd = a.shape
    n = b.shape[1]
    tm, tn, tk = min(tm, m), min(tn, n), min(tk, kd)
    assert kd % tk == 0
    return pl.pallas_call(
        _mm_kgrid_kernel,
        grid=(m // tm, n // tn, kd // tk),
        in_specs=[pl.BlockSpec((tm, tk), lambda i, j, k: (i, k)), pl.BlockSpec((tk, tn), lambda i, j, k: (k, j)),
                  pl.BlockSpec((tm, tn), lambda i, j, k: (i, j))],
        out_specs=pl.BlockSpec((tm, tn), lambda i, j, k: (i, j)),
        out_shape=jax.ShapeDtypeStruct((m, n), out_dtype),
        scratch_shapes=[pltpu.VMEM((tm, tn), F32)],
        compiler_params=_params("parallel", "parallel", "arbitrary"),
        name="matmul_kgrid_res",
    )(a, b, residual)


def _ffn_up_kernel(h_ref, wg_ref, wu_ref, o_ref, *, n_valid):
    h = h_ref[...]
    g = _dot(h, wg_ref[...])
    u = _dot(h, wu_ref[...])
    tn = o_ref.shape[1]
    col = pl.program_id(1) * tn + lax.broadcasted_iota(jnp.int32, g.shape, 1)
    o_ref[...] = jnp.where(col < n_valid, jax.nn.silu(g) * u, 0.0).astype(o_ref.dtype)


def ffn_up(h, wg, wu, n_out, tm=1024, tn=512):
    m, kd = h.shape
    n = wg.shape[1]
    tm, tn = min(tm, m), min(tn, n_out)
    assert n_out % tn == 0 and n_out - n < tn
    return pl.pallas_call(
        functools.partial(_ffn_up_kernel, n_valid=n),
        grid=(m // tm, n_out // tn),
        in_specs=[pl.BlockSpec((tm, kd), lambda i, j: (i, 0)), pl.BlockSpec((kd, tn), lambda i, j: (0, j)),
                  pl.BlockSpec((kd, tn), lambda i, j: (0, j))],
        out_specs=pl.BlockSpec((tm, tn), lambda i, j: (i, j)),
        out_shape=jax.ShapeDtypeStruct((m, n_out), BF16),
        compiler_params=_params("parallel", "parallel"),
        name="ffn_up",
    )(h, wg, wu)


def _halo_specs(tb, width, col, nb, rev=False):
    r = tb // V7X_SUBLANES
    last = nb * r - 1

    def ti(i):
        return (nb - 1 - i) if rev else i

    return [
        pl.BlockSpec((V7X_SUBLANES, width), lambda i: (jnp.maximum(ti(i) * r - 1, 0), col)),
        pl.BlockSpec((tb, width), lambda i: (ti(i), col)),
        pl.BlockSpec((V7X_SUBLANES, width), lambda i: (jnp.minimum((ti(i) + 1) * r, last), col)),
    ]


def _dwconv_block(prev_ref, x_ref, next_ref, w_ref, b_ref, ext_ref, ti, nb, tb, width):
    left = width // 2
    prev = jnp.where(ti == 0, 0.0, prev_ref[...])
    nxt = jnp.where(ti == nb - 1, 0.0, next_ref[...])
    ext_ref[0:V7X_SUBLANES, :] = prev
    ext_ref[V7X_SUBLANES:V7X_SUBLANES + tb, :] = x_ref[...]
    ext_ref[V7X_SUBLANES + tb:2 * V7X_SUBLANES + tb, :] = nxt
    y = b_ref[...]
    for j in range(width):
        y = y + ext_ref[pl.ds(V7X_SUBLANES - left + j, tb), :] * w_ref[j:j + 1, :]
    return y


def _softplus(x):
    return jnp.maximum(x, 0.0) + jnp.log1p(jnp.exp(-jnp.abs(x)))


def _lru_kernel(*refs, reverse, final, nb, tb):
    if final:
        (xp_ref, x_ref, xn_ref, cw_ref, cb_ref, w_ref, gb_ref, lam_ref, hf_ref, gate_ref, norm_ref,
         o_ref, ext_ref, a_ref, b_ref, carry_ref) = refs
    else:
        (xp_ref, x_ref, xn_ref, cw_ref, cb_ref, w_ref, gb_ref, lam_ref,
         o_ref, ext_ref, a_ref, b_ref, carry_ref) = refs
    i = pl.program_id(0)
    ti = (nb - 1 - i) if reverse else i
    hd = V7X_LANES
    width = x_ref.shape[-1]

    @pl.when(i == 0)
    def _():
        carry_ref[...] = jnp.zeros_like(carry_ref)

    xc = _dwconv_block(xp_ref, x_ref, xn_ref, cw_ref, cb_ref, ext_ref, ti, nb, tb, 4)
    sp = _softplus(-lam_ref[...])
    for h in range(LRU_HEADS):
        sl = slice(h * hd, (h + 1) * hd)
        xh = xc[:, sl]
        z = _dot(xh.astype(BF16), w_ref[h]) + gb_ref[h]
        r = jax.nn.sigmoid(z[:, :hd])
        ig = jax.nn.sigmoid(z[:, hd:])
        log_a = -LRU_C * r * sp[:, sl]
        a = jnp.exp(log_a)
        a_ref[:, sl] = a
        b_ref[:, sl] = jnp.sqrt(-jnp.tanh(log_a) * (a * a + 1.0)) * (ig * xh)

    ng = tb // V7X_SUBLANES
    row = lax.broadcasted_iota(jnp.int32, (V7X_SUBLANES, width), 0)

    def body(g, carry):
        gi = (ng - 1 - g) if reverse else g
        off = pl.multiple_of(gi * V7X_SUBLANES, V7X_SUBLANES)
        a = a_ref[pl.ds(off, V7X_SUBLANES), :]
        b = b_ref[pl.ds(off, V7X_SUBLANES), :]
        for s in (1, 2, 4):
            if reverse:
                shift, m = V7X_SUBLANES - s, row < V7X_SUBLANES - s
            else:
                shift, m = s, row >= s
            b = jnp.where(m, a * pltpu.roll(b, shift, 0) + b, b)
            a = jnp.where(m, a * pltpu.roll(a, shift, 0), a)
        hcur = a * carry + b
        b_ref[pl.ds(off, V7X_SUBLANES), :] = hcur
        return hcur[0:1, :] if reverse else hcur[V7X_SUBLANES - 1:V7X_SUBLANES, :]

    carry_ref[...] = lax.fori_loop(0, ng, body, carry_ref[...])

    if not final:
        o_ref[...] = b_ref[...]
    else:
        y = jax.nn.gelu(gate_ref[...], approximate=True) * (hf_ref[...] + b_ref[...])
        for h in range(LRU_HEADS):
            sl = slice(h * hd, (h + 1) * hd)
            yh = y[:, sl]
            yh = yh * lax.rsqrt(jnp.mean(yh * yh, axis=-1, keepdims=True) + EPS)
            o_ref[:, sl] = (yh * norm_ref[:, sl]).astype(o_ref.dtype)


def lru_branch(proj, conv_w, conv_b, wr, br, wi, bi, lam, norm, tb=512):
    s = proj.shape[0]
    c = conv_w.shape[1]
    hd = c // LRU_HEADS
    tb = min(tb, s)
    nb = s // tb
    w = jnp.concatenate([wr, wi], axis=-1).astype(BF16)
    gb = jnp.concatenate([br.reshape(2, LRU_HEADS, 1, hd), bi.reshape(2, LRU_HEADS, 1, hd)], axis=-1).astype(F32)
    full = lambda shape: pl.BlockSpec(shape, lambda i: (0,) * len(shape))
    scratch = [pltpu.VMEM((tb + 2 * V7X_SUBLANES, c), F32), pltpu.VMEM((tb, c), F32), pltpu.VMEM((tb, c), F32),
               pltpu.VMEM((1, c), F32)]
    common = [full((4, c)), full((1, c)), full((LRU_HEADS, hd, 2 * hd)), full((LRU_HEADS, 1, 2 * hd)), full((1, c))]
    cw, cb = conv_w.astype(F32), conv_b.reshape(1, c).astype(F32)
    h_fwd = pl.pallas_call(
        functools.partial(_lru_kernel, reverse=False, final=False, nb=nb, tb=tb),
        grid=(nb,),
        in_specs=_halo_specs(tb, c, 1, nb) + common,
        out_specs=pl.BlockSpec((tb, c), lambda i: (i, 0)),
        out_shape=jax.ShapeDtypeStruct((s, c), F32),
        scratch_shapes=scratch,
        compiler_params=_params("arbitrary"),
        name="lru_fwd",
    )(proj, proj, proj, cw, cb, w[0], gb[0], lam[0].reshape(1, c).astype(F32))
    rev = lambda i: (nb - 1 - i, 0)
    return pl.pallas_call(
        functools.partial(_lru_kernel, reverse=True, final=True, nb=nb, tb=tb),
        grid=(nb,),
        in_specs=_halo_specs(tb, c, 1, nb, rev=True) + common
        + [pl.BlockSpec((tb, c), rev), pl.BlockSpec((tb, c), rev), full((1, c))],
        out_specs=pl.BlockSpec((tb, c), rev),
        out_shape=jax.ShapeDtypeStruct((s, c), BF16),
        scratch_shapes=scratch,
        compiler_params=_params("arbitrary"),
        name="lru_bwd",
    )(proj, proj, proj, cw, cb, w[1], gb[1], lam[1].reshape(1, c).astype(F32), h_fwd, proj,
      norm.reshape(1, c).astype(F32))


def _ret_log_gamma():
    return [float(np.log1p(-np.exp2(np.float32(-5.0 - h)), dtype=np.float32)) for h in range(RET_HEADS)]


def _ret_kernel(*refs, reverse, nc, c):
    if reverse:
        (q_ref, k_ref, v0_ref, v1_ref, inv_ref, y1_ref, g0_ref, g1_ref, norm_ref,
         o_ref, state_ref, dm_ref) = refs
    else:
        q_ref, k_ref, v0_ref, v1_ref, inv_ref, o_ref, state_ref, dm_ref = refs
    i = pl.program_id(0)
    ti = (nc - 1 - i) if reverse else i
    dk = RET_KEY_DIM
    dv = v0_ref.shape[-1] * 2 // RET_HEADS
    log_g = _ret_log_gamma()
    idx = lax.broadcasted_iota(jnp.int32, (c, 1), 0).astype(F32)

    @pl.when(i == 0)
    def _():
        state_ref[...] = jnp.zeros_like(state_ref)
        if not reverse:
            d = jnp.abs(lax.broadcasted_iota(jnp.int32, (c, c), 0)
                        - lax.broadcasted_iota(jnp.int32, (c, c), 1)).astype(F32)
            for h in range(RET_HEADS):
                dm_ref[h] = jnp.exp(log_g[h] * d)

    pos = (ti * c).astype(F32) + idx
    ang = pos * inv_ref[...]
    lane = lax.broadcasted_iota(jnp.int32, (c, dk), 1)
    cos_t = jnp.cos(ang)
    sin_t = jnp.where(lane < dk // 2, -1.0, 1.0) * jnp.sin(ang)

    def rot(x):
        return x * cos_t + pltpu.roll(x, dk // 2, 1) * sin_t

    hpb = RET_HEADS // 2
    for h in range(RET_HEADS):
        qh = rot(q_ref[:, h * dk:(h + 1) * dk])
        kh = rot(k_ref[:, h * dk:(h + 1) * dk]) * (dk ** -0.5)
        v_ref = v0_ref if h < hpb else v1_ref
        vs = slice((h % hpb) * dv, (h % hpb + 1) * dv)
        vh = v_ref[:, vs].astype(BF16)
        lg = log_g[h]
        if reverse:
            q_dec = qh * jnp.exp(lg * (c - idx))
            k_dec = kh * jnp.exp(lg * idx)
            y = y1_ref[:, h * dv:(h + 1) * dv]
        else:
            q_dec = qh * jnp.exp(lg * (idx + 1.0))
            k_dec = kh * jnp.exp(lg * (c - 1.0 - idx))
            scores = lax.dot_general(qh.astype(BF16), kh.astype(BF16), (((1,), (1,)), ((), ())),
                                     preferred_element_type=F32) * dm_ref[h]
            y = _dot(scores.astype(BF16), vh)
        st = state_ref[h]
        y = y + _dot(q_dec.astype(BF16), st.astype(BF16))
        kv = lax.dot_general(k_dec.astype(BF16), vh, (((0,), (0,)), ((), ())), preferred_element_type=F32)
        state_ref[h] = math.exp(lg * c) * st + kv
        if reverse:
            g_ref = g0_ref if h < hpb else g1_ref
            yn = y * lax.rsqrt(jnp.mean(y * y, axis=-1, keepdims=True) + EPS) * norm_ref[:, h * dv:(h + 1) * dv]
            o_ref[:, h * dv:(h + 1) * dv] = (jax.nn.silu(g_ref[:, vs]) * yn).astype(o_ref.dtype)
        else:
            o_ref[:, h * dv:(h + 1) * dv] = y


def retention_branch(proj, norm, col0, c=256):
    s = proj.shape[0]
    d_ret = norm.shape[0]
    qk = RET_HEADS * RET_KEY_DIM
    c = min(c, s)
    nc = s // c
    half = RET_KEY_DIM // 2
    inv = ROPE_BASE ** (-jnp.arange(half, dtype=F32) / half)
    inv = jnp.concatenate([inv, inv]).reshape(1, RET_KEY_DIM)
    vw = d_ret // 2
    qb, kb = col0 // qk, (col0 + qk) // qk
    vb = (col0 + 2 * qk) // vw
    gb = (col0 + 2 * qk + d_ret) // vw
    assert col0 % qk == 0 and (col0 + 2 * qk) % vw == 0
    scratch = [pltpu.VMEM((RET_HEADS, RET_KEY_DIM, d_ret // RET_HEADS), F32), pltpu.VMEM((RET_HEADS, c, c), F32)]

    def specs(ti):
        return [pl.BlockSpec((c, qk), lambda i: (ti(i), qb)), pl.BlockSpec((c, qk), lambda i: (ti(i), kb)),
                pl.BlockSpec((c, vw), lambda i: (ti(i), vb)), pl.BlockSpec((c, vw), lambda i: (ti(i), vb + 1)),
                pl.BlockSpec((1, RET_KEY_DIM), lambda i: (0, 0))]

    fwd = lambda i: i
    y1 = pl.pallas_call(
        functools.partial(_ret_kernel, reverse=False, nc=nc, c=c),
        grid=(nc,),
        in_specs=specs(fwd),
        out_specs=pl.BlockSpec((c, d_ret), lambda i: (i, 0)),
        out_shape=jax.ShapeDtypeStruct((s, d_ret), F32),
        scratch_shapes=scratch,
        compiler_params=_params("arbitrary"),
        name="ret_fwd",
    )(proj, proj, proj, proj, inv)
    rev = lambda i: nc - 1 - i
    return pl.pallas_call(
        functools.partial(_ret_kernel, reverse=True, nc=nc, c=c),
        grid=(nc,),
        in_specs=specs(rev) + [pl.BlockSpec((c, d_ret), lambda i: (rev(i), 0)),
                               pl.BlockSpec((c, vw), lambda i: (rev(i), gb)),
                               pl.BlockSpec((c, vw), lambda i: (rev(i), gb + 1)),
                               pl.BlockSpec((1, d_ret), lambda i: (0, 0))],
        out_specs=pl.BlockSpec((c, d_ret), lambda i: (rev(i), 0)),
        out_shape=jax.ShapeDtypeStruct((s, d_ret), BF16),
        scratch_shapes=scratch,
        compiler_params=_params("arbitrary"),
        name="ret_bwd",
    )(proj, proj, proj, proj, inv, y1, proj, proj, norm.reshape(1, d_ret).astype(F32))


def _hy_conv_kernel(xp_ref, x_ref, xn_ref, w_ref, b_ref, o_ref, ext_ref, *, nb, tb):
    ti = pl.program_id(1)
    o_ref[...] = _dwconv_block(xp_ref, x_ref, xn_ref, w_ref, b_ref, ext_ref, ti, nb, tb, 3)


def hyena_short_conv(proj, conv_w, conv_b, col_block, c, tb=512):
    s = proj.shape[0]
    tb = min(tb, s)
    nb = s // tb
    r = tb // V7X_SUBLANES
    last = nb * r - 1
    return pl.pallas_call(
        functools.partial(_hy_conv_kernel, nb=nb, tb=tb),
        grid=(3, nb),
        in_specs=[
            pl.BlockSpec((V7X_SUBLANES, c), lambda j, i: (jnp.maximum(i * r - 1, 0), col_block + j)),
            pl.BlockSpec((tb, c), lambda j, i: (i, col_block + j)),
            pl.BlockSpec((V7X_SUBLANES, c), lambda j, i: (jnp.minimum((i + 1) * r, last), col_block + j)),
            pl.BlockSpec((3, c), lambda j, i: (0, j)),
            pl.BlockSpec((1, c), lambda j, i: (0, j)),
        ],
        out_specs=pl.BlockSpec((None, tb, c), lambda j, i: (j, i, 0)),
        out_shape=jax.ShapeDtypeStruct((3, s, c), F32),
        scratch_shapes=[pltpu.VMEM((tb + 2 * V7X_SUBLANES, c), F32)],
        compiler_params=_params("parallel", "parallel"),
        name="hyena_short_conv",
    )(proj, proj, proj, conv_w.astype(F32), conv_b.reshape(1, -1).astype(F32))


def _hy_filter_kernel(fb_ref, w1_ref, b1_ref, w2_ref, b2_ref, w3_ref, freq_ref, decay_ref, o_ref, *, length, tb):
    i = pl.program_id(0)
    hi = lax.Precision.HIGHEST
    idx = (i * tb).astype(F32) + lax.broadcasted_iota(jnp.int32, (tb, 1), 0).astype(F32)
    t = idx / (length - 1.0)
    omega = (2.0 * math.pi / length) * idx
    lane = lax.broadcasted_iota(jnp.int32, (tb, V7X_LANES), 1)
    phase = fb_ref[...] * omega
    feats = jnp.where(lane == 0, t, jnp.where(lane <= HYENA_BANDS, jnp.cos(phase),
                                              jnp.where(lane <= 2 * HYENA_BANDS, -jnp.sin(phase), 0.0)))
    freq = freq_ref[...]
    hdn = jnp.sin(freq * (jnp.dot(feats, w1_ref[...], precision=hi, preferred_element_type=F32) + b1_ref[...]))
    hdn = jnp.sin(freq * (jnp.dot(hdn, w2_ref[...], precision=hi, preferred_element_type=F32) + b2_ref[...]))
    filt = _dot(hdn.astype(BF16), w3_ref[...])
    o_ref[...] = filt * jnp.exp(-t * jnp.abs(decay_ref[...]))


def hyena_filters(length, w1, b1, w2, b2, w3, freq, decay, tb=512):
    emb, hid = w1.shape
    bands = (emb - 1) // 2
    assert bands == HYENA_BANDS
    n_out = w3.shape[1]
    tb = min(tb, length)
    f = jnp.linspace(1e-4, bands - 1, bands, dtype=F32)
    fb = jnp.zeros((1, V7X_LANES), F32).at[0, 1:1 + bands].set(f).at[0, 1 + bands:1 + 2 * bands].set(f)
    w1p = jnp.zeros((V7X_LANES, hid), F32).at[:emb].set(w1.astype(F32))
    full = lambda shape: pl.BlockSpec(shape, lambda i: (0,) * len(shape))
    return pl.pallas_call(
        functools.partial(_hy_filter_kernel, length=length, tb=tb),
        grid=(length // tb,),
        in_specs=[full((1, V7X_LANES)), full((V7X_LANES, hid)), full((1, hid)), full((hid, hid)), full((1, hid)),
                  full((hid, n_out)), full((1, hid)), full((1, n_out))],
        out_specs=pl.BlockSpec((tb, n_out), lambda i: (i, 0)),
        out_shape=jax.ShapeDtypeStruct((length, n_out), F32),
        compiler_params=_params("parallel"),
        name="hyena_filters",
    )(fb, w1p, b1.reshape(1, hid).astype(F32), w2.astype(F32), b2.reshape(1, hid).astype(F32), w3.astype(BF16),
      freq.reshape(1, hid).astype(F32), decay.reshape(1, n_out).astype(F32))


def _fft_sizes(length):
    n = 2 * length
    n2 = min(FFT_N2, n // 4)
    n1 = n // n2
    return n, n1, n2, n1 // 2, n1 // 2 + 1


def _fft_tables(length):
    n, n1, n2, n1h, k1n = _fft_sizes(length)
    eye = np.eye(V7X_SUBLANES)
    k1 = np.arange(k1n)[:, None]
    a1 = np.arange(n1h)[None, :]
    ang1 = 2.0 * np.pi * ((k1 * a1) % n1) / n1
    f1 = np.concatenate([np.kron(np.cos(ang1), eye), np.kron(-np.sin(ang1), eye)], axis=0)
    wgt = np.full((k1n,), 2.0)
    wgt[0] = wgt[-1] = 1.0
    cw = (np.cos(ang1) * wgt[:, None] / n).T
    sw = (np.sin(ang1) * wgt[:, None] / n).T
    b3 = np.concatenate([np.kron(cw, eye), np.kron(-sw, eye)], axis=1)
    ik1 = jnp.arange(k1n, dtype=jnp.int32)[:, None, None]
    ik2 = jnp.arange(n2, dtype=jnp.int32)[None, :, None]
    in2 = jnp.arange(n2, dtype=jnp.int32)[None, None, :]
    m = (in2 * (ik1 + n1 * ik2)) & (n - 1)
    ang = m.astype(F32) * (2.0 * math.pi / n)
    cs, sn = jnp.cos(ang), jnp.sin(ang)
    t2 = jnp.concatenate([jnp.concatenate([cs, sn], -1), jnp.concatenate([-sn, cs], -1)], -2).astype(BF16)
    ct, st = jnp.swapaxes(cs, 1, 2), jnp.swapaxes(sn, 1, 2)
    t2i = jnp.concatenate([jnp.concatenate([ct, -st], -1), jnp.concatenate([st, ct], -1)], -2).astype(BF16)
    return jnp.asarray(f1, BF16), t2, t2i, jnp.asarray(b3, BF16)


def _pack_pair(re, im):
    hi = lax.bitcast_convert_type(re.astype(BF16).astype(F32), jnp.uint32)
    lo = lax.bitcast_convert_type(im.astype(BF16).astype(F32), jnp.uint32)
    return hi | (lo >> 16)


def _unpack_pair(w):
    re = lax.bitcast_convert_type(w & jnp.uint32(0xFFFF0000), F32)
    im = lax.bitcast_convert_type(w << 16, F32)
    return re, im


def _fft1_kernel(z_ref, f1_ref, a_ref):
    n1h, sub, cb = z_ref.shape
    k1n = a_ref.shape[0]
    z = z_ref[...].reshape(n1h * sub, cb).astype(BF16)
    a = _dot(f1_ref[...], z)
    a_ref[...] = _pack_pair(a[:k1n * sub], a[k1n * sub:]).reshape(k1n, sub, cb)


def fft_stage1(z, f1, length, cb=1024):
    _, _, n2, n1h, k1n = _fft_sizes(length)
    c = z.shape[-1]
    cb = min(cb, c)
    lead = z.shape[:-2]
    z4 = z.reshape(lead + (n1h, n2, c))
    nl = len(lead)
    sub = V7X_SUBLANES
    return pl.pallas_call(
        _fft1_kernel,
        grid=(n2 // sub, c // cb),
        in_specs=[pl.BlockSpec((None,) * nl + (n1h, sub, cb), lambda j, ci: (0,) * nl + (0, j, ci)),
                  pl.BlockSpec(f1.shape, lambda j, ci: (0, 0))],
        out_specs=pl.BlockSpec((k1n, sub, cb), lambda j, ci: (0, j, ci)),
        out_shape=jax.ShapeDtypeStruct((k1n, n2, c), jnp.uint32),
        compiler_params=_params("parallel", "parallel"),
        name="fft_stage1",
    )(z4, f1)


def _slab_dft(t_ref, xr, xi):
    n2 = xr.shape[0]
    x = jnp.concatenate([xr.astype(BF16), xi.astype(BF16)], axis=0)
    y = _dot(t_ref[...], x)
    return y[:n2], y[n2:]


def _filter_spec_kernel(af_ref, ab_ref, t2_ref, g_ref):
    fr, fi = _slab_dft(t2_ref, *_unpack_pair(af_ref[...]))
    br, bi = _slab_dft(t2_ref, *_unpack_pair(ab_ref[...]))
    g_ref[...] = _pack_pair(fr + br, fi - bi)


def filter_spectrum(a, t2, c, cb=512):
    k1n, n2, cf = a.shape
    orders = cf // (2 * c)
    cb = min(cb, c)
    per = c // cb
    fcol = lambda k, j: (k, 0, (j // per) * 2 * per + j % per)
    bcol = lambda k, j: (k, 0, (j // per) * 2 * per + per + j % per)
    blk = (None, n2, cb)
    return pl.pallas_call(
        _filter_spec_kernel,
        grid=(k1n, orders * per),
        in_specs=[pl.BlockSpec(blk, fcol), pl.BlockSpec(blk, bcol),
                  pl.BlockSpec((None, 2 * n2, 2 * n2), lambda k, j: (k, 0, 0))],
        out_specs=pl.BlockSpec(blk, lambda k, j: (k, 0, j)),
        out_shape=jax.ShapeDtypeStruct((k1n, n2, orders * c), jnp.uint32),
        compiler_params=_params("parallel", "parallel"),
        name="hyena_filter_spectrum",
    )(a, a, t2)


def _slab_conv_kernel(a_ref, g_ref, t2_ref, t2i_ref, p_ref):
    xr, xi = _slab_dft(t2_ref, *_unpack_pair(a_ref[...]))
    gr, gi = _unpack_pair(g_ref[...])
    pr, pi = _slab_dft(t2i_ref, xr * gr - xi * gi, xr * gi + xi * gr)
    p_ref[...] = _pack_pair(pr, pi)


def slab_conv(a, g, t2, t2i, order, cb=512):
    k1n, n2, c = a.shape
    cb = min(cb, c)
    per = c // cb
    blk = (None, n2, cb)
    dcol = lambda k, j: (k, 0, j)
    gcol = lambda k, j: (k, 0, order * per + j)
    tcol = lambda k, j: (k, 0, 0)
    return pl.pallas_call(
        _slab_conv_kernel,
        grid=(k1n, per),
        in_specs=[pl.BlockSpec(blk, dcol), pl.BlockSpec(blk, gcol),
                  pl.BlockSpec((None, 2 * n2, 2 * n2), tcol), pl.BlockSpec((None, 2 * n2, 2 * n2), tcol)],
        out_specs=pl.BlockSpec(blk, dcol),
        out_shape=jax.ShapeDtypeStruct((k1n, n2, c), jnp.uint32),
        compiler_params=_params("parallel", "parallel"),
        name="hyena_slab_conv",
    )(a, g, t2, t2i)


def _ifft3_kernel(p_ref, b3_ref, z_ref, gate_ref, skip_ref, o_ref):
    k1n, sub, cb = p_ref.shape
    n1h = o_ref.shape[0]
    pr, pi = _unpack_pair(p_ref[...])
    p = jnp.concatenate([pr.reshape(k1n * sub, cb).astype(BF16), pi.reshape(k1n * sub, cb).astype(BF16)], axis=0)
    y = _dot(b3_ref[...], p).reshape(n1h, sub, cb)
    z = z_ref[...]
    o_ref[...] = gate_ref[...] * (y + skip_ref[...] * z)


def ifft_stage3_gate(p, b3, u, z_arr, gate_idx, skip, length, cb=1024):
    _, _, n2, n1h, k1n = _fft_sizes(length)
    c = p.shape[-1]
    cb = min(cb, c)
    sub = V7X_SUBLANES
    u4 = u.reshape(u.shape[0], n1h, n2, c)
    z4 = z_arr.reshape((-1, n1h, n2, c))
    pblk = pl.BlockSpec((k1n, sub, cb), lambda j, ci: (0, j, ci))
    out = pl.pallas_call(
        _ifft3_kernel,
        grid=(n2 // sub, c // cb),
        in_specs=[pblk, pl.BlockSpec(b3.shape, lambda j, ci: (0, 0)),
                  pl.BlockSpec((None, n1h, sub, cb), lambda j, ci: (0, 0, j, ci)),
                  pl.BlockSpec((None, n1h, sub, cb), lambda j, ci: (gate_idx, 0, j, ci)),
                  pl.BlockSpec((1, cb), lambda j, ci: (0, ci))],
        out_specs=pl.BlockSpec((n1h, sub, cb), lambda j, ci: (0, j, ci)),
        out_shape=jax.ShapeDtypeStruct((n1h, n2, c), F32),
        compiler_params=_params("parallel", "parallel"),
        name="ifft_stage3_gate",
    )(p, b3, z4, u4, skip.reshape(1, c).astype(F32))
    return out.reshape(1, length, c)


def hyena_branch(proj, tables, conv_w, conv_b, f_w1, f_b1, f_w2, f_b2, f_w3, freq, decay, skip, norm, col_block):
    length = proj.shape[0]
    c = norm.shape[0]
    f1, t2, t2i, b3 = tables
    u = hyena_short_conv(proj, conv_w, conv_b, col_block, c)
    filt = hyena_filters(length, f_w1, f_b1, f_w2, f_b2, f_w3, freq, decay)
    g = filter_spectrum(fft_stage1(filt, f1, length), t2, c)
    z = u
    for o in range(skip.shape[0]):
        p = slab_conv(fft_stage1(z, f1, length), g, t2, t2i, o)
        z = ifft_stage3_gate(p, b3, u, z, 1 + o, skip[o], length)
    return group_norm(z.reshape(length, c), norm, HYENA_GROUPS, BF16)


def _pad_cols(w, n):
    return jnp.pad(w, ((0, 0), (0, n - w.shape[1])))


def kernel(x, norm_mix, w_in, lru_conv_w, lru_conv_b, lru_wr, lru_br, lru_wi, lru_bi, lru_lambda, lru_norm, hy_conv_w, hy_conv_b, hy_f_w1, hy_f_b1, hy_f_w2, hy_f_b2, hy_f_w3, hy_freq, hy_decay, hy_skip, hy_norm, ret_norm, w_out, norm_ffn, w_gate, w_up, w_down, norm_final):
    b, s, d = x.shape
    assert b == 1
    depth = w_in.shape[0]
    d_lru = lru_conv_w.shape[-1]
    d_hy = hy_norm.shape[-1]
    d_ff = w_gate.shape[-1]
    d_ff_pad = -(-d_ff // 1024) * 1024
    tables = _fft_tables(s)
    xs = x.reshape(s, d)
    for l in range(depth):
        h = rmsnorm(xs, norm_mix[l], BF16)
        proj = matmul_fullk([h], w_in[l].astype(BF16))
        y_a = lru_branch(proj, lru_conv_w[l], lru_conv_b[l], lru_wr[l], lru_br[l], lru_wi[l], lru_bi[l],
                         lru_lambda[l], lru_norm[l])
        y_b = hyena_branch(proj, tables, hy_conv_w[l], hy_conv_b[l], hy_f_w1[l], hy_f_b1[l], hy_f_w2[l],
                           hy_f_b2[l], hy_f_w3[l], hy_freq[l], hy_decay[l], hy_skip[l], hy_norm[l],
                           (2 * d_lru) // d_hy)
        y_c = retention_branch(proj, ret_norm[l], 2 * d_lru + 3 * d_hy)
        xs = matmul_fullk([y_a, y_b, y_c], w_out[l].astype(BF16), residual=xs, tn=512)
        h = rmsnorm(xs, norm_ffn[l], BF16)
        act = ffn_up(h, w_gate[l].astype(BF16), w_up[l].astype(BF16), d_ff_pad)
        w_dn = jnp.pad(w_down[l], ((0, d_ff_pad - d_ff), (0, 0))).astype(BF16)
        xs = matmul_kgrid(act, w_dn, xs, tk=d_ff_pad // 4)
    return rmsnorm(xs, norm_final, x.dtype).reshape(b, s, d)
```

```python
import functools
import math

import numpy as np
import jax
import jax.numpy as jnp
from jax import lax
from jax.experimental import pallas as pl
from jax.experimental.pallas import tpu as pltpu

EPS = 1e-6
LRU_HEADS = 8
LRU_C = 8.0
HYENA_GROUPS = 8
HYENA_BANDS = 16
RET_HEADS = 8
RET_KEY_DIM = 128
ROPE_BASE = 10000.0

V7X_SUBLANES = 8
V7X_LANES = 128
VMEM_LIMIT_BYTES = 56 * 1024 * 1024
FFT_N2 = 256
F32 = jnp.float32
BF16 = jnp.bfloat16


def _params(*sem):
    return pltpu.CompilerParams(dimension_semantics=sem, vmem_limit_bytes=VMEM_LIMIT_BYTES)


def _dot(a, b):
    return jnp.dot(a, b, preferred_element_type=F32)


def _rmsnorm_kernel(x_ref, g_ref, o_ref):
    x = x_ref[...]
    y = x * lax.rsqrt(jnp.mean(x * x, axis=-1, keepdims=True) + EPS)
    o_ref[...] = (y * g_ref[...]).astype(o_ref.dtype)


def rmsnorm(x, gain, out_dtype, tm=256):
    s, d = x.shape
    tm = min(tm, s)
    return pl.pallas_call(
        _rmsnorm_kernel,
        grid=(s // tm,),
        in_specs=[pl.BlockSpec((tm, d), lambda i: (i, 0)), pl.BlockSpec((1, d), lambda i: (0, 0))],
        out_specs=pl.BlockSpec((tm, d), lambda i: (i, 0)),
        out_shape=jax.ShapeDtypeStruct((s, d), out_dtype),
        compiler_params=_params("parallel"),
        name="rmsnorm",
    )(x, gain.reshape(1, d).astype(F32))


def _group_norm_kernel(x_ref, g_ref, o_ref, *, groups):
    x = x_ref[...]
    w = x.shape[-1] // groups
    for h in range(groups):
        xh = x[:, h * w:(h + 1) * w]
        yh = xh * lax.rsqrt(jnp.mean(xh * xh, axis=-1, keepdims=True) + EPS)
        o_ref[:, h * w:(h + 1) * w] = (yh * g_ref[:, h * w:(h + 1) * w]).astype(o_ref.dtype)


def group_norm(x, gain, groups, out_dtype, tm=512):
    s, d = x.shape
    tm = min(tm, s)
    return pl.pallas_call(
        functools.partial(_group_norm_kernel, groups=groups),
        grid=(s // tm,),
        in_specs=[pl.BlockSpec((tm, d), lambda i: (i, 0)), pl.BlockSpec((1, d), lambda i: (0, 0))],
        out_specs=pl.BlockSpec((tm, d), lambda i: (i, 0)),
        out_shape=jax.ShapeDtypeStruct((s, d), out_dtype),
        compiler_params=_params("parallel"),
        name="group_norm",
    )(x, gain.reshape(1, d).astype(F32))


def _layer_spec(layer, block, index_map):
    return pl.BlockSpec((None,) + block, lambda *g: (layer,) + index_map(*g))


def _mm_fullk_kernel(*refs, n_a, has_res):
    a_refs, b_ref, o_ref = refs[:n_a], refs[n_a], refs[-1]
    acc, off = None, 0
    for a_ref in a_refs:
        kw = a_ref.shape[1]
        part = _dot(a_ref[...], b_ref[off:off + kw, :].astype(BF16))
        acc = part if acc is None else acc + part
        off += kw
    if has_res:
        acc = refs[n_a + 1][...] + acc
    o_ref[...] = acc.astype(o_ref.dtype)


def matmul_fullk(a_parts, w, layer, residual=None, out_dtype=F32, tm=2048, tn=256):
    m = a_parts[0].shape[0]
    _, kd, n = w.shape
    assert sum(a.shape[1] for a in a_parts) == kd
    tm, tn = min(tm, m), min(tn, n)
    has_res = residual is not None
    in_specs = [pl.BlockSpec((tm, a.shape[1]), lambda i, j: (i, 0)) for a in a_parts]
    in_specs.append(_layer_spec(layer, (kd, tn), lambda i, j: (0, j)))
    args = list(a_parts) + [w]
    if has_res:
        in_specs.append(pl.BlockSpec((tm, tn), lambda i, j: (i, j)))
        args.append(residual)
    return pl.pallas_call(
        functools.partial(_mm_fullk_kernel, n_a=len(a_parts), has_res=has_res),
        grid=(m // tm, n // tn),
        in_specs=in_specs,
        out_specs=pl.BlockSpec((tm, tn), lambda i, j: (i, j)),
        out_shape=jax.ShapeDtypeStruct((m, n), out_dtype),
        compiler_params=_params("parallel", "parallel"),
        name="matmul_fullk_res" if has_res else "matmul_fullk",
    )(*args)


def _mm_kgrid_kernel(a_ref, b_ref, r_ref, o_ref, acc_ref):
    k = pl.program_id(2)

    @pl.when(k == 0)
    def _():
        acc_ref[...] = jnp.zeros_like(acc_ref)

    acc_ref[...] += _dot(a_ref[...], b_ref[...])

    @pl.when(k == pl.num_programs(2) - 1)
    def _():
        o_ref[...] = (r_ref[...] + acc_ref[...]).astype(o_ref.dtype)


def matmul_kgrid(a, b, residual, out_dtype=F32, tm=1024, tn=1024, tk=2816):
    m, kd = a.shape
    n = b.shape[1]
    tm, tn, tk = min(tm, m), min(tn, n), min(tk, kd)
    assert kd % tk == 0
    return pl.pallas_call(
        _mm_kgrid_kernel,
        grid=(m // tm, n // tn, kd // tk),
        in_specs=[pl.BlockSpec((tm, tk), lambda i, j, k: (i, k)), pl.BlockSpec((tk, tn), lambda i, j, k: (k, j)),
                  pl.BlockSpec((tm, tn), lambda i, j, k: (i, j))],
        out_specs=pl.BlockSpec((tm, tn), lambda i, j, k: (i, j)),
        out_shape=jax.ShapeDtypeStruct((m, n), out_dtype),
        scratch_shapes=[pltpu.VMEM((tm, tn), F32)],
        compiler_params=_params("parallel", "parallel", "arbitrary"),
        name="matmul_kgrid_res",
    )(a, b, residual)


def _ffn_up_kernel(h_ref, wg_ref, wu_ref, o_ref, *, n_valid):
    h = h_ref[...]
    g = _dot(h, wg_ref[...].astype(BF16))
    u = _dot(h, wu_ref[...].astype(BF16))
    tn = o_ref.shape[1]
    col = pl.program_id(1) * tn + lax.broadcasted_iota(jnp.int32, g.shape, 1)
    o_ref[...] = jnp.where(col < n_valid, jax.nn.silu(g) * u, 0.0).astype(o_ref.dtype)


def ffn_up(h, wg, wu, layer, n_out, tm=2048, tn=256):
    m, kd = h.shape
    n = wg.shape[2]
    tm, tn = min(tm, m), min(tn, n_out)
    assert n_out % tn == 0 and n_out >= n
    last = pl.cdiv(n, tn) - 1
    wspec = _layer_spec(layer, (kd, tn), lambda i, j: (0, jnp.minimum(j, last)))
    return pl.pallas_call(
        functools.partial(_ffn_up_kernel, n_valid=n),
        grid=(m // tm, n_out // tn),
        in_specs=[pl.BlockSpec((tm, kd), lambda i, j: (i, 0)), wspec, wspec],
        out_specs=pl.BlockSpec((tm, tn), lambda i, j: (i, j)),
        out_shape=jax.ShapeDtypeStruct((m, n_out), BF16),
        compiler_params=_params("parallel", "parallel"),
        name="ffn_up",
    )(h, wg, wu)


def _cast_pad_kernel(w_ref, o_ref, *, rows_valid):
    tb = o_ref.shape[0]
    row = pl.program_id(0) * tb + lax.broadcasted_iota(jnp.int32, w_ref.shape, 0)
    o_ref[...] = jnp.where(row < rows_valid, w_ref[...], 0.0).astype(o_ref.dtype)


def cast_pad_rows(w, layer, rows_out, tb=512):
    _, r, n = w.shape
    assert rows_out % tb == 0 and rows_out - r < tb
    return pl.pallas_call(
        functools.partial(_cast_pad_kernel, rows_valid=r),
        grid=(rows_out // tb,),
        in_specs=[_layer_spec(layer, (tb, n), lambda i: (i, 0))],
        out_specs=pl.BlockSpec((tb, n), lambda i: (i, 0)),
        out_shape=jax.ShapeDtypeStruct((rows_out, n), BF16),
        compiler_params=_params("parallel"),
        name="cast_pad_rows",
    )(w)


def _halo_specs(tb, width, col, nb, rev=False):
    r = tb // V7X_SUBLANES
    last = nb * r - 1

    def ti(i):
        return (nb - 1 - i) if rev else i

    return [
        pl.BlockSpec((V7X_SUBLANES, width), lambda i: (jnp.maximum(ti(i) * r - 1, 0), col)),
        pl.BlockSpec((tb, width), lambda i: (ti(i), col)),
        pl.BlockSpec((V7X_SUBLANES, width), lambda i: (jnp.minimum((ti(i) + 1) * r, last), col)),
    ]


def _dwconv_block(prev_ref, x_ref, next_ref, w_ref, b_ref, ext_ref, ti, nb, tb, width):
    left = width // 2
    prev = jnp.where(ti == 0, 0.0, prev_ref[...])
    nxt = jnp.where(ti == nb - 1, 0.0, next_ref[...])
    ext_ref[0:V7X_SUBLANES, :] = prev
    ext_ref[V7X_SUBLANES:V7X_SUBLANES + tb, :] = x_ref[...]
    ext_ref[V7X_SUBLANES + tb:2 * V7X_SUBLANES + tb, :] = nxt
    y = b_ref[...]
    for j in range(width):
        y = y + ext_ref[pl.ds(V7X_SUBLANES - left + j, tb), :] * w_ref[j:j + 1, :]
    return y


def _softplus(x):
    return jnp.maximum(x, 0.0) + jnp.log1p(jnp.exp(-jnp.abs(x)))


def _lru_kernel(*refs, reverse, final, nb, tb):
    if final:
        (xp_ref, x_ref, xn_ref, cw_ref, cb_ref, w_ref, gb_ref, lam_ref, hf_ref, gate_ref, norm_ref,
         o_ref, ext_ref, a_ref, b_ref, carry_ref) = refs
    else:
        (xp_ref, x_ref, xn_ref, cw_ref, cb_ref, w_ref, gb_ref, lam_ref,
         o_ref, ext_ref, a_ref, b_ref, carry_ref) = refs
    i = pl.program_id(0)
    ti = (nb - 1 - i) if reverse else i
    hd = V7X_LANES
    width = x_ref.shape[-1]

    @pl.when(i == 0)
    def _():
        carry_ref[...] = jnp.zeros_like(carry_ref)

    xc = _dwconv_block(xp_ref, x_ref, xn_ref, cw_ref, cb_ref, ext_ref, ti, nb, tb, 4)
    sp = _softplus(-lam_ref[...])
    for h in range(LRU_HEADS):
        sl = slice(h * hd, (h + 1) * hd)
        xh = xc[:, sl]
        z = _dot(xh.astype(BF16), w_ref[h]) + gb_ref[h]
        r = jax.nn.sigmoid(z[:, :hd])
        ig = jax.nn.sigmoid(z[:, hd:])
        log_a = -LRU_C * r * sp[:, sl]
        a = jnp.exp(log_a)
        a_ref[:, sl] = a
        b_ref[:, sl] = jnp.sqrt(-jnp.tanh(log_a) * (a * a + 1.0)) * (ig * xh)

    ng = tb // V7X_SUBLANES
    row = lax.broadcasted_iota(jnp.int32, (V7X_SUBLANES, width), 0)

    def body(g, carry):
        gi = (ng - 1 - g) if reverse else g
        off = pl.multiple_of(gi * V7X_SUBLANES, V7X_SUBLANES)
        a = a_ref[pl.ds(off, V7X_SUBLANES), :]
        b = b_ref[pl.ds(off, V7X_SUBLANES), :]
        for s in (1, 2, 4):
            if reverse:
                shift, m = V7X_SUBLANES - s, row < V7X_SUBLANES - s
            else:
                shift, m = s, row >= s
            b = jnp.where(m, a * pltpu.roll(b, shift, 0) + b, b)
            a = jnp.where(m, a * pltpu.roll(a, shift, 0), a)
        hcur = a * carry + b
        b_ref[pl.ds(off, V7X_SUBLANES), :] = hcur
        return hcur[0:1, :] if reverse else hcur[V7X_SUBLANES - 1:V7X_SUBLANES, :]

    carry_ref[...] = lax.fori_loop(0, ng, body, carry_ref[...])

    if not final:
        o_ref[...] = b_ref[...]
    else:
        y = jax.nn.gelu(gate_ref[...], approximate=True) * (hf_ref[...] + b_ref[...])
        for h in range(LRU_HEADS):
            sl = slice(h * hd, (h + 1) * hd)
            yh = y[:, sl]
            yh = yh * lax.rsqrt(jnp.mean(yh * yh, axis=-1, keepdims=True) + EPS)
            o_ref[:, sl] = (yh * norm_ref[:, sl]).astype(o_ref.dtype)


def lru_branch(proj, conv_w, conv_b, wr, br, wi, bi, lam, norm, tb=512):
    s = proj.shape[0]
    c = conv_w.shape[1]
    hd = c // LRU_HEADS
    tb = min(tb, s)
    nb = s // tb
    w = jnp.concatenate([wr, wi], axis=-1).astype(BF16)
    gb = jnp.concatenate([br.reshape(2, LRU_HEADS, 1, hd), bi.reshape(2, LRU_HEADS, 1, hd)], axis=-1).astype(F32)
    full = lambda shape: pl.BlockSpec(shape, lambda i: (0,) * len(shape))
    scratch = [pltpu.VMEM((tb + 2 * V7X_SUBLANES, c), F32), pltpu.VMEM((tb, c), F32), pltpu.VMEM((tb, c), F32),
               pltpu.VMEM((1, c), F32)]
    common = [full((4, c)), full((1, c)), full((LRU_HEADS, hd, 2 * hd)), full((LRU_HEADS, 1, 2 * hd)), full((1, c))]
    cw, cb = conv_w.astype(F32), conv_b.reshape(1, c).astype(F32)
    h_fwd = pl.pallas_call(
        functools.partial(_lru_kernel, reverse=False, final=False, nb=nb, tb=tb),
        grid=(nb,),
        in_specs=_halo_specs(tb, c, 1, nb) + common,
        out_specs=pl.BlockSpec((tb, c), lambda i: (i, 0)),
        out_shape=jax.ShapeDtypeStruct((s, c), F32),
        scratch_shapes=scratch,
        compiler_params=_params("arbitrary"),
        name="lru_fwd",
    )(proj, proj, proj, cw, cb, w[0], gb[0], lam[0].reshape(1, c).astype(F32))
    rev = lambda i: (nb - 1 - i, 0)
    return pl.pallas_call(
        functools.partial(_lru_kernel, reverse=True, final=True, nb=nb, tb=tb),
        grid=(nb,),
        in_specs=_halo_specs(tb, c, 1, nb, rev=True) + common
        + [pl.BlockSpec((tb, c), rev), pl.BlockSpec((tb, c), rev), full((1, c))],
        out_specs=pl.BlockSpec((tb, c), rev),
        out_shape=jax.ShapeDtypeStruct((s, c), BF16),
        scratch_shapes=scratch,
        compiler_params=_params("arbitrary"),
        name="lru_bwd",
    )(proj, proj, proj, cw, cb, w[1], gb[1], lam[1].reshape(1, c).astype(F32), h_fwd, proj,
      norm.reshape(1, c).astype(F32))


def _ret_log_gamma():
    return [float(np.log1p(-np.exp2(np.float32(-5.0 - h)), dtype=np.float32)) for h in range(RET_HEADS)]


def _ret_kernel(*refs, reverse, nc, c):
    if reverse:
        (q_ref, k_ref, v0_ref, v1_ref, inv_ref, y1_ref, g0_ref, g1_ref, norm_ref,
         o_ref, state_ref, dm_ref, rope_ref, dec_ref) = refs
    else:
        q_ref, k_ref, v0_ref, v1_ref, inv_ref, o_ref, state_ref, dm_ref, rope_ref, dec_ref = refs
    i = pl.program_id(0)
    ti = (nc - 1 - i) if reverse else i
    dk = RET_KEY_DIM
    dv = v0_ref.shape[-1] * 2 // RET_HEADS
    log_g = _ret_log_gamma()
    idx = lax.broadcasted_iota(jnp.int32, (c, 1), 0).astype(F32)
    inv = inv_ref[...]

    @pl.when(i == 0)
    def _():
        state_ref[...] = jnp.zeros_like(state_ref)
        rope_ref[0] = jnp.cos(idx * inv)
        rope_ref[1] = jnp.sin(idx * inv)
        for h in range(RET_HEADS):
            q_pow, k_pow = (c - idx, idx) if reverse else (idx + 1.0, c - 1.0 - idx)
            dec_ref[h, 0] = jnp.broadcast_to(jnp.exp(log_g[h] * q_pow), (c, dk))
            dec_ref[h, 1] = jnp.broadcast_to(jnp.exp(log_g[h] * k_pow), (c, dk))
        if not reverse:
            d = jnp.abs(lax.broadcasted_iota(jnp.int32, (c, c), 0)
                        - lax.broadcasted_iota(jnp.int32, (c, c), 1)).astype(F32)
            for h in range(RET_HEADS):
                dm_ref[h] = jnp.exp(log_g[h] * d)

    start = (ti * c).astype(F32) * inv
    cos_s, sin_s = jnp.cos(start), jnp.sin(start)
    cos_o, sin_o = rope_ref[0], rope_ref[1]
    lane = lax.broadcasted_iota(jnp.int32, (c, dk), 1)
    cos_t = cos_s * cos_o - sin_s * sin_o
    sin_t = jnp.where(lane < dk // 2, -1.0, 1.0) * (sin_s * cos_o + cos_s * sin_o)

    def rot(x):
        return x * cos_t + pltpu.roll(x, dk // 2, 1) * sin_t

    hpb = RET_HEADS // 2
    for h in range(RET_HEADS):
        qh = rot(q_ref[:, h * dk:(h + 1) * dk])
        kh = rot(k_ref[:, h * dk:(h + 1) * dk]) * (dk ** -0.5)
        v_ref = v0_ref if h < hpb else v1_ref
        vs = slice((h % hpb) * dv, (h % hpb + 1) * dv)
        vh = v_ref[:, vs].astype(BF16)
        lg = log_g[h]
        q_dec = qh * dec_ref[h, 0]
        k_dec = kh * dec_ref[h, 1]
        if reverse:
            y = y1_ref[:, h * dv:(h + 1) * dv]
        else:
            scores = lax.dot_general(qh.astype(BF16), kh.astype(BF16), (((1,), (1,)), ((), ())),
                                     preferred_element_type=F32) * dm_ref[h]
            y = _dot(scores.astype(BF16), vh)
        st = state_ref[h]
        y = y + _dot(q_dec.astype(BF16), st.astype(BF16))
        kv = lax.dot_general(k_dec.astype(BF16), vh, (((0,), (0,)), ((), ())), preferred_element_type=F32)
        state_ref[h] = math.exp(lg * c) * st + kv
        if reverse:
            g_ref = g0_ref if h < hpb else g1_ref
            yn = y * lax.rsqrt(jnp.mean(y * y, axis=-1, keepdims=True) + EPS) * norm_ref[:, h * dv:(h + 1) * dv]
            o_ref[:, h * dv:(h + 1) * dv] = (jax.nn.silu(g_ref[:, vs]) * yn).astype(o_ref.dtype)
        else:
            o_ref[:, h * dv:(h + 1) * dv] = y


def retention_branch(proj, norm, col0, c=256):
    s = proj.shape[0]
    d_ret = norm.shape[0]
    qk = RET_HEADS * RET_KEY_DIM
    c = min(c, s)
    nc = s // c
    half = RET_KEY_DIM // 2
    inv = ROPE_BASE ** (-jnp.arange(half, dtype=F32) / half)
    inv = jnp.concatenate([inv, inv]).reshape(1, RET_KEY_DIM)
    vw = d_ret // 2
    qb, kb = col0 // qk, (col0 + qk) // qk
    vb = (col0 + 2 * qk) // vw
    gb = (col0 + 2 * qk + d_ret) // vw
    assert col0 % qk == 0 and (col0 + 2 * qk) % vw == 0
    scratch = [pltpu.VMEM((RET_HEADS, RET_KEY_DIM, d_ret // RET_HEADS), F32), pltpu.VMEM((RET_HEADS, c, c), F32),
               pltpu.VMEM((2, c, RET_KEY_DIM), F32), pltpu.VMEM((RET_HEADS, 2, c, RET_KEY_DIM), F32)]

    def specs(ti):
        return [pl.BlockSpec((c, qk), lambda i: (ti(i), qb)), pl.BlockSpec((c, qk), lambda i: (ti(i), kb)),
                pl.BlockSpec((c, vw), lambda i: (ti(i), vb)), pl.BlockSpec((c, vw), lambda i: (ti(i), vb + 1)),
                pl.BlockSpec((1, RET_KEY_DIM), lambda i: (0, 0))]

    fwd = lambda i: i
    y1 = pl.pallas_call(
        functools.partial(_ret_kernel, reverse=False, nc=nc, c=c),
        grid=(nc,),
        in_specs=specs(fwd),
        out_specs=pl.BlockSpec((c, d_ret), lambda i: (i, 0)),
        out_shape=jax.ShapeDtypeStruct((s, d_ret), F32),
        scratch_shapes=scratch,
        compiler_params=_params("arbitrary"),
        name="ret_fwd",
    )(proj, proj, proj, proj, inv)
    rev = lambda i: nc - 1 - i
    return pl.pallas_call(
        functools.partial(_ret_kernel, reverse=True, nc=nc, c=c),
        grid=(nc,),
        in_specs=specs(rev) + [pl.BlockSpec((c, d_ret), lambda i: (rev(i), 0)),
                               pl.BlockSpec((c, vw), lambda i: (rev(i), gb)),
                               pl.BlockSpec((c, vw), lambda i: (rev(i), gb + 1)),
                               pl.BlockSpec((1, d_ret), lambda i: (0, 0))],
        out_specs=pl.BlockSpec((c, d_ret), lambda i: (rev(i), 0)),
        out_shape=jax.ShapeDtypeStruct((s, d_ret), BF16),
        scratch_shapes=scratch,
        compiler_params=_params("arbitrary"),
        name="ret_bwd",
    )(proj, proj, proj, proj, inv, y1, proj, proj, norm.reshape(1, d_ret).astype(F32))


def _hy_conv_kernel(xp_ref, x_ref, xn_ref, w_ref, b_ref, o_ref, ext_ref, *, nb, tb):
    ti = pl.program_id(1)
    o_ref[...] = _dwconv_block(xp_ref, x_ref, xn_ref, w_ref, b_ref, ext_ref, ti, nb, tb, 3)


def hyena_short_conv(proj, conv_w, conv_b, col_block, c, tb=512):
    s = proj.shape[0]
    tb = min(tb, s)
    nb = s // tb
    r = tb // V7X_SUBLANES
    last = nb * r - 1
    return pl.pallas_call(
        functools.partial(_hy_conv_kernel, nb=nb, tb=tb),
        grid=(3, nb),
        in_specs=[
            pl.BlockSpec((V7X_SUBLANES, c), lambda j, i: (jnp.maximum(i * r - 1, 0), col_block + j)),
            pl.BlockSpec((tb, c), lambda j, i: (i, col_block + j)),
            pl.BlockSpec((V7X_SUBLANES, c), lambda j, i: (jnp.minimum((i + 1) * r, last), col_block + j)),
            pl.BlockSpec((3, c), lambda j, i: (0, j)),
            pl.BlockSpec((1, c), lambda j, i: (0, j)),
        ],
        out_specs=pl.BlockSpec((None, tb, c), lambda j, i: (j, i, 0)),
        out_shape=jax.ShapeDtypeStruct((3, s, c), F32),
        scratch_shapes=[pltpu.VMEM((tb + 2 * V7X_SUBLANES, c), F32)],
        compiler_params=_params("parallel", "parallel"),
        name="hyena_short_conv",
    )(proj, proj, proj, conv_w.astype(F32), conv_b.reshape(1, -1).astype(F32))


def _hy_filter_kernel(fb_ref, w1_ref, b1_ref, w2_ref, b2_ref, w3_ref, freq_ref, decay_ref, o_ref, *, length, tb):
    i = pl.program_id(0)
    hi = lax.Precision.HIGHEST
    idx = (i * tb).astype(F32) + lax.broadcasted_iota(jnp.int32, (tb, 1), 0).astype(F32)
    t = idx / (length - 1.0)
    omega = (2.0 * math.pi / length) * idx
    lane = lax.broadcasted_iota(jnp.int32, (tb, V7X_LANES), 1)
    phase = fb_ref[...] * omega
    feats = jnp.where(lane == 0, t, jnp.where(lane <= HYENA_BANDS, jnp.cos(phase),
                                              jnp.where(lane <= 2 * HYENA_BANDS, -jnp.sin(phase), 0.0)))
    freq = freq_ref[...]
    hdn = jnp.sin(freq * (jnp.dot(feats, w1_ref[...], precision=hi, preferred_element_type=F32) + b1_ref[...]))
    hdn = jnp.sin(freq * (jnp.dot(hdn, w2_ref[...], precision=hi, preferred_element_type=F32) + b2_ref[...]))
    filt = _dot(hdn.astype(BF16), w3_ref[...])
    o_ref[...] = filt * jnp.exp(-t * jnp.abs(decay_ref[...]))


def hyena_filters(length, w1, b1, w2, b2, w3, freq, decay, tb=512):
    emb, hid = w1.shape
    bands = (emb - 1) // 2
    assert bands == HYENA_BANDS
    n_out = w3.shape[1]
    tb = min(tb, length)
    f = jnp.linspace(1e-4, bands - 1, bands, dtype=F32)
    fb = jnp.zeros((1, V7X_LANES), F32).at[0, 1:1 + bands].set(f).at[0, 1 + bands:1 + 2 * bands].set(f)
    w1p = jnp.zeros((V7X_LANES, hid), F32).at[:emb].set(w1.astype(F32))
    full = lambda shape: pl.BlockSpec(shape, lambda i: (0,) * len(shape))
    return pl.pallas_call(
        functools.partial(_hy_filter_kernel, length=length, tb=tb),
        grid=(length // tb,),
        in_specs=[full((1, V7X_LANES)), full((V7X_LANES, hid)), full((1, hid)), full((hid, hid)), full((1, hid)),
                  full((hid, n_out)), full((1, hid)), full((1, n_out))],
        out_specs=pl.BlockSpec((tb, n_out), lambda i: (i, 0)),
        out_shape=jax.ShapeDtypeStruct((length, n_out), F32),
        compiler_params=_params("parallel"),
        name="hyena_filters",
    )(fb, w1p, b1.reshape(1, hid).astype(F32), w2.astype(F32), b2.reshape(1, hid).astype(F32), w3.astype(BF16),
      freq.reshape(1, hid).astype(F32), decay.reshape(1, n_out).astype(F32))


def _fft_sizes(length):
    n = 2 * length
    n2 = min(FFT_N2, n // 4)
    n1 = n // n2
    return n, n1, n2, n1 // 2, n1 // 2 + 1


def _fft_tables(length):
    n, n1, n2, n1h, k1n = _fft_sizes(length)
    eye = np.eye(V7X_SUBLANES)
    k1 = np.arange(k1n)[:, None]
    a1 = np.arange(n1h)[None, :]
    ang1 = 2.0 * np.pi * ((k1 * a1) % n1) / n1
    f1 = np.concatenate([np.kron(np.cos(ang1), eye), np.kron(-np.sin(ang1), eye)], axis=0)
    wgt = np.full((k1n,), 2.0)
    wgt[0] = wgt[-1] = 1.0
    cw = (np.cos(ang1) * wgt[:, None] / n).T
    sw = (np.sin(ang1) * wgt[:, None] / n).T
    b3 = np.concatenate([np.kron(cw, eye), np.kron(-sw, eye)], axis=1)
    ik1 = jnp.arange(k1n, dtype=jnp.int32)[:, None, None]
    ik2 = jnp.arange(n2, dtype=jnp.int32)[None, :, None]
    in2 = jnp.arange(n2, dtype=jnp.int32)[None, None, :]
    m = (in2 * (ik1 + n1 * ik2)) & (n - 1)
    ang = m.astype(F32) * (2.0 * math.pi / n)
    cs, sn = jnp.cos(ang), jnp.sin(ang)
    t2 = jnp.concatenate([jnp.concatenate([cs, sn], -1), jnp.concatenate([-sn, cs], -1)], -2).astype(BF16)
    ct, st = jnp.swapaxes(cs, 1, 2), jnp.swapaxes(sn, 1, 2)
    t2i = jnp.concatenate([jnp.concatenate([ct, -st], -1), jnp.concatenate([st, ct], -1)], -2).astype(BF16)
    return jnp.asarray(f1, BF16), t2, t2i, jnp.asarray(b3, BF16)


def _pack_pair(re, im):
    hi = lax.bitcast_convert_type(re.astype(BF16).astype(F32), jnp.uint32)
    lo = lax.bitcast_convert_type(im.astype(BF16).astype(F32), jnp.uint32)
    return hi | (lo >> 16)


def _unpack_pair(w):
    re = lax.bitcast_convert_type(w & jnp.uint32(0xFFFF0000), F32)
    im = lax.bitcast_convert_type(w << 16, F32)
    return re, im


def _fft1_kernel(z_ref, f1_ref, a_ref):
    n1h, sub, cb = z_ref.shape
    k1n = a_ref.shape[0]
    z = z_ref[...].reshape(n1h * sub, cb).astype(BF16)
    a = _dot(f1_ref[...], z)
    a_ref[...] = _pack_pair(a[:k1n * sub], a[k1n * sub:]).reshape(k1n, sub, cb)


def fft_stage1(z, f1, length, cb=1024):
    _, _, n2, n1h, k1n = _fft_sizes(length)
    c = z.shape[-1]
    cb = min(cb, c)
    lead = z.shape[:-2]
    z4 = z.reshape(lead + (n1h, n2, c))
    nl = len(lead)
    sub = V7X_SUBLANES
    return pl.pallas_call(
        _fft1_kernel,
        grid=(n2 // sub, c // cb),
        in_specs=[pl.BlockSpec((None,) * nl + (n1h, sub, cb), lambda j, ci: (0,) * nl + (0, j, ci)),
                  pl.BlockSpec(f1.shape, lambda j, ci: (0, 0))],
        out_specs=pl.BlockSpec((k1n, sub, cb), lambda j, ci: (0, j, ci)),
        out_shape=jax.ShapeDtypeStruct((k1n, n2, c), jnp.uint32),
        compiler_params=_params("parallel", "parallel"),
        name="fft_stage1",
    )(z4, f1)


def _slab_dft(t_ref, xr, xi):
    n2 = xr.shape[0]
    x = jnp.concatenate([xr.astype(BF16), xi.astype(BF16)], axis=0)
    y = _dot(t_ref[...], x)
    return y[:n2], y[n2:]


def _filter_spec_kernel(af_ref, ab_ref, t2_ref, g_ref):
    fr, fi = _slab_dft(t2_ref, *_unpack_pair(af_ref[...]))
    br, bi = _slab_dft(t2_ref, *_unpack_pair(ab_ref[...]))
    g_ref[...] = _pack_pair(fr + br, fi - bi)


def filter_spectrum(a, t2, c, cb=1024):
    k1n, n2, cf = a.shape
    orders = cf // (2 * c)
    cb = min(cb, c)
    per = c // cb
    fcol = lambda k, j: (k, 0, (j // per) * 2 * per + j % per)
    bcol = lambda k, j: (k, 0, (j // per) * 2 * per + per + j % per)
    blk = (None, n2, cb)
    return pl.pallas_call(
        _filter_spec_kernel,
        grid=(k1n, orders * per),
        in_specs=[pl.BlockSpec(blk, fcol), pl.BlockSpec(blk, bcol),
                  pl.BlockSpec((None, 2 * n2, 2 * n2), lambda k, j: (k, 0, 0))],
        out_specs=pl.BlockSpec(blk, lambda k, j: (k, 0, j)),
        out_shape=jax.ShapeDtypeStruct((k1n, n2, orders * c), jnp.uint32),
        compiler_params=_params("parallel", "parallel"),
        name="hyena_filter_spectrum",
    )(a, a, t2)


def _slab_conv_kernel(a_ref, g_ref, t2_ref, t2i_ref, p_ref):
    xr, xi = _slab_dft(t2_ref, *_unpack_pair(a_ref[...]))
    gr, gi = _unpack_pair(g_ref[...])
    pr, pi = _slab_dft(t2i_ref, xr * gr - xi * gi, xr * gi + xi * gr)
    p_ref[...] = _pack_pair(pr, pi)


def slab_conv(a, g, t2, t2i, order, cb=1024):
    k1n, n2, c = a.shape
    cb = min(cb, c)
    per = c // cb
    blk = (None, n2, cb)
    dcol = lambda k, j: (k, 0, j)
    gcol = lambda k, j: (k, 0, order * per + j)
    tcol = lambda k, j: (k, 0, 0)
    return pl.pallas_call(
        _slab_conv_kernel,
        grid=(k1n, per),
        in_specs=[pl.BlockSpec(blk, dcol), pl.BlockSpec(blk, gcol),
                  pl.BlockSpec((None, 2 * n2, 2 * n2), tcol), pl.BlockSpec((None, 2 * n2, 2 * n2), tcol)],
        out_specs=pl.BlockSpec(blk, dcol),
        out_shape=jax.ShapeDtypeStruct((k1n, n2, c), jnp.uint32),
        compiler_params=_params("parallel", "parallel"),
        name="hyena_slab_conv",
    )(a, g, t2, t2i)


def _ifft3_kernel(p_ref, b3_ref, z_ref, gate_ref, skip_ref, o_ref):
    k1n, sub, cb = p_ref.shape
    n1h = o_ref.shape[0]
    pr, pi = _unpack_pair(p_ref[...])
    p = jnp.concatenate([pr.reshape(k1n * sub, cb).astype(BF16), pi.reshape(k1n * sub, cb).astype(BF16)], axis=0)
    y = _dot(b3_ref[...], p).reshape(n1h, sub, cb)
    z = z_ref[...]
    o_ref[...] = gate_ref[...] * (y + skip_ref[...] * z)


def ifft_stage3_gate(p, b3, u, z_arr, gate_idx, skip, length, cb=1024):
    _, _, n2, n1h, k1n = _fft_sizes(length)
    c = p.shape[-1]
    cb = min(cb, c)
    sub = V7X_SUBLANES
    u4 = u.reshape(u.shape[0], n1h, n2, c)
    z4 = z_arr.reshape((-1, n1h, n2, c))
    pblk = pl.BlockSpec((k1n, sub, cb), lambda j, ci: (0, j, ci))
    out = pl.pallas_call(
        _ifft3_kernel,
        grid=(n2 // sub, c // cb),
        in_specs=[pblk, pl.BlockSpec(b3.shape, lambda j, ci: (0, 0)),
                  pl.BlockSpec((None, n1h, sub, cb), lambda j, ci: (0, 0, j, ci)),
                  pl.BlockSpec((None, n1h, sub, cb), lambda j, ci: (gate_idx, 0, j, ci)),
                  pl.BlockSpec((1, cb), lambda j, ci: (0, ci))],
        out_specs=pl.BlockSpec((n1h, sub, cb), lambda j, ci: (0, j, ci)),
        out_shape=jax.ShapeDtypeStruct((n1h, n2, c), F32),
        compiler_params=_params("parallel", "parallel"),
        name="ifft_stage3_gate",
    )(p, b3, z4, u4, skip.reshape(1, c).astype(F32))
    return out.reshape(1, length, c)


def hyena_branch(proj, tables, conv_w, conv_b, f_w1, f_b1, f_w2, f_b2, f_w3, freq, decay, skip, norm, col_block):
    length = proj.shape[0]
    c = norm.shape[0]
    f1, t2, t2i, b3 = tables
    u = hyena_short_conv(proj, conv_w, conv_b, col_block, c)
    filt = hyena_filters(length, f_w1, f_b1, f_w2, f_b2, f_w3, freq, decay)
    g = filter_spectrum(fft_stage1(filt, f1, length), t2, c)
    z = u
    for o in range(skip.shape[0]):
        p = slab_conv(fft_stage1(z, f1, length), g, t2, t2i, o)
        z = ifft_stage3_gate(p, b3, u, z, 1 + o, skip[o], length)
    return group_norm(z.reshape(length, c), norm, HYENA_GROUPS, BF16)


def kernel(x, norm_mix, w_in, lru_conv_w, lru_conv_b, lru_wr, lru_br, lru_wi, lru_bi, lru_lambda, lru_norm, hy_conv_w, hy_conv_b, hy_f_w1, hy_f_b1, hy_f_w2, hy_f_b2, hy_f_w3, hy_freq, hy_decay, hy_skip, hy_norm, ret_norm, w_out, norm_ffn, w_gate, w_up, w_down, norm_final):
    b, s, d = x.shape
    assert b == 1
    depth = w_in.shape[0]
    d_lru = lru_conv_w.shape[-1]
    d_hy = hy_norm.shape[-1]
    d_ff = w_gate.shape[-1]
    d_ff_pad = -(-d_ff // 1024) * 1024
    tables = _fft_tables(s)
    xs = x.reshape(s, d)
    for l in range(depth):
        h = rmsnorm(xs, norm_mix[l], BF16)
        proj = matmul_fullk([h], w_in, l)
        y_a = lru_branch(proj, lru_conv_w[l], lru_conv_b[l], lru_wr[l], lru_br[l], lru_wi[l], lru_bi[l],
                         lru_lambda[l], lru_norm[l])
        y_b = hyena_branch(proj, tables, hy_conv_w[l], hy_conv_b[l], hy_f_w1[l], hy_f_b1[l], hy_f_w2[l],
                           hy_f_b2[l], hy_f_w3[l], hy_freq[l], hy_decay[l], hy_skip[l], hy_norm[l],
                           (2 * d_lru) // d_hy)
        y_c = retention_branch(proj, ret_norm[l], 2 * d_lru + 3 * d_hy)
        xs = matmul_fullk([y_a, y_b, y_c], w_out, l, residual=xs, tm=1024, tn=512)
        h = rmsnorm(xs, norm_ffn[l], BF16)
        act = ffn_up(h, w_gate, w_up, l, d_ff_pad)
        xs = matmul_kgrid(act, cast_pad_rows(w_down, l, d_ff_pad), xs, tk=d_ff_pad // 4)
    return rmsnorm(xs, norm_final, x.dtype).reshape(b, s, d)
```

```python
import functools
import math

import numpy as np
import jax
import jax.numpy as jnp
from jax import lax
from jax.experimental import pallas as pl
from jax.experimental.pallas import tpu as pltpu

EPS = 1e-6
LRU_HEADS = 8
LRU_C = 8.0
HYENA_GROUPS = 8
HYENA_BANDS = 16
RET_HEADS = 8
RET_KEY_DIM = 128
ROPE_BASE = 10000.0

V7X_SUBLANES = 8
V7X_LANES = 128
VMEM_LIMIT_BYTES = 56 * 1024 * 1024
FFT_N2 = 256
F32 = jnp.float32
BF16 = jnp.bfloat16


def _params(*sem):
    return pltpu.CompilerParams(dimension_semantics=sem, vmem_limit_bytes=VMEM_LIMIT_BYTES)


def _dot(a, b):
    return jnp.dot(a, b, preferred_element_type=F32)


def _rmsnorm_kernel(x_ref, g_ref, o_ref):
    x = x_ref[...]
    y = x * lax.rsqrt(jnp.mean(x * x, axis=-1, keepdims=True) + EPS)
    o_ref[...] = (y * g_ref[...]).astype(o_ref.dtype)


def rmsnorm(x, gain, out_dtype, tm=256):
    s, d = x.shape
    tm = min(tm, s)
    return pl.pallas_call(
        _rmsnorm_kernel,
        grid=(s // tm,),
        in_specs=[pl.BlockSpec((tm, d), lambda i: (i, 0)), pl.BlockSpec((1, d), lambda i: (0, 0))],
        out_specs=pl.BlockSpec((tm, d), lambda i: (i, 0)),
        out_shape=jax.ShapeDtypeStruct((s, d), out_dtype),
        compiler_params=_params("parallel"),
        name="rmsnorm",
    )(x, gain.reshape(1, d).astype(F32))


def _group_norm_kernel(x_ref, g_ref, o_ref, *, groups):
    x = x_ref[...]
    w = x.shape[-1] // groups
    for h in range(groups):
        xh = x[:, h * w:(h + 1) * w]
        yh = xh * lax.rsqrt(jnp.mean(xh * xh, axis=-1, keepdims=True) + EPS)
        o_ref[:, h * w:(h + 1) * w] = (yh * g_ref[:, h * w:(h + 1) * w]).astype(o_ref.dtype)


def group_norm(x, gain, groups, out_dtype, tm=512):
    s, d = x.shape
    tm = min(tm, s)
    return pl.pallas_call(
        functools.partial(_group_norm_kernel, groups=groups),
        grid=(s // tm,),
        in_specs=[pl.BlockSpec((tm, d), lambda i: (i, 0)), pl.BlockSpec((1, d), lambda i: (0, 0))],
        out_specs=pl.BlockSpec((tm, d), lambda i: (i, 0)),
        out_shape=jax.ShapeDtypeStruct((s, d), out_dtype),
        compiler_params=_params("parallel"),
        name="group_norm",
    )(x, gain.reshape(1, d).astype(F32))


def _layer_spec(layer, block, index_map):
    return pl.BlockSpec((None,) + block, lambda *g: (layer,) + index_map(*g))


def _mm_fullk_kernel(*refs, n_a, has_res):
    a_refs, b_ref, o_ref = refs[:n_a], refs[n_a], refs[-1]
    acc, off = None, 0
    for a_ref in a_refs:
        kw = a_ref.shape[1]
        part = _dot(a_ref[...], b_ref[off:off + kw, :].astype(BF16))
        acc = part if acc is None else acc + part
        off += kw
    if has_res:
        acc = refs[n_a + 1][...] + acc
    o_ref[...] = acc.astype(o_ref.dtype)


def matmul_fullk(a_parts, w, layer=None, residual=None, out_dtype=F32, tm=2048, tn=256):
    m = a_parts[0].shape[0]
    kd, n = w.shape[-2:]
    assert sum(a.shape[1] for a in a_parts) == kd
    tm, tn = min(tm, m), min(tn, n)
    has_res = residual is not None
    in_specs = [pl.BlockSpec((tm, a.shape[1]), lambda i, j: (i, 0)) for a in a_parts]
    if layer is None:
        in_specs.append(pl.BlockSpec((kd, tn), lambda i, j: (0, j)))
    else:
        in_specs.append(_layer_spec(layer, (kd, tn), lambda i, j: (0, j)))
    args = list(a_parts) + [w]
    if has_res:
        in_specs.append(pl.BlockSpec((tm, tn), lambda i, j: (i, j)))
        args.append(residual)
    return pl.pallas_call(
        functools.partial(_mm_fullk_kernel, n_a=len(a_parts), has_res=has_res),
        grid=(m // tm, n // tn),
        in_specs=in_specs,
        out_specs=pl.BlockSpec((tm, tn), lambda i, j: (i, j)),
        out_shape=jax.ShapeDtypeStruct((m, n), out_dtype),
        compiler_params=_params("parallel", "parallel"),
        name="matmul_fullk_res" if has_res else "matmul_fullk",
    )(*args)


def _ffn_up_kernel(h_ref, wg_ref, wu_ref, o_ref):
    h = h_ref[...]
    g = _dot(h, wg_ref[...].astype(BF16))
    u = _dot(h, wu_ref[...].astype(BF16))
    o_ref[...] = (jax.nn.silu(g) * u).astype(o_ref.dtype)


def ffn_up(h, wg, wu, layer, tm=2048, tn=256):
    m, kd = h.shape
    n = wg.shape[2]
    tm, tn = min(tm, m), min(tn, n)
    assert n % tn == 0
    wspec = _layer_spec(layer, (kd, tn), lambda i, j: (0, j))
    return pl.pallas_call(
        _ffn_up_kernel,
        grid=(m // tm, n // tn),
        in_specs=[pl.BlockSpec((tm, kd), lambda i, j: (i, 0)), wspec, wspec],
        out_specs=pl.BlockSpec((tm, tn), lambda i, j: (i, j)),
        out_shape=jax.ShapeDtypeStruct((m, n), BF16),
        compiler_params=_params("parallel", "parallel"),
        name="ffn_up",
    )(h, wg, wu)


def _cast_kernel(w_ref, o_ref):
    o_ref[...] = w_ref[...].astype(o_ref.dtype)


def cast_bf16(w, layer, tb=256):
    _, r, n = w.shape
    tb = min(tb, r)
    assert r % tb == 0
    return pl.pallas_call(
        _cast_kernel,
        grid=(r // tb,),
        in_specs=[_layer_spec(layer, (tb, n), lambda i: (i, 0))],
        out_specs=pl.BlockSpec((tb, n), lambda i: (i, 0)),
        out_shape=jax.ShapeDtypeStruct((r, n), BF16),
        compiler_params=_params("parallel"),
        name="cast_bf16",
    )(w)


def _halo_specs(tb, width, col, nb, rev=False):
    r = tb // V7X_SUBLANES
    last = nb * r - 1

    def ti(i):
        return (nb - 1 - i) if rev else i

    return [
        pl.BlockSpec((V7X_SUBLANES, width), lambda i: (jnp.maximum(ti(i) * r - 1, 0), col)),
        pl.BlockSpec((tb, width), lambda i: (ti(i), col)),
        pl.BlockSpec((V7X_SUBLANES, width), lambda i: (jnp.minimum((ti(i) + 1) * r, last), col)),
    ]


def _dwconv_block(prev_ref, x_ref, next_ref, w_ref, b_ref, ext_ref, ti, nb, tb, width):
    left = width // 2
    prev = jnp.where(ti == 0, 0.0, prev_ref[...])
    nxt = jnp.where(ti == nb - 1, 0.0, next_ref[...])
    ext_ref[0:V7X_SUBLANES, :] = prev
    ext_ref[V7X_SUBLANES:V7X_SUBLANES + tb, :] = x_ref[...]
    ext_ref[V7X_SUBLANES + tb:2 * V7X_SUBLANES + tb, :] = nxt
    y = b_ref[...]
    for j in range(width):
        y = y + ext_ref[pl.ds(V7X_SUBLANES - left + j, tb), :] * w_ref[j:j + 1, :]
    return y


def _softplus(x):
    return jnp.maximum(x, 0.0) + jnp.log1p(jnp.exp(-jnp.abs(x)))


def _lru_kernel(*refs, backward, nb, tb):
    reverse = final = backward
    if final:
        xc_ref, w_ref, gb_ref, lam_ref, hf_ref, gate_ref, norm_ref, o_ref, a_ref, b_ref, carry_ref = refs
    else:
        (xp_ref, x_ref, xn_ref, cw_ref, cb_ref, w_ref, gb_ref, lam_ref,
         o_ref, xc_ref, ext_ref, a_ref, b_ref, carry_ref) = refs
    i = pl.program_id(0)
    hd = V7X_LANES
    width = o_ref.shape[-1]

    @pl.when(i == 0)
    def _():
        carry_ref[...] = jnp.zeros_like(carry_ref)

    if final:
        xc = xc_ref[...]
    else:
        xc = _dwconv_block(xp_ref, x_ref, xn_ref, cw_ref, cb_ref, ext_ref, i, nb, tb, 4)
        xc_ref[...] = xc
    sp = _softplus(-lam_ref[...])
    for h in range(LRU_HEADS):
        sl = slice(h * hd, (h + 1) * hd)
        xh = xc[:, sl]
        z = _dot(xh.astype(BF16), w_ref[h]) + gb_ref[h]
        r = jax.nn.sigmoid(z[:, :hd])
        ig = jax.nn.sigmoid(z[:, hd:])
        log_a = -LRU_C * r * sp[:, sl]
        a = jnp.exp(log_a)
        a_ref[:, sl] = a
        b_ref[:, sl] = jnp.sqrt(-jnp.tanh(log_a) * (a * a + 1.0)) * (ig * xh)

    ng = tb // V7X_SUBLANES
    row = lax.broadcasted_iota(jnp.int32, (V7X_SUBLANES, width), 0)

    def body(g, carry):
        gi = (ng - 1 - g) if reverse else g
        off = pl.multiple_of(gi * V7X_SUBLANES, V7X_SUBLANES)
        a = a_ref[pl.ds(off, V7X_SUBLANES), :]
        b = b_ref[pl.ds(off, V7X_SUBLANES), :]
        for s in (1, 2, 4):
            if reverse:
                shift, m = V7X_SUBLANES - s, row < V7X_SUBLANES - s
            else:
                shift, m = s, row >= s
            b = jnp.where(m, a * pltpu.roll(b, shift, 0) + b, b)
            a = jnp.where(m, a * pltpu.roll(a, shift, 0), a)
        hcur = a * carry + b
        b_ref[pl.ds(off, V7X_SUBLANES), :] = hcur
        return hcur[0:1, :] if reverse else hcur[V7X_SUBLANES - 1:V7X_SUBLANES, :]

    carry_ref[...] = lax.fori_loop(0, ng, body, carry_ref[...])

    if not final:
        o_ref[...] = b_ref[...]
    else:
        y = jax.nn.gelu(gate_ref[...], approximate=True) * (hf_ref[...] + b_ref[...])
        for h in range(LRU_HEADS):
            sl = slice(h * hd, (h + 1) * hd)
            yh = y[:, sl]
            yh = yh * lax.rsqrt(jnp.mean(yh * yh, axis=-1, keepdims=True) + EPS)
            o_ref[:, sl] = (yh * norm_ref[:, sl]).astype(o_ref.dtype)


def lru_branch(proj, conv_w, conv_b, wr, br, wi, bi, lam, norm, tb=512):
    s = proj.shape[0]
    c = conv_w.shape[1]
    hd = c // LRU_HEADS
    tb = min(tb, s)
    nb = s // tb
    w = jnp.concatenate([wr, wi], axis=-1).astype(BF16)
    gb = jnp.concatenate([br.reshape(2, LRU_HEADS, 1, hd), bi.reshape(2, LRU_HEADS, 1, hd)], axis=-1).astype(F32)
    full = lambda shape: pl.BlockSpec(shape, lambda i: (0,) * len(shape))
    scratch = [pltpu.VMEM((tb, c), F32), pltpu.VMEM((tb, c), F32), pltpu.VMEM((1, c), F32)]
    gates = [full((LRU_HEADS, hd, 2 * hd)), full((LRU_HEADS, 1, 2 * hd)), full((1, c))]
    fwd = pl.BlockSpec((tb, c), lambda i: (i, 0))
    h_fwd, xc = pl.pallas_call(
        functools.partial(_lru_kernel, backward=False, nb=nb, tb=tb),
        grid=(nb,),
        in_specs=_halo_specs(tb, c, 1, nb) + [full((4, c)), full((1, c))] + gates,
        out_specs=[fwd, fwd],
        out_shape=[jax.ShapeDtypeStruct((s, c), F32)] * 2,
        scratch_shapes=[pltpu.VMEM((tb + 2 * V7X_SUBLANES, c), F32)] + scratch,
        compiler_params=_params("arbitrary"),
        name="lru_fwd",
    )(proj, proj, proj, conv_w.astype(F32), conv_b.reshape(1, c).astype(F32), w[0], gb[0],
      lam[0].reshape(1, c).astype(F32))
    rev = pl.BlockSpec((tb, c), lambda i: (nb - 1 - i, 0))
    return pl.pallas_call(
        functools.partial(_lru_kernel, backward=True, nb=nb, tb=tb),
        grid=(nb,),
        in_specs=[rev] + gates + [rev, rev, full((1, c))],
        out_specs=rev,
        out_shape=jax.ShapeDtypeStruct((s, c), BF16),
        scratch_shapes=scratch,
        compiler_params=_params("arbitrary"),
        name="lru_bwd",
    )(xc, w[1], gb[1], lam[1].reshape(1, c).astype(F32), h_fwd, proj, norm.reshape(1, c).astype(F32))


def _ret_log_gamma():
    return [float(np.log1p(-np.exp2(np.float32(-5.0 - h)), dtype=np.float32)) for h in range(RET_HEADS)]


def _ret_kernel(*refs, reverse, nc, c):
    if reverse:
        (q_ref, k_ref, v0_ref, v1_ref, inv_ref, y1_ref, g0_ref, g1_ref, norm_ref,
         o_ref, state_ref, dm_ref, rope_ref, dec_ref) = refs
    else:
        q_ref, k_ref, v0_ref, v1_ref, inv_ref, o_ref, state_ref, dm_ref, rope_ref, dec_ref = refs
    i = pl.program_id(0)
    ti = (nc - 1 - i) if reverse else i
    dk = RET_KEY_DIM
    dv = v0_ref.shape[-1] * 2 // RET_HEADS
    log_g = _ret_log_gamma()
    idx = lax.broadcasted_iota(jnp.int32, (c, 1), 0).astype(F32)
    inv = inv_ref[...]

    @pl.when(i == 0)
    def _():
        state_ref[...] = jnp.zeros_like(state_ref)
        rope_ref[0] = jnp.cos(idx * inv)
        rope_ref[1] = jnp.sin(idx * inv)
        for h in range(RET_HEADS):
            q_pow, k_pow = (c - idx, idx) if reverse else (idx + 1.0, c - 1.0 - idx)
            dec_ref[h, 0] = jnp.broadcast_to(jnp.exp(log_g[h] * q_pow), (c, dk))
            dec_ref[h, 1] = jnp.broadcast_to(jnp.exp(log_g[h] * k_pow), (c, dk))
        if not reverse:
            d = jnp.abs(lax.broadcasted_iota(jnp.int32, (c, c), 0)
                        - lax.broadcasted_iota(jnp.int32, (c, c), 1)).astype(F32)
            for h in range(RET_HEADS):
                dm_ref[h] = jnp.exp(log_g[h] * d)

    start = (ti * c).astype(F32) * inv
    cos_s, sin_s = jnp.cos(start), jnp.sin(start)
    cos_o, sin_o = rope_ref[0], rope_ref[1]
    lane = lax.broadcasted_iota(jnp.int32, (c, dk), 1)
    cos_t = cos_s * cos_o - sin_s * sin_o
    sin_t = jnp.where(lane < dk // 2, -1.0, 1.0) * (sin_s * cos_o + cos_s * sin_o)

    def rot(x):
        return x * cos_t + pltpu.roll(x, dk // 2, 1) * sin_t

    hpb = RET_HEADS // 2
    for h in range(RET_HEADS):
        qh = rot(q_ref[:, h * dk:(h + 1) * dk])
        kh = rot(k_ref[:, h * dk:(h + 1) * dk]) * (dk ** -0.5)
        v_ref = v0_ref if h < hpb else v1_ref
        vs = slice((h % hpb) * dv, (h % hpb + 1) * dv)
        vh = v_ref[:, vs].astype(BF16)
        lg = log_g[h]
        q_dec = qh * dec_ref[h, 0]
        k_dec = kh * dec_ref[h, 1]
        if reverse:
            y = y1_ref[:, h * dv:(h + 1) * dv]
        else:
            scores = lax.dot_general(qh.astype(BF16), kh.astype(BF16), (((1,), (1,)), ((), ())),
                                     preferred_element_type=F32) * dm_ref[h]
            y = _dot(scores.astype(BF16), vh)
        st = state_ref[h]
        y = y + _dot(q_dec.astype(BF16), st.astype(BF16))
        kv = lax.dot_general(k_dec.astype(BF16), vh, (((0,), (0,)), ((), ())), preferred_element_type=F32)
        state_ref[h] = math.exp(lg * c) * st + kv
        if reverse:
            g_ref = g0_ref if h < hpb else g1_ref
            yn = y * lax.rsqrt(jnp.mean(y * y, axis=-1, keepdims=True) + EPS) * norm_ref[:, h * dv:(h + 1) * dv]
            o_ref[:, h * dv:(h + 1) * dv] = (jax.nn.silu(g_ref[:, vs]) * yn).astype(o_ref.dtype)
        else:
            o_ref[:, h * dv:(h + 1) * dv] = y


def retention_branch(proj, norm, col0, c=256):
    s = proj.shape[0]
    d_ret = norm.shape[0]
    qk = RET_HEADS * RET_KEY_DIM
    c = min(c, s)
    nc = s // c
    half = RET_KEY_DIM // 2
    inv = ROPE_BASE ** (-jnp.arange(half, dtype=F32) / half)
    inv = jnp.concatenate([inv, inv]).reshape(1, RET_KEY_DIM)
    vw = d_ret // 2
    qb, kb = col0 // qk, (col0 + qk) // qk
    vb = (col0 + 2 * qk) // vw
    gb = (col0 + 2 * qk + d_ret) // vw
    assert col0 % qk == 0 and (col0 + 2 * qk) % vw == 0
    scratch = [pltpu.VMEM((RET_HEADS, RET_KEY_DIM, d_ret // RET_HEADS), F32), pltpu.VMEM((RET_HEADS, c, c), F32),
               pltpu.VMEM((2, c, RET_KEY_DIM), F32), pltpu.VMEM((RET_HEADS, 2, c, RET_KEY_DIM), F32)]

    def specs(ti):
        return [pl.BlockSpec((c, qk), lambda i: (ti(i), qb)), pl.BlockSpec((c, qk), lambda i: (ti(i), kb)),
                pl.BlockSpec((c, vw), lambda i: (ti(i), vb)), pl.BlockSpec((c, vw), lambda i: (ti(i), vb + 1)),
                pl.BlockSpec((1, RET_KEY_DIM), lambda i: (0, 0))]

    fwd = lambda i: i
    y1 = pl.pallas_call(
        functools.partial(_ret_kernel, reverse=False, nc=nc, c=c),
        grid=(nc,),
        in_specs=specs(fwd),
        out_specs=pl.BlockSpec((c, d_ret), lambda i: (i, 0)),
        out_shape=jax.ShapeDtypeStruct((s, d_ret), F32),
        scratch_shapes=scratch,
        compiler_params=_params("arbitrary"),
        name="ret_fwd",
    )(proj, proj, proj, proj, inv)
    rev = lambda i: nc - 1 - i
    return pl.pallas_call(
        functools.partial(_ret_kernel, reverse=True, nc=nc, c=c),
        grid=(nc,),
        in_specs=specs(rev) + [pl.BlockSpec((c, d_ret), lambda i: (rev(i), 0)),
                               pl.BlockSpec((c, vw), lambda i: (rev(i), gb)),
                               pl.BlockSpec((c, vw), lambda i: (rev(i), gb + 1)),
                               pl.BlockSpec((1, d_ret), lambda i: (0, 0))],
        out_specs=pl.BlockSpec((c, d_ret), lambda i: (rev(i), 0)),
        out_shape=jax.ShapeDtypeStruct((s, d_ret), BF16),
        scratch_shapes=scratch,
        compiler_params=_params("arbitrary"),
        name="ret_bwd",
    )(proj, proj, proj, proj, inv, y1, proj, proj, norm.reshape(1, d_ret).astype(F32))


def _hy_conv_kernel(xp_ref, x_ref, xn_ref, w_ref, b_ref, o_ref, ext_ref, *, nb, tb):
    ti = pl.program_id(1)
    o_ref[...] = _dwconv_block(xp_ref, x_ref, xn_ref, w_ref, b_ref, ext_ref, ti, nb, tb, 3)


def hyena_short_conv(proj, conv_w, conv_b, col_block, c, tb=512):
    s = proj.shape[0]
    tb = min(tb, s)
    nb = s // tb
    r = tb // V7X_SUBLANES
    last = nb * r - 1
    return pl.pallas_call(
        functools.partial(_hy_conv_kernel, nb=nb, tb=tb),
        grid=(3, nb),
        in_specs=[
            pl.BlockSpec((V7X_SUBLANES, c), lambda j, i: (jnp.maximum(i * r - 1, 0), col_block + j)),
            pl.BlockSpec((tb, c), lambda j, i: (i, col_block + j)),
            pl.BlockSpec((V7X_SUBLANES, c), lambda j, i: (jnp.minimum((i + 1) * r, last), col_block + j)),
            pl.BlockSpec((3, c), lambda j, i: (0, j)),
            pl.BlockSpec((1, c), lambda j, i: (0, j)),
        ],
        out_specs=pl.BlockSpec((None, tb, c), lambda j, i: (j, i, 0)),
        out_shape=jax.ShapeDtypeStruct((3, s, c), F32),
        scratch_shapes=[pltpu.VMEM((tb + 2 * V7X_SUBLANES, c), F32)],
        compiler_params=_params("parallel", "parallel"),
        name="hyena_short_conv",
    )(proj, proj, proj, conv_w.astype(F32), conv_b.reshape(1, -1).astype(F32))


def _hy_filter_kernel(fb_ref, w1_ref, b1_ref, w2_ref, b2_ref, w3_ref, freq_ref, decay_ref, o_ref, *, length, tb):
    i = pl.program_id(0)
    hi = lax.Precision.HIGHEST
    idx = (i * tb).astype(F32) + lax.broadcasted_iota(jnp.int32, (tb, 1), 0).astype(F32)
    t = idx / (length - 1.0)
    omega = (2.0 * math.pi / length) * idx
    lane = lax.broadcasted_iota(jnp.int32, (tb, V7X_LANES), 1)
    phase = fb_ref[...] * omega
    feats = jnp.where(lane == 0, t, jnp.where(lane <= HYENA_BANDS, jnp.cos(phase),
                                              jnp.where(lane <= 2 * HYENA_BANDS, -jnp.sin(phase), 0.0)))
    freq = freq_ref[...]
    hdn = jnp.sin(freq * (jnp.dot(feats, w1_ref[...], precision=hi, preferred_element_type=F32) + b1_ref[...]))
    hdn = jnp.sin(freq * (jnp.dot(hdn, w2_ref[...], precision=hi, preferred_element_type=F32) + b2_ref[...]))
    filt = _dot(hdn.astype(BF16), w3_ref[...])
    o_ref[...] = filt * jnp.exp(-t * jnp.abs(decay_ref[...]))


def hyena_filters(length, w1, b1, w2, b2, w3, freq, decay, tb=512):
    emb, hid = w1.shape
    bands = (emb - 1) // 2
    assert bands == HYENA_BANDS
    n_out = w3.shape[1]
    tb = min(tb, length)
    f = jnp.linspace(1e-4, bands - 1, bands, dtype=F32)
    fb = jnp.zeros((1, V7X_LANES), F32).at[0, 1:1 + bands].set(f).at[0, 1 + bands:1 + 2 * bands].set(f)
    w1p = jnp.zeros((V7X_LANES, hid), F32).at[:emb].set(w1.astype(F32))
    full = lambda shape: pl.BlockSpec(shape, lambda i: (0,) * len(shape))
    return pl.pallas_call(
        functools.partial(_hy_filter_kernel, length=length, tb=tb),
        grid=(length // tb,),
        in_specs=[full((1, V7X_LANES)), full((V7X_LANES, hid)), full((1, hid)), full((hid, hid)), full((1, hid)),
                  full((hid, n_out)), full((1, hid)), full((1, n_out))],
        out_specs=pl.BlockSpec((tb, n_out), lambda i: (i, 0)),
        out_shape=jax.ShapeDtypeStruct((length, n_out), F32),
        compiler_params=_params("parallel"),
        name="hyena_filters",
    )(fb, w1p, b1.reshape(1, hid).astype(F32), w2.astype(F32), b2.reshape(1, hid).astype(F32), w3.astype(BF16),
      freq.reshape(1, hid).astype(F32), decay.reshape(1, n_out).astype(F32))


def _fft_sizes(length):
    n = 2 * length
    n2 = min(FFT_N2, n // 4)
    n1 = n // n2
    return n, n1, n2, n1 // 2, n1 // 2 + 1


def _fft_tables(length):
    n, n1, n2, n1h, k1n = _fft_sizes(length)
    eye = np.eye(V7X_SUBLANES)
    k1 = np.arange(k1n)[:, None]
    a1 = np.arange(n1h)[None, :]
    ang1 = 2.0 * np.pi * ((k1 * a1) % n1) / n1
    f1 = np.concatenate([np.kron(np.cos(ang1), eye), np.kron(-np.sin(ang1), eye)], axis=0)
    wgt = np.full((k1n,), 2.0)
    wgt[0] = wgt[-1] = 1.0
    cw = (np.cos(ang1) * wgt[:, None] / n).T
    sw = (np.sin(ang1) * wgt[:, None] / n).T
    b3 = np.concatenate([np.kron(cw, eye), np.kron(-sw, eye)], axis=1)
    a2 = np.arange(n2)
    angt = 2.0 * np.pi * (k1 * a2[None, :]) / n
    tw = jnp.asarray(np.stack([np.cos(angt), np.sin(angt)]), F32)
    tw = jnp.broadcast_to(tw[..., None], tw.shape + (V7X_LANES,))
    ang2 = 2.0 * np.pi * ((a2[:, None] * a2[None, :]) % n2) / n2
    c2, s2 = np.cos(ang2), np.sin(ang2)
    f2 = np.block([[c2, s2], [-s2, c2]])
    f2i = np.block([[c2, -s2], [s2, c2]])
    return jnp.asarray(f1, BF16), tw, jnp.asarray(f2, BF16), jnp.asarray(f2i, BF16), jnp.asarray(b3, BF16)


def _pack_pair(re, im):
    hi = lax.bitcast_convert_type(re.astype(BF16).astype(F32), jnp.uint32)
    lo = lax.bitcast_convert_type(im.astype(BF16).astype(F32), jnp.uint32)
    return hi | (lo >> 16)


def _unpack_pair(w):
    re = lax.bitcast_convert_type(w & jnp.uint32(0xFFFF0000), F32)
    im = lax.bitcast_convert_type(w << 16, F32)
    return re, im


def _twiddle(tw_ref, xr, xi, conj):
    c, s = tw_ref[0], tw_ref[1]
    if conj:
        s = -s
    out_r, out_i = [], []
    for g in range(xr.shape[-1] // V7X_LANES):
        sl = slice(g * V7X_LANES, (g + 1) * V7X_LANES)
        out_r.append(xr[..., sl] * c + xi[..., sl] * s)
        out_i.append(xi[..., sl] * c - xr[..., sl] * s)
    return jnp.concatenate(out_r, axis=-1), jnp.concatenate(out_i, axis=-1)


def _fft1_kernel(z_ref, f1_ref, tw_ref, a_ref):
    n1h, sub, cb = z_ref.shape
    k1n = a_ref.shape[0]
    z = z_ref[...].reshape(n1h * sub, cb).astype(BF16)
    a = _dot(f1_ref[...], z)
    ar = a[:k1n * sub].reshape(k1n, sub, cb)
    ai = a[k1n * sub:].reshape(k1n, sub, cb)
    a_ref[...] = _pack_pair(*_twiddle(tw_ref, ar, ai, False))


def fft_stage1(z, f1, tw, length, cb=1024):
    _, _, n2, n1h, k1n = _fft_sizes(length)
    c = z.shape[-1]
    cb = min(cb, c)
    lead = z.shape[:-2]
    z4 = z.reshape(lead + (n1h, n2, c))
    nl = len(lead)
    sub = V7X_SUBLANES
    return pl.pallas_call(
        _fft1_kernel,
        grid=(n2 // sub, c // cb),
        in_specs=[pl.BlockSpec((None,) * nl + (n1h, sub, cb), lambda j, ci: (0,) * nl + (0, j, ci)),
                  pl.BlockSpec(f1.shape, lambda j, ci: (0, 0)),
                  pl.BlockSpec((2, k1n, sub, V7X_LANES), lambda j, ci: (0, 0, j, 0))],
        out_specs=pl.BlockSpec((k1n, sub, cb), lambda j, ci: (0, j, ci)),
        out_shape=jax.ShapeDtypeStruct((k1n, n2, c), jnp.uint32),
        compiler_params=_params("parallel", "parallel"),
        name="fft_stage1",
    )(z4, f1, tw)


def _slab_dft(t_ref, xr, xi):
    n2 = xr.shape[0]
    x = jnp.concatenate([xr.astype(BF16), xi.astype(BF16)], axis=0)
    y = _dot(t_ref[...], x)
    return y[:n2], y[n2:]


def _slab_specs(n2):
    tw = pl.BlockSpec((2, None, n2, V7X_LANES), lambda k, j: (0, k, 0, 0))
    f2 = pl.BlockSpec((2 * n2, 2 * n2), lambda k, j: (0, 0))
    return tw, f2


def _filter_spec_kernel(af_ref, ab_ref, f2_ref, g_ref):
    fr, fi = _slab_dft(f2_ref, *_unpack_pair(af_ref[...]))
    br, bi = _slab_dft(f2_ref, *_unpack_pair(ab_ref[...]))
    g_ref[...] = _pack_pair(fr + br, fi - bi)


def filter_spectrum(a, f2, c, cb=1024):
    k1n, n2, cf = a.shape
    orders = cf // (2 * c)
    cb = min(cb, c)
    per = c // cb
    fcol = lambda k, j: (k, 0, (j // per) * 2 * per + j % per)
    bcol = lambda k, j: (k, 0, (j // per) * 2 * per + per + j % per)
    blk = (None, n2, cb)
    _, f2_spec = _slab_specs(n2)
    return pl.pallas_call(
        _filter_spec_kernel,
        grid=(k1n, orders * per),
        in_specs=[pl.BlockSpec(blk, fcol), pl.BlockSpec(blk, bcol), f2_spec],
        out_specs=pl.BlockSpec(blk, lambda k, j: (k, 0, j)),
        out_shape=jax.ShapeDtypeStruct((k1n, n2, orders * c), jnp.uint32),
        compiler_params=_params("parallel", "parallel"),
        name="hyena_filter_spectrum",
    )(a, a, f2)


def _slab_conv_kernel(a_ref, g_ref, tw_ref, f2_ref, f2i_ref, p_ref):
    xr, xi = _slab_dft(f2_ref, *_unpack_pair(a_ref[...]))
    gr, gi = _unpack_pair(g_ref[...])
    qr, qi = _slab_dft(f2i_ref, xr * gr - xi * gi, xr * gi + xi * gr)
    p_ref[...] = _pack_pair(*_twiddle(tw_ref, qr, qi, True))


def slab_conv(a, g, tw, f2, f2i, order, cb=1024):
    k1n, n2, c = a.shape
    cb = min(cb, c)
    per = c // cb
    blk = (None, n2, cb)
    dcol = lambda k, j: (k, 0, j)
    gcol = lambda k, j: (k, 0, order * per + j)
    tw_spec, f2_spec = _slab_specs(n2)
    return pl.pallas_call(
        _slab_conv_kernel,
        grid=(k1n, per),
        in_specs=[pl.BlockSpec(blk, dcol), pl.BlockSpec(blk, gcol), tw_spec, f2_spec, f2_spec],
        out_specs=pl.BlockSpec(blk, dcol),
        out_shape=jax.ShapeDtypeStruct((k1n, n2, c), jnp.uint32),
        compiler_params=_params("parallel", "parallel"),
        name="hyena_slab_conv",
    )(a, g, tw, f2, f2i)


def _ifft3_kernel(p_ref, b3_ref, z_ref, gate_ref, skip_ref, o_ref):
    k1n, sub, cb = p_ref.shape
    n1h = o_ref.shape[0]
    pr, pi = _unpack_pair(p_ref[...])
    p = jnp.concatenate([pr.reshape(k1n * sub, cb).astype(BF16), pi.reshape(k1n * sub, cb).astype(BF16)], axis=0)
    y = _dot(b3_ref[...], p).reshape(n1h, sub, cb)
    z = z_ref[...]
    o_ref[...] = gate_ref[...] * (y + skip_ref[...] * z)


def ifft_stage3_gate(p, b3, u, z_arr, gate_idx, skip, length, cb=1024):
    _, _, n2, n1h, k1n = _fft_sizes(length)
    c = p.shape[-1]
    cb = min(cb, c)
    sub = V7X_SUBLANES
    u4 = u.reshape(u.shape[0], n1h, n2, c)
    z4 = z_arr.reshape((-1, n1h, n2, c))
    pblk = pl.BlockSpec((k1n, sub, cb), lambda j, ci: (0, j, ci))
    out = pl.pallas_call(
        _ifft3_kernel,
        grid=(n2 // sub, c // cb),
        in_specs=[pblk, pl.BlockSpec(b3.shape, lambda j, ci: (0, 0)),
                  pl.BlockSpec((None, n1h, sub, cb), lambda j, ci: (0, 0, j, ci)),
                  pl.BlockSpec((None, n1h, sub, cb), lambda j, ci: (gate_idx, 0, j, ci)),
                  pl.BlockSpec((1, cb), lambda j, ci: (0, ci))],
        out_specs=pl.BlockSpec((n1h, sub, cb), lambda j, ci: (0, j, ci)),
        out_shape=jax.ShapeDtypeStruct((n1h, n2, c), F32),
        compiler_params=_params("parallel", "parallel"),
        name="ifft_stage3_gate",
    )(p, b3, z4, u4, skip.reshape(1, c).astype(F32))
    return out.reshape(1, length, c)


def hyena_branch(proj, tables, conv_w, conv_b, f_w1, f_b1, f_w2, f_b2, f_w3, freq, decay, skip, norm, col_block):
    length = proj.shape[0]
    c = norm.shape[0]
    f1, tw, f2, f2i, b3 = tables
    u = hyena_short_conv(proj, conv_w, conv_b, col_block, c)
    filt = hyena_filters(length, f_w1, f_b1, f_w2, f_b2, f_w3, freq, decay)
    g = filter_spectrum(fft_stage1(filt, f1, tw, length), f2, c)
    z = u
    for o in range(skip.shape[0]):
        p = slab_conv(fft_stage1(z, f1, tw, length), g, tw, f2, f2i, o)
        z = ifft_stage3_gate(p, b3, u, z, 1 + o, skip[o], length)
    return group_norm(z.reshape(length, c), norm, HYENA_GROUPS, BF16)


def kernel(x, norm_mix, w_in, lru_conv_w, lru_conv_b, lru_wr, lru_br, lru_wi, lru_bi, lru_lambda, lru_norm, hy_conv_w, hy_conv_b, hy_f_w1, hy_f_b1, hy_f_w2, hy_f_b2, hy_f_w3, hy_freq, hy_decay, hy_skip, hy_norm, ret_norm, w_out, norm_ffn, w_gate, w_up, w_down, norm_final):
    b, s, d = x.shape
    assert b == 1
    depth = w_in.shape[0]
    d_lru = lru_conv_w.shape[-1]
    d_hy = hy_norm.shape[-1]
    tables = _fft_tables(s)
    xs = x.reshape(s, d)
    for l in range(depth):
        h = rmsnorm(xs, norm_mix[l], BF16)
        proj = matmul_fullk([h], w_in, l)
        y_a = lru_branch(proj, lru_conv_w[l], lru_conv_b[l], lru_wr[l], lru_br[l], lru_wi[l], lru_bi[l],
                         lru_lambda[l], lru_norm[l])
        y_b = hyena_branch(proj, tables, hy_conv_w[l], hy_conv_b[l], hy_f_w1[l], hy_f_b1[l], hy_f_w2[l],
                           hy_f_b2[l], hy_f_w3[l], hy_freq[l], hy_decay[l], hy_skip[l], hy_norm[l],
                           (2 * d_lru) // d_hy)
        y_c = retention_branch(proj, ret_norm[l], 2 * d_lru + 3 * d_hy)
        xs = matmul_fullk([y_a, y_b, y_c], w_out, l, residual=xs, tm=1024, tn=512)
        h = rmsnorm(xs, norm_ffn[l], BF16)
        act = ffn_up(h, w_gate, w_up, l)
        xs = matmul_fullk([act], cast_bf16(w_down, l), residual=xs, tm=512, tn=512)
    return rmsnorm(xs, norm_final, x.dtype).reshape(b, s, d)
```

```python
import functools
import math

import numpy as np
import jax
import jax.numpy as jnp
from jax import lax
from jax.experimental import pallas as pl
from jax.experimental.pallas import tpu as pltpu

EPS = 1e-6
LRU_HEADS = 8
LRU_C = 8.0
HYENA_GROUPS = 8
HYENA_BANDS = 16
RET_HEADS = 8
RET_KEY_DIM = 128
ROPE_BASE = 10000.0

V7X_SUBLANES = 8
V7X_LANES = 128
VMEM_LIMIT_BYTES = 58 * 1024 * 1024
FFT_N2 = 256
F32 = jnp.float32
BF16 = jnp.bfloat16


def _params(*sem):
    return pltpu.CompilerParams(dimension_semantics=sem, vmem_limit_bytes=VMEM_LIMIT_BYTES)


def _dot(a, b):
    return jnp.dot(a, b, preferred_element_type=F32)


def _rmsnorm_kernel(x_ref, g_ref, o_ref):
    x = x_ref[...]
    y = x * lax.rsqrt(jnp.mean(x * x, axis=-1, keepdims=True) + EPS)
    o_ref[...] = (y * g_ref[...]).astype(o_ref.dtype)


def rmsnorm(x, gain, out_dtype, tm=256):
    s, d = x.shape
    tm = min(tm, s)
    return pl.pallas_call(
        _rmsnorm_kernel,
        grid=(s // tm,),
        in_specs=[pl.BlockSpec((tm, d), lambda i: (i, 0)), pl.BlockSpec((1, d), lambda i: (0, 0))],
        out_specs=pl.BlockSpec((tm, d), lambda i: (i, 0)),
        out_shape=jax.ShapeDtypeStruct((s, d), out_dtype),
        compiler_params=_params("parallel"),
        name="rmsnorm",
    )(x, gain.reshape(1, d).astype(F32))


def _group_norm_kernel(x_ref, g_ref, o_ref, *, groups):
    x = x_ref[...]
    w = x.shape[-1] // groups
    for h in range(groups):
        xh = x[:, h * w:(h + 1) * w]
        yh = xh * lax.rsqrt(jnp.mean(xh * xh, axis=-1, keepdims=True) + EPS)
        o_ref[:, h * w:(h + 1) * w] = (yh * g_ref[:, h * w:(h + 1) * w]).astype(o_ref.dtype)


def group_norm(x, gain, groups, out_dtype, tm=512):
    s, d = x.shape
    tm = min(tm, s)
    return pl.pallas_call(
        functools.partial(_group_norm_kernel, groups=groups),
        grid=(s // tm,),
        in_specs=[pl.BlockSpec((tm, d), lambda i: (i, 0)), pl.BlockSpec((1, d), lambda i: (0, 0))],
        out_specs=pl.BlockSpec((tm, d), lambda i: (i, 0)),
        out_shape=jax.ShapeDtypeStruct((s, d), out_dtype),
        compiler_params=_params("parallel"),
        name="group_norm",
    )(x, gain.reshape(1, d).astype(F32))


def _layer_spec(layer, block, index_map):
    return pl.BlockSpec((None,) + block, lambda *g: (layer,) + index_map(*g))


def _scale_rows(x, s):
    parts = [x[:, g * V7X_LANES:(g + 1) * V7X_LANES] * s for g in range(x.shape[1] // V7X_LANES)]
    return jnp.concatenate(parts, axis=1)


def _mm_fullk_kernel(*refs, n_a, has_scale, has_res, emit_norm, n_total):
    refs = list(refs)
    a_refs = [refs.pop(0) for _ in range(n_a)]
    b_ref = refs.pop(0)
    s_ref = refs.pop(0) if has_scale else None
    r_ref = refs.pop(0) if has_res else None
    gain_ref = refs.pop(0) if emit_norm else None
    o_ref = refs.pop(0)
    acc, off = None, 0
    for a_ref in a_refs:
        kw = a_ref.shape[1]
        part = _dot(a_ref[...], b_ref[off:off + kw, :].astype(BF16))
        acc = part if acc is None else acc + part
        off += kw
    if has_scale:
        acc = _scale_rows(acc, s_ref[...])
    if has_res:
        acc = r_ref[...] + acc
    o_ref[...] = acc.astype(o_ref.dtype)
    if emit_norm:
        xg_ref, rstd_ref, ss_ref = refs
        j = pl.program_id(1)

        @pl.when(j == 0)
        def _():
            ss_ref[...] = jnp.zeros_like(ss_ref)

        ss_ref[...] += jnp.sum(acc * acc, axis=-1, keepdims=True)
        xg_ref[...] = (acc * gain_ref[...]).astype(xg_ref.dtype)

        @pl.when(j == pl.num_programs(1) - 1)
        def _():
            rstd_ref[...] = lax.rsqrt(ss_ref[...] * (1.0 / n_total) + EPS)


def matmul_fullk(a_parts, w, layer=None, row_scale=None, residual=None, norm_gain=None, out_dtype=F32,
                 tm=2048, tn=256):
    m = a_parts[0].shape[0]
    kd, n = w.shape[-2:]
    assert sum(a.shape[1] for a in a_parts) == kd
    tm, tn = min(tm, m), min(tn, n)
    has_scale, has_res, emit_norm = row_scale is not None, residual is not None, norm_gain is not None
    row_blk = pl.BlockSpec((tm, V7X_LANES), lambda i, j: (i, 0))
    out_blk = pl.BlockSpec((tm, tn), lambda i, j: (i, j))
    in_specs = [pl.BlockSpec((tm, a.shape[1]), lambda i, j: (i, 0)) for a in a_parts]
    if layer is None:
        in_specs.append(pl.BlockSpec((kd, tn), lambda i, j: (0, j)))
    else:
        in_specs.append(_layer_spec(layer, (kd, tn), lambda i, j: (0, j)))
    args = list(a_parts) + [w]
    if has_scale:
        in_specs.append(row_blk)
        args.append(row_scale)
    if has_res:
        in_specs.append(out_blk)
        args.append(residual)
    out_specs, out_shape, scratch = out_blk, jax.ShapeDtypeStruct((m, n), out_dtype), []
    if emit_norm:
        in_specs.append(pl.BlockSpec((1, tn), lambda i, j: (0, j)))
        args.append(norm_gain.reshape(1, n).astype(F32))
        out_specs = [out_blk, out_blk, row_blk]
        out_shape = [out_shape, jax.ShapeDtypeStruct((m, n), BF16), jax.ShapeDtypeStruct((m, V7X_LANES), F32)]
        scratch = [pltpu.VMEM((tm, V7X_LANES), F32)]
    return pl.pallas_call(
        functools.partial(_mm_fullk_kernel, n_a=len(a_parts), has_scale=has_scale, has_res=has_res,
                          emit_norm=emit_norm, n_total=n),
        grid=(m // tm, n // tn),
        in_specs=in_specs,
        out_specs=out_specs,
        out_shape=out_shape,
        scratch_shapes=scratch,
        compiler_params=_params("parallel", "arbitrary" if emit_norm else "parallel"),
        name="matmul_fullk_res" if has_res else "matmul_fullk",
    )(*args)


def _ffn_up_kernel(h_ref, wg_ref, wu_ref, s_ref, o_ref):
    h = h_ref[...]
    s = s_ref[...]
    g = _scale_rows(_dot(h, wg_ref[...].astype(BF16)), s)
    u = _scale_rows(_dot(h, wu_ref[...].astype(BF16)), s)
    o_ref[...] = (jax.nn.silu(g) * u).astype(o_ref.dtype)


def ffn_up(h, wg, wu, layer, row_scale, tm=2048, tn=256):
    m, kd = h.shape
    n = wg.shape[2]
    tm, tn = min(tm, m), min(tn, n)
    assert n % tn == 0
    wspec = _layer_spec(layer, (kd, tn), lambda i, j: (0, j))
    return pl.pallas_call(
        _ffn_up_kernel,
        grid=(m // tm, n // tn),
        in_specs=[pl.BlockSpec((tm, kd), lambda i, j: (i, 0)), wspec, wspec,
                  pl.BlockSpec((tm, V7X_LANES), lambda i, j: (i, 0))],
        out_specs=pl.BlockSpec((tm, tn), lambda i, j: (i, j)),
        out_shape=jax.ShapeDtypeStruct((m, n), BF16),
        compiler_params=_params("parallel", "parallel"),
        name="ffn_up",
    )(h, wg, wu, row_scale)


def _cast_kernel(w_ref, o_ref):
    o_ref[...] = w_ref[...].astype(o_ref.dtype)


def cast_bf16(w, layer, tb=256):
    _, r, n = w.shape
    tb = min(tb, r)
    assert r % tb == 0
    return pl.pallas_call(
        _cast_kernel,
        grid=(r // tb,),
        in_specs=[_layer_spec(layer, (tb, n), lambda i: (i, 0))],
        out_specs=pl.BlockSpec((tb, n), lambda i: (i, 0)),
        out_shape=jax.ShapeDtypeStruct((r, n), BF16),
        compiler_params=_params("parallel"),
        name="cast_bf16",
    )(w)


def _halo_specs(tb, width, col, nb, rev=False):
    r = tb // V7X_SUBLANES
    last = nb * r - 1

    def ti(i):
        return (nb - 1 - i) if rev else i

    return [
        pl.BlockSpec((V7X_SUBLANES, width), lambda i: (jnp.maximum(ti(i) * r - 1, 0), col)),
        pl.BlockSpec((tb, width), lambda i: (ti(i), col)),
        pl.BlockSpec((V7X_SUBLANES, width), lambda i: (jnp.minimum((ti(i) + 1) * r, last), col)),
    ]


def _dwconv_block(prev_ref, x_ref, next_ref, w_ref, b_ref, ext_ref, ti, nb, tb, width):
    left = width // 2
    prev = jnp.where(ti == 0, 0.0, prev_ref[...])
    nxt = jnp.where(ti == nb - 1, 0.0, next_ref[...])
    ext_ref[0:V7X_SUBLANES, :] = prev
    ext_ref[V7X_SUBLANES:V7X_SUBLANES + tb, :] = x_ref[...]
    ext_ref[V7X_SUBLANES + tb:2 * V7X_SUBLANES + tb, :] = nxt
    y = b_ref[...]
    for j in range(width):
        y = y + ext_ref[pl.ds(V7X_SUBLANES - left + j, tb), :] * w_ref[j:j + 1, :]
    return y


def _softplus(x):
    return jnp.maximum(x, 0.0) + jnp.log1p(jnp.exp(-jnp.abs(x)))


def _lru_kernel(*refs, backward, nb, tb):
    reverse = final = backward
    if final:
        xc_ref, w_ref, gb_ref, lam_ref, hf_ref, gate_ref, norm_ref, o_ref, a_ref, b_ref, carry_ref = refs
    else:
        (xp_ref, x_ref, xn_ref, cw_ref, cb_ref, w_ref, gb_ref, lam_ref,
         o_ref, xc_ref, ext_ref, a_ref, b_ref, carry_ref) = refs
    i = pl.program_id(0)
    hd = V7X_LANES
    width = o_ref.shape[-1]

    @pl.when(i == 0)
    def _():
        carry_ref[...] = jnp.zeros_like(carry_ref)

    if final:
        xc = xc_ref[...]
    else:
        xc = _dwconv_block(xp_ref, x_ref, xn_ref, cw_ref, cb_ref, ext_ref, i, nb, tb, 4)
        xc_ref[...] = xc
    sp = _softplus(-lam_ref[...])
    for h in range(LRU_HEADS):
        sl = slice(h * hd, (h + 1) * hd)
        xh = xc[:, sl]
        z = _dot(xh.astype(BF16), w_ref[h]) + gb_ref[h]
        r = jax.nn.sigmoid(z[:, :hd])
        ig = jax.nn.sigmoid(z[:, hd:])
        log_a = -LRU_C * r * sp[:, sl]
        a = jnp.exp(log_a)
        a_ref[:, sl] = a
        b_ref[:, sl] = jnp.sqrt(-jnp.tanh(log_a) * (a * a + 1.0)) * (ig * xh)

    ng = tb // V7X_SUBLANES
    row = lax.broadcasted_iota(jnp.int32, (V7X_SUBLANES, width), 0)

    def body(g, carry):
        gi = (ng - 1 - g) if reverse else g
        off = pl.multiple_of(gi * V7X_SUBLANES, V7X_SUBLANES)
        a = a_ref[pl.ds(off, V7X_SUBLANES), :]
        b = b_ref[pl.ds(off, V7X_SUBLANES), :]
        for s in (1, 2, 4):
            if reverse:
                shift, m = V7X_SUBLANES - s, row < V7X_SUBLANES - s
            else:
                shift, m = s, row >= s
            b = jnp.where(m, a * pltpu.roll(b, shift, 0) + b, b)
            a = jnp.where(m, a * pltpu.roll(a, shift, 0), a)
        hcur = a * carry + b
        b_ref[pl.ds(off, V7X_SUBLANES), :] = hcur
        return hcur[0:1, :] if reverse else hcur[V7X_SUBLANES - 1:V7X_SUBLANES, :]

    carry_ref[...] = lax.fori_loop(0, ng, body, carry_ref[...])

    if not final:
        o_ref[...] = b_ref[...]
    else:
        y = jax.nn.gelu(gate_ref[...], approximate=True) * (hf_ref[...] + b_ref[...])
        for h in range(LRU_HEADS):
            sl = slice(h * hd, (h + 1) * hd)
            yh = y[:, sl]
            yh = yh * lax.rsqrt(jnp.mean(yh * yh, axis=-1, keepdims=True) + EPS)
            o_ref[:, sl] = (yh * norm_ref[:, sl]).astype(o_ref.dtype)


def lru_branch(proj, conv_w, conv_b, wr, br, wi, bi, lam, norm, tb=512):
    s = proj.shape[0]
    c = conv_w.shape[1]
    hd = c // LRU_HEADS
    tb = min(tb, s)
    nb = s // tb
    w = jnp.concatenate([wr, wi], axis=-1).astype(BF16)
    gb = jnp.concatenate([br.reshape(2, LRU_HEADS, 1, hd), bi.reshape(2, LRU_HEADS, 1, hd)], axis=-1).astype(F32)
    full = lambda shape: pl.BlockSpec(shape, lambda i: (0,) * len(shape))
    scratch = [pltpu.VMEM((tb, c), F32), pltpu.VMEM((tb, c), F32), pltpu.VMEM((1, c), F32)]
    gates = [full((LRU_HEADS, hd, 2 * hd)), full((LRU_HEADS, 1, 2 * hd)), full((1, c))]
    fwd = pl.BlockSpec((tb, c), lambda i: (i, 0))
    h_fwd, xc = pl.pallas_call(
        functools.partial(_lru_kernel, backward=False, nb=nb, tb=tb),
        grid=(nb,),
        in_specs=_halo_specs(tb, c, 1, nb) + [full((4, c)), full((1, c))] + gates,
        out_specs=[fwd, fwd],
        out_shape=[jax.ShapeDtypeStruct((s, c), F32)] * 2,
        scratch_shapes=[pltpu.VMEM((tb + 2 * V7X_SUBLANES, c), F32)] + scratch,
        compiler_params=_params("arbitrary"),
        name="lru_fwd",
    )(proj, proj, proj, conv_w.astype(F32), conv_b.reshape(1, c).astype(F32), w[0], gb[0],
      lam[0].reshape(1, c).astype(F32))
    rev = pl.BlockSpec((tb, c), lambda i: (nb - 1 - i, 0))
    return pl.pallas_call(
        functools.partial(_lru_kernel, backward=True, nb=nb, tb=tb),
        grid=(nb,),
        in_specs=[rev] + gates + [rev, rev, full((1, c))],
        out_specs=rev,
        out_shape=jax.ShapeDtypeStruct((s, c), BF16),
        scratch_shapes=scratch,
        compiler_params=_params("arbitrary"),
        name="lru_bwd",
    )(xc, w[1], gb[1], lam[1].reshape(1, c).astype(F32), h_fwd, proj, norm.reshape(1, c).astype(F32))


def _ret_log_gamma():
    return [float(np.log1p(-np.exp2(np.float32(-5.0 - h)), dtype=np.float32)) for h in range(RET_HEADS)]


def _ret_kernel(*refs, reverse, nc, c):
    if reverse:
        (q_ref, k_ref, v0_ref, v1_ref, inv_ref, y1_ref, g0_ref, g1_ref, norm_ref,
         o_ref, state_ref, dm_ref, rope_ref, dec_ref) = refs
    else:
        q_ref, k_ref, v0_ref, v1_ref, inv_ref, o_ref, state_ref, dm_ref, rope_ref, dec_ref = refs
    i = pl.program_id(0)
    ti = (nc - 1 - i) if reverse else i
    dk = RET_KEY_DIM
    dv = v0_ref.shape[-1] * 2 // RET_HEADS
    log_g = _ret_log_gamma()
    idx = lax.broadcasted_iota(jnp.int32, (c, 1), 0).astype(F32)
    inv = inv_ref[...]

    @pl.when(i == 0)
    def _():
        state_ref[...] = jnp.zeros_like(state_ref)
        rope_ref[0] = jnp.cos(idx * inv)
        rope_ref[1] = jnp.sin(idx * inv)
        for h in range(RET_HEADS):
            q_pow, k_pow = (c - idx, idx) if reverse else (idx + 1.0, c - 1.0 - idx)
            dec_ref[h, 0] = jnp.broadcast_to(jnp.exp(log_g[h] * q_pow), (c, dk))
            dec_ref[h, 1] = jnp.broadcast_to(jnp.exp(log_g[h] * k_pow), (c, dk))
        if not reverse:
            d = jnp.abs(lax.broadcasted_iota(jnp.int32, (c, c), 0)
                        - lax.broadcasted_iota(jnp.int32, (c, c), 1)).astype(F32)
            for h in range(RET_HEADS):
                dm_ref[h] = jnp.exp(log_g[h] * d)

    start = (ti * c).astype(F32) * inv
    cos_s, sin_s = jnp.cos(start), jnp.sin(start)
    cos_o, sin_o = rope_ref[0], rope_ref[1]
    lane = lax.broadcasted_iota(jnp.int32, (c, dk), 1)
    cos_t = cos_s * cos_o - sin_s * sin_o
    sin_t = jnp.where(lane < dk // 2, -1.0, 1.0) * (sin_s * cos_o + cos_s * sin_o)

    def rot(x):
        return x * cos_t + pltpu.roll(x, dk // 2, 1) * sin_t

    hpb = RET_HEADS // 2
    for h in range(RET_HEADS):
        qh = rot(q_ref[:, h * dk:(h + 1) * dk])
        kh = rot(k_ref[:, h * dk:(h + 1) * dk]) * (dk ** -0.5)
        v_ref = v0_ref if h < hpb else v1_ref
        vs = slice((h % hpb) * dv, (h % hpb + 1) * dv)
        vh = v_ref[:, vs].astype(BF16)
        lg = log_g[h]
        q_dec = qh * dec_ref[h, 0]
        k_dec = kh * dec_ref[h, 1]
        if reverse:
            y = y1_ref[:, h * dv:(h + 1) * dv]
        else:
            scores = lax.dot_general(qh.astype(BF16), kh.astype(BF16), (((1,), (1,)), ((), ())),
                                     preferred_element_type=F32) * dm_ref[h]
            y = _dot(scores.astype(BF16), vh)
        st = state_ref[h]
        y = y + _dot(q_dec.astype(BF16), st.astype(BF16))
        kv = lax.dot_general(k_dec.astype(BF16), vh, (((0,), (0,)), ((), ())), preferred_element_type=F32)
        state_ref[h] = math.exp(lg * c) * st + kv
        if reverse:
            g_ref = g0_ref if h < hpb else g1_ref
            yn = y * lax.rsqrt(jnp.mean(y * y, axis=-1, keepdims=True) + EPS) * norm_ref[:, h * dv:(h + 1) * dv]
            o_ref[:, h * dv:(h + 1) * dv] = (jax.nn.silu(g_ref[:, vs]) * yn).astype(o_ref.dtype)
        else:
            o_ref[:, h * dv:(h + 1) * dv] = y


def retention_branch(proj, norm, col0, c=256):
    s = proj.shape[0]
    d_ret = norm.shape[0]
    qk = RET_HEADS * RET_KEY_DIM
    c = min(c, s)
    nc = s // c
    half = RET_KEY_DIM // 2
    inv = ROPE_BASE ** (-jnp.arange(half, dtype=F32) / half)
    inv = jnp.concatenate([inv, inv]).reshape(1, RET_KEY_DIM)
    vw = d_ret // 2
    qb, kb = col0 // qk, (col0 + qk) // qk
    vb = (col0 + 2 * qk) // vw
    gb = (col0 + 2 * qk + d_ret) // vw
    assert col0 % qk == 0 and (col0 + 2 * qk) % vw == 0
    scratch = [pltpu.VMEM((RET_HEADS, RET_KEY_DIM, d_ret // RET_HEADS), F32), pltpu.VMEM((RET_HEADS, c, c), F32),
               pltpu.VMEM((2, c, RET_KEY_DIM), F32), pltpu.VMEM((RET_HEADS, 2, c, RET_KEY_DIM), F32)]

    def specs(ti):
        return [pl.BlockSpec((c, qk), lambda i: (ti(i), qb)), pl.BlockSpec((c, qk), lambda i: (ti(i), kb)),
                pl.BlockSpec((c, vw), lambda i: (ti(i), vb)), pl.BlockSpec((c, vw), lambda i: (ti(i), vb + 1)),
                pl.BlockSpec((1, RET_KEY_DIM), lambda i: (0, 0))]

    fwd = lambda i: i
    y1 = pl.pallas_call(
        functools.partial(_ret_kernel, reverse=False, nc=nc, c=c),
        grid=(nc,),
        in_specs=specs(fwd),
        out_specs=pl.BlockSpec((c, d_ret), lambda i: (i, 0)),
        out_shape=jax.ShapeDtypeStruct((s, d_ret), F32),
        scratch_shapes=scratch,
        compiler_params=_params("arbitrary"),
        name="ret_fwd",
    )(proj, proj, proj, proj, inv)
    rev = lambda i: nc - 1 - i
    return pl.pallas_call(
        functools.partial(_ret_kernel, reverse=True, nc=nc, c=c),
        grid=(nc,),
        in_specs=specs(rev) + [pl.BlockSpec((c, d_ret), lambda i: (rev(i), 0)),
                               pl.BlockSpec((c, vw), lambda i: (rev(i), gb)),
                               pl.BlockSpec((c, vw), lambda i: (rev(i), gb + 1)),
                               pl.BlockSpec((1, d_ret), lambda i: (0, 0))],
        out_specs=pl.BlockSpec((c, d_ret), lambda i: (rev(i), 0)),
        out_shape=jax.ShapeDtypeStruct((s, d_ret), BF16),
        scratch_shapes=scratch,
        compiler_params=_params("arbitrary"),
        name="ret_bwd",
    )(proj, proj, proj, proj, inv, y1, proj, proj, norm.reshape(1, d_ret).astype(F32))


def _hy_conv_kernel(xp_ref, x_ref, xn_ref, w_ref, b_ref, o_ref, ext_ref, *, nb, tb):
    ti = pl.program_id(1)
    o_ref[...] = _dwconv_block(xp_ref, x_ref, xn_ref, w_ref, b_ref, ext_ref, ti, nb, tb, 3)


def hyena_short_conv(proj, conv_w, conv_b, col_block, c, tb=512):
    s = proj.shape[0]
    tb = min(tb, s)
    nb = s // tb
    r = tb // V7X_SUBLANES
    last = nb * r - 1
    return pl.pallas_call(
        functools.partial(_hy_conv_kernel, nb=nb, tb=tb),
        grid=(3, nb),
        in_specs=[
            pl.BlockSpec((V7X_SUBLANES, c), lambda j, i: (jnp.maximum(i * r - 1, 0), col_block + j)),
            pl.BlockSpec((tb, c), lambda j, i: (i, col_block + j)),
            pl.BlockSpec((V7X_SUBLANES, c), lambda j, i: (jnp.minimum((i + 1) * r, last), col_block + j)),
            pl.BlockSpec((3, c), lambda j, i: (0, j)),
            pl.BlockSpec((1, c), lambda j, i: (0, j)),
        ],
        out_specs=pl.BlockSpec((None, tb, c), lambda j, i: (j, i, 0)),
        out_shape=jax.ShapeDtypeStruct((3, s, c), F32),
        scratch_shapes=[pltpu.VMEM((tb + 2 * V7X_SUBLANES, c), F32)],
        compiler_params=_params("parallel", "parallel"),
        name="hyena_short_conv",
    )(proj, proj, proj, conv_w.astype(F32), conv_b.reshape(1, -1).astype(F32))


def _hy_filter_kernel(fb_ref, w1_ref, b1_ref, w2_ref, b2_ref, w3_ref, freq_ref, decay_ref, o_ref, *, length, tb):
    i = pl.program_id(0)
    hi = lax.Precision.HIGHEST
    idx = (i * tb).astype(F32) + lax.broadcasted_iota(jnp.int32, (tb, 1), 0).astype(F32)
    t = idx / (length - 1.0)
    omega = (2.0 * math.pi / length) * idx
    lane = lax.broadcasted_iota(jnp.int32, (tb, V7X_LANES), 1)
    phase = fb_ref[...] * omega
    feats = jnp.where(lane == 0, t, jnp.where(lane <= HYENA_BANDS, jnp.cos(phase),
                                              jnp.where(lane <= 2 * HYENA_BANDS, -jnp.sin(phase), 0.0)))
    freq = freq_ref[...]
    hdn = jnp.sin(freq * (jnp.dot(feats, w1_ref[...], precision=hi, preferred_element_type=F32) + b1_ref[...]))
    hdn = jnp.sin(freq * (jnp.dot(hdn, w2_ref[...], precision=hi, preferred_element_type=F32) + b2_ref[...]))
    filt = _dot(hdn.astype(BF16), w3_ref[...])
    o_ref[...] = filt * jnp.exp(-t * jnp.abs(decay_ref[...]))


def hyena_filters(length, w1, b1, w2, b2, w3, freq, decay, tb=512):
    emb, hid = w1.shape
    bands = (emb - 1) // 2
    assert bands == HYENA_BANDS
    n_out = w3.shape[1]
    tb = min(tb, length)
    f = jnp.linspace(1e-4, bands - 1, bands, dtype=F32)
    fb = jnp.zeros((1, V7X_LANES), F32).at[0, 1:1 + bands].set(f).at[0, 1 + bands:1 + 2 * bands].set(f)
    w1p = jnp.zeros((V7X_LANES, hid), F32).at[:emb].set(w1.astype(F32))
    full = lambda shape: pl.BlockSpec(shape, lambda i: (0,) * len(shape))
    return pl.pallas_call(
        functools.partial(_hy_filter_kernel, length=length, tb=tb),
        grid=(length // tb,),
        in_specs=[full((1, V7X_LANES)), full((V7X_LANES, hid)), full((1, hid)), full((hid, hid)), full((1, hid)),
                  full((hid, n_out)), full((1, hid)), full((1, n_out))],
        out_specs=pl.BlockSpec((tb, n_out), lambda i: (i, 0)),
        out_shape=jax.ShapeDtypeStruct((length, n_out), F32),
        compiler_params=_params("parallel"),
        name="hyena_filters",
    )(fb, w1p, b1.reshape(1, hid).astype(F32), w2.astype(F32), b2.reshape(1, hid).astype(F32), w3.astype(BF16),
      freq.reshape(1, hid).astype(F32), decay.reshape(1, n_out).astype(F32))


def _fft_sizes(length):
    n = 2 * length
    n2 = min(FFT_N2, n // 4)
    n1 = n // n2
    return n, n1, n2, n1 // 2, n1 // 2 + 1


def _fft_tables(length):
    n, n1, n2, n1h, k1n = _fft_sizes(length)
    eye = np.eye(V7X_SUBLANES)
    k1 = np.arange(k1n)[:, None]
    a1 = np.arange(n1h)[None, :]
    ang1 = 2.0 * np.pi * ((k1 * a1) % n1) / n1
    f1 = np.concatenate([np.kron(np.cos(ang1), eye), np.kron(-np.sin(ang1), eye)], axis=0)
    wgt = np.full((k1n,), 2.0)
    wgt[0] = wgt[-1] = 1.0
    cw = (np.cos(ang1) * wgt[:, None] / n).T
    sw = (np.sin(ang1) * wgt[:, None] / n).T
    b3 = np.concatenate([np.kron(cw, eye), np.kron(-sw, eye)], axis=1)
    a2 = np.arange(n2)
    angt = 2.0 * np.pi * (k1 * a2[None, :]) / n
    tw = jnp.asarray(np.stack([np.cos(angt), np.sin(angt)]), F32)
    tw = jnp.broadcast_to(tw[..., None], tw.shape + (V7X_LANES,))
    ang2 = 2.0 * np.pi * ((a2[:, None] * a2[None, :]) % n2) / n2
    c2, s2 = np.cos(ang2), np.sin(ang2)
    f2 = np.block([[c2, s2], [-s2, c2]])
    f2i = np.block([[c2, -s2], [s2, c2]])
    return jnp.asarray(f1, BF16), tw, jnp.asarray(f2, BF16), jnp.asarray(f2i, BF16), jnp.asarray(b3, BF16)


def _pack_pair(re, im):
    hi = lax.bitcast_convert_type(re.astype(BF16).astype(F32), jnp.uint32)
    lo = lax.bitcast_convert_type(im.astype(BF16).astype(F32), jnp.uint32)
    return hi | (lo >> 16)


def _unpack_pair(w):
    re = lax.bitcast_convert_type(w & jnp.uint32(0xFFFF0000), F32)
    im = lax.bitcast_convert_type(w << 16, F32)
    return re, im


def _twiddle(tw_ref, xr, xi, conj):
    c, s = tw_ref[0], tw_ref[1]
    if conj:
        s = -s
    out_r, out_i = [], []
    for g in range(xr.shape[-1] // V7X_LANES):
        sl = slice(g * V7X_LANES, (g + 1) * V7X_LANES)
        out_r.append(xr[..., sl] * c + xi[..., sl] * s)
        out_i.append(xi[..., sl] * c - xr[..., sl] * s)
    return jnp.concatenate(out_r, axis=-1), jnp.concatenate(out_i, axis=-1)


def _fft1_kernel(z_ref, f1_ref, tw_ref, a_ref):
    n1h, sub, cb = z_ref.shape
    k1n = a_ref.shape[0]
    z = z_ref[...].reshape(n1h * sub, cb).astype(BF16)
    a = _dot(f1_ref[...], z)
    ar = a[:k1n * sub].reshape(k1n, sub, cb)
    ai = a[k1n * sub:].reshape(k1n, sub, cb)
    a_ref[...] = _pack_pair(*_twiddle(tw_ref, ar, ai, False))


def fft_stage1(z, f1, tw, length, cb=1024):
    _, _, n2, n1h, k1n = _fft_sizes(length)
    c = z.shape[-1]
    cb = min(cb, c)
    lead = z.shape[:-2]
    z4 = z.reshape(lead + (n1h, n2, c))
    nl = len(lead)
    sub = V7X_SUBLANES
    return pl.pallas_call(
        _fft1_kernel,
        grid=(n2 // sub, c // cb),
        in_specs=[pl.BlockSpec((None,) * nl + (n1h, sub, cb), lambda j, ci: (0,) * nl + (0, j, ci)),
                  pl.BlockSpec(f1.shape, lambda j, ci: (0, 0)),
                  pl.BlockSpec((2, k1n, sub, V7X_LANES), lambda j, ci: (0, 0, j, 0))],
        out_specs=pl.BlockSpec((k1n, sub, cb), lambda j, ci: (0, j, ci)),
        out_shape=jax.ShapeDtypeStruct((k1n, n2, c), jnp.uint32),
        compiler_params=_params("parallel", "parallel"),
        name="fft_stage1",
    )(z4, f1, tw)


def _slab_dft(t_ref, xr, xi):
    n2 = xr.shape[0]
    x = jnp.concatenate([xr.astype(BF16), xi.astype(BF16)], axis=0)
    y = _dot(t_ref[...], x)
    return y[:n2], y[n2:]


def _slab_specs(n2):
    tw = pl.BlockSpec((2, None, n2, V7X_LANES), lambda k, j: (0, k, 0, 0))
    f2 = pl.BlockSpec((2 * n2, 2 * n2), lambda k, j: (0, 0))
    return tw, f2


def _filter_spec_kernel(af_ref, ab_ref, f2_ref, g_ref):
    fr, fi = _slab_dft(f2_ref, *_unpack_pair(af_ref[...]))
    br, bi = _slab_dft(f2_ref, *_unpack_pair(ab_ref[...]))
    g_ref[...] = _pack_pair(fr + br, fi - bi)


def filter_spectrum(a, f2, c, cb=1024):
    k1n, n2, cf = a.shape
    orders = cf // (2 * c)
    cb = min(cb, c)
    per = c // cb
    fcol = lambda k, j: (k, 0, (j // per) * 2 * per + j % per)
    bcol = lambda k, j: (k, 0, (j // per) * 2 * per + per + j % per)
    blk = (None, n2, cb)
    _, f2_spec = _slab_specs(n2)
    return pl.pallas_call(
        _filter_spec_kernel,
        grid=(k1n, orders * per),
        in_specs=[pl.BlockSpec(blk, fcol), pl.BlockSpec(blk, bcol), f2_spec],
        out_specs=pl.BlockSpec(blk, lambda k, j: (k, 0, j)),
        out_shape=jax.ShapeDtypeStruct((k1n, n2, orders * c), jnp.uint32),
        compiler_params=_params("parallel", "parallel"),
        name="hyena_filter_spectrum",
    )(a, a, f2)


def _slab_conv_kernel(a_ref, g_ref, tw_ref, f2_ref, f2i_ref, p_ref):
    xr, xi = _slab_dft(f2_ref, *_unpack_pair(a_ref[...]))
    gr, gi = _unpack_pair(g_ref[...])
    qr, qi = _slab_dft(f2i_ref, xr * gr - xi * gi, xr * gi + xi * gr)
    p_ref[...] = _pack_pair(*_twiddle(tw_ref, qr, qi, True))


def slab_conv(a, g, tw, f2, f2i, order, cb=1024):
    k1n, n2, c = a.shape
    cb = min(cb, c)
    per = c // cb
    blk = (None, n2, cb)
    dcol = lambda k, j: (k, 0, j)
    gcol = lambda k, j: (k, 0, order * per + j)
    tw_spec, f2_spec = _slab_specs(n2)
    return pl.pallas_call(
        _slab_conv_kernel,
        grid=(k1n, per),
        in_specs=[pl.BlockSpec(blk, dcol), pl.BlockSpec(blk, gcol), tw_spec, f2_spec, f2_spec],
        out_specs=pl.BlockSpec(blk, dcol),
        out_shape=jax.ShapeDtypeStruct((k1n, n2, c), jnp.uint32),
        compiler_params=_params("parallel", "parallel"),
        name="hyena_slab_conv",
    )(a, g, tw, f2, f2i)


def _ifft3_kernel(p_ref, b3_ref, z_ref, gate_ref, skip_ref, o_ref):
    k1n, sub, cb = p_ref.shape
    n1h = o_ref.shape[0]
    pr, pi = _unpack_pair(p_ref[...])
    p = jnp.concatenate([pr.reshape(k1n * sub, cb).astype(BF16), pi.reshape(k1n * sub, cb).astype(BF16)], axis=0)
    y = _dot(b3_ref[...], p).reshape(n1h, sub, cb)
    z = z_ref[...]
    o_ref[...] = gate_ref[...] * (y + skip_ref[...] * z)


def ifft_stage3_gate(p, b3, u, z_arr, gate_idx, skip, length, cb=1024):
    _, _, n2, n1h, k1n = _fft_sizes(length)
    c = p.shape[-1]
    cb = min(cb, c)
    sub = V7X_SUBLANES
    u4 = u.reshape(u.shape[0], n1h, n2, c)
    z4 = z_arr.reshape((-1, n1h, n2, c))
    pblk = pl.BlockSpec((k1n, sub, cb), lambda j, ci: (0, j, ci))
    out = pl.pallas_call(
        _ifft3_kernel,
        grid=(n2 // sub, c // cb),
        in_specs=[pblk, pl.BlockSpec(b3.shape, lambda j, ci: (0, 0)),
                  pl.BlockSpec((None, n1h, sub, cb), lambda j, ci: (0, 0, j, ci)),
                  pl.BlockSpec((None, n1h, sub, cb), lambda j, ci: (gate_idx, 0, j, ci)),
                  pl.BlockSpec((1, cb), lambda j, ci: (0, ci))],
        out_specs=pl.BlockSpec((n1h, sub, cb), lambda j, ci: (0, j, ci)),
        out_shape=jax.ShapeDtypeStruct((n1h, n2, c), F32),
        compiler_params=_params("parallel", "parallel"),
        name="ifft_stage3_gate",
    )(p, b3, z4, u4, skip.reshape(1, c).astype(F32))
    return out.reshape(1, length, c)


def hyena_branch(proj, tables, conv_w, conv_b, f_w1, f_b1, f_w2, f_b2, f_w3, freq, decay, skip, norm, col_block):
    length = proj.shape[0]
    c = norm.shape[0]
    f1, tw, f2, f2i, b3 = tables
    u = hyena_short_conv(proj, conv_w, conv_b, col_block, c)
    filt = hyena_filters(length, f_w1, f_b1, f_w2, f_b2, f_w3, freq, decay)
    g = filter_spectrum(fft_stage1(filt, f1, tw, length), f2, c)
    z = u
    for o in range(skip.shape[0]):
        p = slab_conv(fft_stage1(z, f1, tw, length), g, tw, f2, f2i, o)
        z = ifft_stage3_gate(p, b3, u, z, 1 + o, skip[o], length)
    return group_norm(z.reshape(length, c), norm, HYENA_GROUPS, BF16)


def kernel(x, norm_mix, w_in, lru_conv_w, lru_conv_b, lru_wr, lru_br, lru_wi, lru_bi, lru_lambda, lru_norm, hy_conv_w, hy_conv_b, hy_f_w1, hy_f_b1, hy_f_w2, hy_f_b2, hy_f_w3, hy_freq, hy_decay, hy_skip, hy_norm, ret_norm, w_out, norm_ffn, w_gate, w_up, w_down, norm_final):
    b, s, d = x.shape
    assert b == 1
    depth = w_in.shape[0]
    d_lru = lru_conv_w.shape[-1]
    d_hy = hy_norm.shape[-1]
    tables = _fft_tables(s)
    xs = x.reshape(s, d)
    h, rstd = rmsnorm(xs, norm_mix[0], BF16), None
    for l in range(depth):
        proj = matmul_fullk([h], w_in, l, row_scale=rstd)
        y_a = lru_branch(proj, lru_conv_w[l], lru_conv_b[l], lru_wr[l], lru_br[l], lru_wi[l], lru_bi[l],
                         lru_lambda[l], lru_norm[l])
        y_b = hyena_branch(proj, tables, hy_conv_w[l], hy_conv_b[l], hy_f_w1[l], hy_f_b1[l], hy_f_w2[l],
                           hy_f_b2[l], hy_f_w3[l], hy_freq[l], hy_decay[l], hy_skip[l], hy_norm[l],
                           (2 * d_lru) // d_hy)
        y_c = retention_branch(proj, ret_norm[l], 2 * d_lru + 3 * d_hy)
        xs, h, rstd = matmul_fullk([y_a, y_b, y_c], w_out, l, residual=xs, norm_gain=norm_ffn[l], tm=1024, tn=512)
        act = ffn_up(h, w_gate, w_up, l, rstd)
        w_dn = cast_bf16(w_down, l)
        if l + 1 < depth:
            xs, h, rstd = matmul_fullk([act], w_dn, residual=xs, norm_gain=norm_mix[l + 1], tm=512, tn=512)
        else:
            xs = matmul_fullk([act], w_dn, residual=xs, tm=512, tn=512)
    return rmsnorm(xs, norm_final, x.dtype).reshape(b, s, d)
```

```python
import functools
import math

import numpy as np
import jax
import jax.numpy as jnp
from jax import lax
from jax.experimental import pallas as pl
from jax.experimental.pallas import tpu as pltpu

EPS = 1e-6
LRU_HEADS = 8
LRU_C = 8.0
HYENA_GROUPS = 8
HYENA_BANDS = 16
RET_HEADS = 8
RET_KEY_DIM = 128
ROPE_BASE = 10000.0

V7X_SUBLANES = 8
V7X_LANES = 128
VMEM_LIMIT_BYTES = 56 * 1024 * 1024
FFT_N2 = 256
F32 = jnp.float32
BF16 = jnp.bfloat16


def _params(*sem):
    return pltpu.CompilerParams(dimension_semantics=sem, vmem_limit_bytes=VMEM_LIMIT_BYTES)


def _dot(a, b):
    return jnp.dot(a, b, preferred_element_type=F32)


def _rmsnorm_kernel(x_ref, g_ref, o_ref):
    x = x_ref[...]
    y = x * lax.rsqrt(jnp.mean(x * x, axis=-1, keepdims=True) + EPS)
    o_ref[...] = (y * g_ref[...]).astype(o_ref.dtype)


def rmsnorm(x, gain, out_dtype, tm=256):
    s, d = x.shape
    tm = min(tm, s)
    return pl.pallas_call(
        _rmsnorm_kernel,
        grid=(s // tm,),
        in_specs=[pl.BlockSpec((tm, d), lambda i: (i, 0)), pl.BlockSpec((1, d), lambda i: (0, 0))],
        out_specs=pl.BlockSpec((tm, d), lambda i: (i, 0)),
        out_shape=jax.ShapeDtypeStruct((s, d), out_dtype),
        compiler_params=_params("parallel"),
        name="rmsnorm",
    )(x, gain.reshape(1, d).astype(F32))


def _group_norm_kernel(x_ref, g_ref, o_ref, *, groups):
    x = x_ref[...]
    w = x.shape[-1] // groups
    for h in range(groups):
        xh = x[:, h * w:(h + 1) * w]
        yh = xh * lax.rsqrt(jnp.mean(xh * xh, axis=-1, keepdims=True) + EPS)
        o_ref[:, h * w:(h + 1) * w] = (yh * g_ref[:, h * w:(h + 1) * w]).astype(o_ref.dtype)


def group_norm(x, gain, groups, out_dtype, tm=512):
    s, d = x.shape
    tm = min(tm, s)
    return pl.pallas_call(
        functools.partial(_group_norm_kernel, groups=groups),
        grid=(s // tm,),
        in_specs=[pl.BlockSpec((tm, d), lambda i: (i, 0)), pl.BlockSpec((1, d), lambda i: (0, 0))],
        out_specs=pl.BlockSpec((tm, d), lambda i: (i, 0)),
        out_shape=jax.ShapeDtypeStruct((s, d), out_dtype),
        compiler_params=_params("parallel"),
        name="group_norm",
    )(x, gain.reshape(1, d).astype(F32))


def _layer_spec(layer, block, index_map):
    return pl.BlockSpec((None,) + block, lambda *g: (layer,) + index_map(*g))


def _mm_fullk_kernel(*refs, n_a, has_res):
    a_refs, b_ref, o_ref = refs[:n_a], refs[n_a], refs[-1]
    acc, off = None, 0
    for a_ref in a_refs:
        kw = a_ref.shape[1]
        part = _dot(a_ref[...], b_ref[off:off + kw, :].astype(BF16))
        acc = part if acc is None else acc + part
        off += kw
    if has_res:
        acc = refs[n_a + 1][...] + acc
    o_ref[...] = acc.astype(o_ref.dtype)


def matmul_fullk(a_parts, w, layer=None, residual=None, out_dtype=F32, tm=2048, tn=256):
    m = a_parts[0].shape[0]
    kd, n = w.shape[-2:]
    assert sum(a.shape[1] for a in a_parts) == kd
    tm, tn = min(tm, m), min(tn, n)
    has_res = residual is not None
    in_specs = [pl.BlockSpec((tm, a.shape[1]), lambda i, j: (i, 0)) for a in a_parts]
    if layer is None:
        in_specs.append(pl.BlockSpec((kd, tn), lambda i, j: (0, j)))
    else:
        in_specs.append(_layer_spec(layer, (kd, tn), lambda i, j: (0, j)))
    args = list(a_parts) + [w]
    if has_res:
        in_specs.append(pl.BlockSpec((tm, tn), lambda i, j: (i, j)))
        args.append(residual)
    return pl.pallas_call(
        functools.partial(_mm_fullk_kernel, n_a=len(a_parts), has_res=has_res),
        grid=(m // tm, n // tn),
        in_specs=in_specs,
        out_specs=pl.BlockSpec((tm, tn), lambda i, j: (i, j)),
        out_shape=jax.ShapeDtypeStruct((m, n), out_dtype),
        compiler_params=_params("parallel", "parallel"),
        name="matmul_fullk_res" if has_res else "matmul_fullk",
    )(*args)


def _ffn_up_kernel(h_ref, wg_ref, wu_ref, o_ref):
    h = h_ref[...]
    g = _dot(h, wg_ref[...].astype(BF16))
    u = _dot(h, wu_ref[...].astype(BF16))
    o_ref[...] = (jax.nn.silu(g) * u).astype(o_ref.dtype)


def ffn_up(h, wg, wu, layer, tm=2048, tn=256):
    m, kd = h.shape
    n = wg.shape[2]
    tm, tn = min(tm, m), min(tn, n)
    assert n % tn == 0
    wspec = _layer_spec(layer, (kd, tn), lambda i, j: (0, j))
    return pl.pallas_call(
        _ffn_up_kernel,
        grid=(m // tm, n // tn),
        in_specs=[pl.BlockSpec((tm, kd), lambda i, j: (i, 0)), wspec, wspec],
        out_specs=pl.BlockSpec((tm, tn), lambda i, j: (i, j)),
        out_shape=jax.ShapeDtypeStruct((m, n), BF16),
        compiler_params=_params("parallel", "parallel"),
        name="ffn_up",
    )(h, wg, wu)


def _cast_kernel(w_ref, o_ref):
    o_ref[...] = w_ref[...].astype(o_ref.dtype)


def cast_bf16(w, layer, tb=256):
    _, r, n = w.shape
    tb = min(tb, r)
    assert r % tb == 0
    return pl.pallas_call(
        _cast_kernel,
        grid=(r // tb,),
        in_specs=[_layer_spec(layer, (tb, n), lambda i: (i, 0))],
        out_specs=pl.BlockSpec((tb, n), lambda i: (i, 0)),
        out_shape=jax.ShapeDtypeStruct((r, n), BF16),
        compiler_params=_params("parallel"),
        name="cast_bf16",
    )(w)


def _halo_specs(tb, width, col, nb):
    r = tb // V7X_SUBLANES
    last = nb * r - 1
    return [
        pl.BlockSpec((V7X_SUBLANES, width), lambda i: (jnp.maximum(i * r - 1, 0), col)),
        pl.BlockSpec((tb, width), lambda i: (i, col)),
        pl.BlockSpec((V7X_SUBLANES, width), lambda i: (jnp.minimum((i + 1) * r, last), col)),
    ]


def _dwconv_block(prev_ref, x_ref, next_ref, w_ref, b_ref, ext_ref, ti, nb, tb, width):
    left = width // 2
    prev = jnp.where(ti == 0, 0.0, prev_ref[...])
    nxt = jnp.where(ti == nb - 1, 0.0, next_ref[...])
    ext_ref[0:V7X_SUBLANES, :] = prev
    ext_ref[V7X_SUBLANES:V7X_SUBLANES + tb, :] = x_ref[...]
    ext_ref[V7X_SUBLANES + tb:2 * V7X_SUBLANES + tb, :] = nxt
    y = b_ref[...]
    for j in range(width):
        y = y + ext_ref[pl.ds(V7X_SUBLANES - left + j, tb), :] * w_ref[j:j + 1, :]
    return y


def _softplus(x):
    return jnp.maximum(x, 0.0) + jnp.log1p(jnp.exp(-jnp.abs(x)))


def _lru_kernel(*refs, backward, nb, tb):
    reverse = final = backward
    if final:
        xc_ref, w_ref, gb_ref, lam_ref, hf_ref, gate_ref, norm_ref, o_ref, a_ref, b_ref, carry_ref = refs
    else:
        (xp_ref, x_ref, xn_ref, cw_ref, cb_ref, w_ref, gb_ref, lam_ref,
         o_ref, xc_ref, ext_ref, a_ref, b_ref, carry_ref) = refs
    i = pl.program_id(0)
    hd = V7X_LANES
    width = o_ref.shape[-1]

    @pl.when(i == 0)
    def _():
        carry_ref[...] = jnp.zeros_like(carry_ref)

    if final:
        xc = xc_ref[...]
    else:
        xc = _dwconv_block(xp_ref, x_ref, xn_ref, cw_ref, cb_ref, ext_ref, i, nb, tb, 4)
        xc_ref[...] = xc
    sp = _softplus(-lam_ref[...])
    for h in range(LRU_HEADS):
        sl = slice(h * hd, (h + 1) * hd)
        xh = xc[:, sl]
        z = _dot(xh.astype(BF16), w_ref[h]) + gb_ref[h]
        r = jax.nn.sigmoid(z[:, :hd])
        ig = jax.nn.sigmoid(z[:, hd:])
        log_a = -LRU_C * r * sp[:, sl]
        a = jnp.exp(log_a)
        a_ref[:, sl] = a
        b_ref[:, sl] = jnp.sqrt(-jnp.tanh(log_a) * (a * a + 1.0)) * (ig * xh)

    ng = tb // V7X_SUBLANES
    row = lax.broadcasted_iota(jnp.int32, (V7X_SUBLANES, width), 0)

    def body(g, carry):
        gi = (ng - 1 - g) if reverse else g
        off = pl.multiple_of(gi * V7X_SUBLANES, V7X_SUBLANES)
        a = a_ref[pl.ds(off, V7X_SUBLANES), :]
        b = b_ref[pl.ds(off, V7X_SUBLANES), :]
        for s in (1, 2, 4):
            if reverse:
                shift, m = V7X_SUBLANES - s, row < V7X_SUBLANES - s
            else:
                shift, m = s, row >= s
            b = jnp.where(m, a * pltpu.roll(b, shift, 0) + b, b)
            a = jnp.where(m, a * pltpu.roll(a, shift, 0), a)
        hcur = a * carry + b
        b_ref[pl.ds(off, V7X_SUBLANES), :] = hcur
        return hcur[0:1, :] if reverse else hcur[V7X_SUBLANES - 1:V7X_SUBLANES, :]

    carry_ref[...] = lax.fori_loop(0, ng, body, carry_ref[...])

    if not final:
        o_ref[...] = b_ref[...]
    else:
        y = jax.nn.gelu(gate_ref[...], approximate=True) * (hf_ref[...] + b_ref[...])
        for h in range(LRU_HEADS):
            sl = slice(h * hd, (h + 1) * hd)
            yh = y[:, sl]
            yh = yh * lax.rsqrt(jnp.mean(yh * yh, axis=-1, keepdims=True) + EPS)
            o_ref[:, sl] = (yh * norm_ref[:, sl]).astype(o_ref.dtype)


def lru_branch(proj, conv_w, conv_b, wr, br, wi, bi, lam, norm, tb=512):
    s = proj.shape[0]
    c = conv_w.shape[1]
    hd = c // LRU_HEADS
    tb = min(tb, s)
    nb = s // tb
    w = jnp.concatenate([wr, wi], axis=-1).astype(BF16)
    gb = jnp.concatenate([br.reshape(2, LRU_HEADS, 1, hd), bi.reshape(2, LRU_HEADS, 1, hd)], axis=-1).astype(F32)
    full = lambda shape: pl.BlockSpec(shape, lambda i: (0,) * len(shape))
    scratch = [pltpu.VMEM((tb, c), F32), pltpu.VMEM((tb, c), F32), pltpu.VMEM((1, c), F32)]
    gates = [full((LRU_HEADS, hd, 2 * hd)), full((LRU_HEADS, 1, 2 * hd)), full((1, c))]
    fwd = pl.BlockSpec((tb, c), lambda i: (i, 0))
    h_fwd, xc = pl.pallas_call(
        functools.partial(_lru_kernel, backward=False, nb=nb, tb=tb),
        grid=(nb,),
        in_specs=_halo_specs(tb, c, 1, nb) + [full((4, c)), full((1, c))] + gates,
        out_specs=[fwd, fwd],
        out_shape=[jax.ShapeDtypeStruct((s, c), F32)] * 2,
        scratch_shapes=[pltpu.VMEM((tb + 2 * V7X_SUBLANES, c), F32)] + scratch,
        compiler_params=_params("arbitrary"),
        name="lru_fwd",
    )(proj, proj, proj, conv_w.astype(F32), conv_b.reshape(1, c).astype(F32), w[0], gb[0],
      lam[0].reshape(1, c).astype(F32))
    rev = pl.BlockSpec((tb, c), lambda i: (nb - 1 - i, 0))
    return pl.pallas_call(
        functools.partial(_lru_kernel, backward=True, nb=nb, tb=tb),
        grid=(nb,),
        in_specs=[rev] + gates + [rev, rev, full((1, c))],
        out_specs=rev,
        out_shape=jax.ShapeDtypeStruct((s, c), BF16),
        scratch_shapes=scratch,
        compiler_params=_params("arbitrary"),
        name="lru_bwd",
    )(xc, w[1], gb[1], lam[1].reshape(1, c).astype(F32), h_fwd, proj, norm.reshape(1, c).astype(F32))


def _ret_log_gamma():
    return [float(np.log1p(-np.exp2(np.float32(-5.0 - h)), dtype=np.float32)) for h in range(RET_HEADS)]


def _ret_kernel(*refs, reverse, nc, c):
    if reverse:
        (q_ref, k_ref, v0_ref, v1_ref, inv_ref, y1_ref, g0_ref, g1_ref, norm_ref,
         o_ref, state_ref, dm_ref, rope_ref, dec_ref) = refs
    else:
        q_ref, k_ref, v0_ref, v1_ref, inv_ref, o_ref, state_ref, dm_ref, rope_ref, dec_ref = refs
    i = pl.program_id(0)
    ti = (nc - 1 - i) if reverse else i
    dk = RET_KEY_DIM
    dv = v0_ref.shape[-1] * 2 // RET_HEADS
    log_g = _ret_log_gamma()
    idx = lax.broadcasted_iota(jnp.int32, (c, 1), 0).astype(F32)
    inv = inv_ref[...]

    @pl.when(i == 0)
    def _():
        state_ref[...] = jnp.zeros_like(state_ref)
        rope_ref[0] = jnp.cos(idx * inv)
        rope_ref[1] = jnp.sin(idx * inv)
        for h in range(RET_HEADS):
            q_pow, k_pow = (c - idx, idx) if reverse else (idx + 1.0, c - 1.0 - idx)
            dec_ref[h, 0] = jnp.broadcast_to(jnp.exp(log_g[h] * q_pow), (c, dk))
            dec_ref[h, 1] = jnp.broadcast_to(jnp.exp(log_g[h] * k_pow), (c, dk))
        if not reverse:
            d = jnp.abs(lax.broadcasted_iota(jnp.int32, (c, c), 0)
                        - lax.broadcasted_iota(jnp.int32, (c, c), 1)).astype(F32)
            for h in range(RET_HEADS):
                dm_ref[h] = jnp.exp(log_g[h] * d)

    start = (ti * c).astype(F32) * inv
    cos_s, sin_s = jnp.cos(start), jnp.sin(start)
    cos_o, sin_o = rope_ref[0], rope_ref[1]
    lane = lax.broadcasted_iota(jnp.int32, (c, dk), 1)
    cos_t = cos_s * cos_o - sin_s * sin_o
    sin_t = jnp.where(lane < dk // 2, -1.0, 1.0) * (sin_s * cos_o + cos_s * sin_o)

    def rot(x):
        return x * cos_t + pltpu.roll(x, dk // 2, 1) * sin_t

    hpb = RET_HEADS // 2
    for h in range(RET_HEADS):
        qh = rot(q_ref[:, h * dk:(h + 1) * dk])
        kh = rot(k_ref[:, h * dk:(h + 1) * dk]) * (dk ** -0.5)
        v_ref = v0_ref if h < hpb else v1_ref
        vs = slice((h % hpb) * dv, (h % hpb + 1) * dv)
        vh = v_ref[:, vs].astype(BF16)
        lg = log_g[h]
        q_dec = qh * dec_ref[h, 0]
        k_dec = kh * dec_ref[h, 1]
        if reverse:
            y = y1_ref[:, h * dv:(h + 1) * dv]
        else:
            scores = lax.dot_general(qh.astype(BF16), kh.astype(BF16), (((1,), (1,)), ((), ())),
                                     preferred_element_type=F32) * dm_ref[h]
            y = _dot(scores.astype(BF16), vh)
        st = state_ref[h]
        y = y + _dot(q_dec.astype(BF16), st.astype(BF16))
        kv = lax.dot_general(k_dec.astype(BF16), vh, (((0,), (0,)), ((), ())), preferred_element_type=F32)
        state_ref[h] = math.exp(lg * c) * st + kv
        if reverse:
            g_ref = g0_ref if h < hpb else g1_ref
            yn = y * lax.rsqrt(jnp.mean(y * y, axis=-1, keepdims=True) + EPS) * norm_ref[:, h * dv:(h + 1) * dv]
            o_ref[:, h * dv:(h + 1) * dv] = (jax.nn.silu(g_ref[:, vs]) * yn).astype(o_ref.dtype)
        else:
            o_ref[:, h * dv:(h + 1) * dv] = y


def retention_branch(proj, norm, col0, c=256):
    s = proj.shape[0]
    d_ret = norm.shape[0]
    qk = RET_HEADS * RET_KEY_DIM
    c = min(c, s)
    nc = s // c
    half = RET_KEY_DIM // 2
    inv = ROPE_BASE ** (-jnp.arange(half, dtype=F32) / half)
    inv = jnp.concatenate([inv, inv]).reshape(1, RET_KEY_DIM)
    vw = d_ret // 2
    qb, kb = col0 // qk, (col0 + qk) // qk
    vb = (col0 + 2 * qk) // vw
    gb = (col0 + 2 * qk + d_ret) // vw
    assert col0 % qk == 0 and (col0 + 2 * qk) % vw == 0
    scratch = [pltpu.VMEM((RET_HEADS, RET_KEY_DIM, d_ret // RET_HEADS), F32), pltpu.VMEM((RET_HEADS, c, c), F32),
               pltpu.VMEM((2, c, RET_KEY_DIM), F32), pltpu.VMEM((RET_HEADS, 2, c, RET_KEY_DIM), F32)]

    def specs(ti):
        return [pl.BlockSpec((c, qk), lambda i: (ti(i), qb)), pl.BlockSpec((c, qk), lambda i: (ti(i), kb)),
                pl.BlockSpec((c, vw), lambda i: (ti(i), vb)), pl.BlockSpec((c, vw), lambda i: (ti(i), vb + 1)),
                pl.BlockSpec((1, RET_KEY_DIM), lambda i: (0, 0))]

    fwd = lambda i: i
    y1 = pl.pallas_call(
        functools.partial(_ret_kernel, reverse=False, nc=nc, c=c),
        grid=(nc,),
        in_specs=specs(fwd),
        out_specs=pl.BlockSpec((c, d_ret), lambda i: (i, 0)),
        out_shape=jax.ShapeDtypeStruct((s, d_ret), F32),
        scratch_shapes=scratch,
        compiler_params=_params("arbitrary"),
        name="ret_fwd",
    )(proj, proj, proj, proj, inv)
    rev = lambda i: nc - 1 - i
    return pl.pallas_call(
        functools.partial(_ret_kernel, reverse=True, nc=nc, c=c),
        grid=(nc,),
        in_specs=specs(rev) + [pl.BlockSpec((c, d_ret), lambda i: (rev(i), 0)),
                               pl.BlockSpec((c, vw), lambda i: (rev(i), gb)),
                               pl.BlockSpec((c, vw), lambda i: (rev(i), gb + 1)),
                               pl.BlockSpec((1, d_ret), lambda i: (0, 0))],
        out_specs=pl.BlockSpec((c, d_ret), lambda i: (rev(i), 0)),
        out_shape=jax.ShapeDtypeStruct((s, d_ret), BF16),
        scratch_shapes=scratch,
        compiler_params=_params("arbitrary"),
        name="ret_bwd",
    )(proj, proj, proj, proj, inv, y1, proj, proj, norm.reshape(1, d_ret).astype(F32))


def _hy_conv_kernel(xp_ref, x_ref, xn_ref, w_ref, b_ref, o_ref, ext_ref, *, nb, tb):
    ti = pl.program_id(1)
    o_ref[...] = _dwconv_block(xp_ref, x_ref, xn_ref, w_ref, b_ref, ext_ref, ti, nb, tb, 3)


def hyena_short_conv(proj, conv_w, conv_b, col_block, c, tb=512):
    s = proj.shape[0]
    tb = min(tb, s)
    nb = s // tb
    r = tb // V7X_SUBLANES
    last = nb * r - 1
    return pl.pallas_call(
        functools.partial(_hy_conv_kernel, nb=nb, tb=tb),
        grid=(3, nb),
        in_specs=[
            pl.BlockSpec((V7X_SUBLANES, c), lambda j, i: (jnp.maximum(i * r - 1, 0), col_block + j)),
            pl.BlockSpec((tb, c), lambda j, i: (i, col_block + j)),
            pl.BlockSpec((V7X_SUBLANES, c), lambda j, i: (jnp.minimum((i + 1) * r, last), col_block + j)),
            pl.BlockSpec((3, c), lambda j, i: (0, j)),
            pl.BlockSpec((1, c), lambda j, i: (0, j)),
        ],
        out_specs=pl.BlockSpec((None, tb, c), lambda j, i: (j, i, 0)),
        out_shape=jax.ShapeDtypeStruct((3, s, c), F32),
        scratch_shapes=[pltpu.VMEM((tb + 2 * V7X_SUBLANES, c), F32)],
        compiler_params=_params("parallel", "parallel"),
        name="hyena_short_conv",
    )(proj, proj, proj, conv_w.astype(F32), conv_b.reshape(1, -1).astype(F32))


def _fft_sizes(length):
    n = 2 * length
    n2 = min(FFT_N2, n // 4)
    n1 = n // n2
    return n, n1, n2, n1 // 2, n1 // 2 + 1


def _fft_tables(length):
    n, n1, n2, n1h, k1n = _fft_sizes(length)
    eye = np.eye(V7X_SUBLANES)
    k1 = np.arange(k1n)[:, None]
    a1 = np.arange(n1h)[None, :]
    ang1 = 2.0 * np.pi * ((k1 * a1) % n1) / n1
    f1 = np.concatenate([np.kron(np.cos(ang1), eye), np.kron(-np.sin(ang1), eye)], axis=0)
    wgt = np.full((k1n,), 2.0)
    wgt[0] = wgt[-1] = 1.0
    cw = (np.cos(ang1) * wgt[:, None] / n).T
    sw = (np.sin(ang1) * wgt[:, None] / n).T
    b3 = np.concatenate([np.kron(cw, eye), np.kron(-sw, eye)], axis=1)
    a2 = np.arange(n2)
    angt = 2.0 * np.pi * (k1 * a2[None, :]) / n
    tw = jnp.asarray(np.stack([np.cos(angt), np.sin(angt)]), F32)
    tw = jnp.broadcast_to(tw[..., None], tw.shape + (V7X_LANES,))
    ang2 = 2.0 * np.pi * ((a2[:, None] * a2[None, :]) % n2) / n2
    c2, s2 = np.cos(ang2), np.sin(ang2)
    f2 = np.block([[c2, s2], [-s2, c2]])
    f2i = np.block([[c2, -s2], [s2, c2]])
    return jnp.asarray(f1, BF16), tw, jnp.asarray(f2, BF16), jnp.asarray(f2i, BF16), jnp.asarray(b3, BF16)


def _pack_pair(re, im):
    hi = lax.bitcast_convert_type(re.astype(BF16).astype(F32), jnp.uint32)
    lo = lax.bitcast_convert_type(im.astype(BF16).astype(F32), jnp.uint32)
    return hi | (lo >> 16)


def _unpack_pair(w):
    re = lax.bitcast_convert_type(w & jnp.uint32(0xFFFF0000), F32)
    im = lax.bitcast_convert_type(w << 16, F32)
    return re, im


def _twiddle(tw_ref, xr, xi, conj):
    c, s = tw_ref[0], tw_ref[1]
    if conj:
        s = -s
    out_r, out_i = [], []
    for g in range(xr.shape[-1] // V7X_LANES):
        sl = slice(g * V7X_LANES, (g + 1) * V7X_LANES)
        out_r.append(xr[..., sl] * c + xi[..., sl] * s)
        out_i.append(xi[..., sl] * c - xr[..., sl] * s)
    return jnp.concatenate(out_r, axis=-1), jnp.concatenate(out_i, axis=-1)


def _fft1_kernel(z_ref, f1_ref, tw_ref, a_ref):
    n1h, sub, cb = z_ref.shape
    k1n = a_ref.shape[0]
    z = z_ref[...].reshape(n1h * sub, cb).astype(BF16)
    a = _dot(f1_ref[...], z)
    ar = a[:k1n * sub].reshape(k1n, sub, cb)
    ai = a[k1n * sub:].reshape(k1n, sub, cb)
    a_ref[...] = _pack_pair(*_twiddle(tw_ref, ar, ai, False))


def fft_stage1(z, f1, tw, length, cb=1024):
    _, _, n2, n1h, k1n = _fft_sizes(length)
    c = z.shape[-1]
    cb = min(cb, c)
    lead = z.shape[:-2]
    z4 = z.reshape(lead + (n1h, n2, c))
    nl = len(lead)
    sub = V7X_SUBLANES
    return pl.pallas_call(
        _fft1_kernel,
        grid=(n2 // sub, c // cb),
        in_specs=[pl.BlockSpec((None,) * nl + (n1h, sub, cb), lambda j, ci: (0,) * nl + (0, j, ci)),
                  pl.BlockSpec(f1.shape, lambda j, ci: (0, 0)),
                  pl.BlockSpec((2, k1n, sub, V7X_LANES), lambda j, ci: (0, 0, j, 0))],
        out_specs=pl.BlockSpec((k1n, sub, cb), lambda j, ci: (0, j, ci)),
        out_shape=jax.ShapeDtypeStruct((k1n, n2, c), jnp.uint32),
        compiler_params=_params("parallel", "parallel"),
        name="fft_stage1",
    )(z4, f1, tw)


def _filter_fft1_kernel(fb_ref, w1_ref, b1_ref, w2_ref, b2_ref, w3_ref, freq_ref, decay_ref, f1_ref, tw_ref,
                        a_ref, hdn_ref, *, length, n2):
    k1n, sub, cb = a_ref.shape
    rows = hdn_ref.shape[0]
    j = pl.program_id(0)
    hi = lax.Precision.HIGHEST
    rho = lax.broadcasted_iota(jnp.int32, (rows, 1), 0)
    idx = ((rho // sub) * n2 + j * sub + rho % sub).astype(F32)
    t = idx / (length - 1.0)

    @pl.when(pl.program_id(1) == 0)
    def _():
        omega = (2.0 * math.pi / length) * idx
        lane = lax.broadcasted_iota(jnp.int32, (rows, V7X_LANES), 1)
        phase = fb_ref[...] * omega
        feats = jnp.where(lane == 0, t, jnp.where(lane <= HYENA_BANDS, jnp.cos(phase),
                                                  jnp.where(lane <= 2 * HYENA_BANDS, -jnp.sin(phase), 0.0)))
        freq = freq_ref[...]
        hdn = jnp.sin(freq * (jnp.dot(feats, w1_ref[...], precision=hi, preferred_element_type=F32) + b1_ref[...]))
        hdn_ref[...] = jnp.sin(freq * (jnp.dot(hdn, w2_ref[...], precision=hi, preferred_element_type=F32)
                                       + b2_ref[...]))

    filt = _dot(hdn_ref[...].astype(BF16), w3_ref[...]) * jnp.exp(-t * jnp.abs(decay_ref[...]))
    a = _dot(f1_ref[...], filt.astype(BF16))
    ar = a[:k1n * sub].reshape(k1n, sub, cb)
    ai = a[k1n * sub:].reshape(k1n, sub, cb)
    a_ref[...] = _pack_pair(*_twiddle(tw_ref, ar, ai, False))


def filter_fft_stage1(length, w1, b1, w2, b2, w3, freq, decay, f1, tw, cb=1024):
    _, _, n2, n1h, k1n = _fft_sizes(length)
    emb, hid = w1.shape
    bands = (emb - 1) // 2
    assert bands == HYENA_BANDS
    n_out = w3.shape[1]
    cb = min(cb, n_out)
    sub = V7X_SUBLANES
    f = jnp.linspace(1e-4, bands - 1, bands, dtype=F32)
    fb = jnp.zeros((1, V7X_LANES), F32).at[0, 1:1 + bands].set(f).at[0, 1 + bands:1 + 2 * bands].set(f)
    w1p = jnp.zeros((V7X_LANES, hid), F32).at[:emb].set(w1.astype(F32))
    full = lambda shape: pl.BlockSpec(shape, lambda j, ci: (0,) * len(shape))
    return pl.pallas_call(
        functools.partial(_filter_fft1_kernel, length=length, n2=n2),
        grid=(n2 // sub, n_out // cb),
        in_specs=[full((1, V7X_LANES)), full((V7X_LANES, hid)), full((1, hid)), full((hid, hid)), full((1, hid)),
                  pl.BlockSpec((hid, cb), lambda j, ci: (0, ci)), full((1, hid)),
                  pl.BlockSpec((1, cb), lambda j, ci: (0, ci)), full(f1.shape),
                  pl.BlockSpec((2, k1n, sub, V7X_LANES), lambda j, ci: (0, 0, j, 0))],
        out_specs=pl.BlockSpec((k1n, sub, cb), lambda j, ci: (0, j, ci)),
        out_shape=jax.ShapeDtypeStruct((k1n, n2, n_out), jnp.uint32),
        scratch_shapes=[pltpu.VMEM((n1h * sub, hid), F32)],
        compiler_params=_params("parallel", "arbitrary"),
        name="hyena_filter_fft1",
    )(fb, w1p, b1.reshape(1, hid).astype(F32), w2.astype(F32), b2.reshape(1, hid).astype(F32), w3.astype(BF16),
      freq.reshape(1, hid).astype(F32), decay.reshape(1, n_out).astype(F32), f1, tw)


def _slab_dft(t_ref, xr, xi):
    n2 = xr.shape[0]
    x = jnp.concatenate([xr.astype(BF16), xi.astype(BF16)], axis=0)
    y = _dot(t_ref[...], x)
    return y[:n2], y[n2:]


def _slab_specs(n2):
    tw = pl.BlockSpec((2, None, n2, V7X_LANES), lambda k, j: (0, k, 0, 0))
    f2 = pl.BlockSpec((2 * n2, 2 * n2), lambda k, j: (0, 0))
    return tw, f2


def _filter_spec_kernel(af_ref, ab_ref, f2_ref, f2i_ref, g_ref):
    ar, ai = _unpack_pair(af_ref[...])
    br, bi = _unpack_pair(ab_ref[...])
    n2 = ar.shape[0]
    top = jnp.concatenate([(ar + br).astype(BF16), (ai + bi).astype(BF16)], axis=0)
    bot = jnp.concatenate([(ai - bi).astype(BF16), (ar - br).astype(BF16)], axis=0)
    g_ref[...] = _pack_pair(_dot(f2_ref[:n2, :], top), _dot(f2i_ref[:n2, :], bot))


def filter_spectrum(a, f2, f2i, c, cb=1024):
    k1n, n2, cf = a.shape
    orders = cf // (2 * c)
    cb = min(cb, c)
    per = c // cb
    fcol = lambda k, j: (k, 0, (j // per) * 2 * per + j % per)
    bcol = lambda k, j: (k, 0, (j // per) * 2 * per + per + j % per)
    blk = (None, n2, cb)
    _, f2_spec = _slab_specs(n2)
    return pl.pallas_call(
        _filter_spec_kernel,
        grid=(k1n, orders * per),
        in_specs=[pl.BlockSpec(blk, fcol), pl.BlockSpec(blk, bcol), f2_spec, f2_spec],
        out_specs=pl.BlockSpec(blk, lambda k, j: (k, 0, j)),
        out_shape=jax.ShapeDtypeStruct((k1n, n2, orders * c), jnp.uint32),
        compiler_params=_params("parallel", "parallel"),
        name="hyena_filter_spectrum",
    )(a, a, f2, f2i)


def _slab_conv_kernel(a_ref, g_ref, tw_ref, f2_ref, f2i_ref, p_ref):
    xr, xi = _slab_dft(f2_ref, *_unpack_pair(a_ref[...]))
    gr, gi = _unpack_pair(g_ref[...])
    qr, qi = _slab_dft(f2i_ref, xr * gr - xi * gi, xr * gi + xi * gr)
    p_ref[...] = _pack_pair(*_twiddle(tw_ref, qr, qi, True))


def slab_conv(a, g, tw, f2, f2i, order, cb=1024):
    k1n, n2, c = a.shape
    cb = min(cb, c)
    per = c // cb
    blk = (None, n2, cb)
    dcol = lambda k, j: (k, 0, j)
    gcol = lambda k, j: (k, 0, order * per + j)
    tw_spec, f2_spec = _slab_specs(n2)
    return pl.pallas_call(
        _slab_conv_kernel,
        grid=(k1n, per),
        in_specs=[pl.BlockSpec(blk, dcol), pl.BlockSpec(blk, gcol), tw_spec, f2_spec, f2_spec],
        out_specs=pl.BlockSpec(blk, dcol),
        out_shape=jax.ShapeDtypeStruct((k1n, n2, c), jnp.uint32),
        compiler_params=_params("parallel", "parallel"),
        name="hyena_slab_conv",
    )(a, g, tw, f2, f2i)


def _ifft3_kernel(p_ref, b3_ref, z_ref, gate_ref, skip_ref, o_ref):
    k1n, sub, cb = p_ref.shape
    n1h = o_ref.shape[0]
    pr, pi = _unpack_pair(p_ref[...])
    p = jnp.concatenate([pr.reshape(k1n * sub, cb).astype(BF16), pi.reshape(k1n * sub, cb).astype(BF16)], axis=0)
    y = _dot(b3_ref[...], p).reshape(n1h, sub, cb)
    z = z_ref[...]
    o_ref[...] = gate_ref[...] * (y + skip_ref[...] * z)


def ifft_stage3_gate(p, b3, u, z_arr, gate_idx, skip, length, cb=1024):
    _, _, n2, n1h, k1n = _fft_sizes(length)
    c = p.shape[-1]
    cb = min(cb, c)
    sub = V7X_SUBLANES
    u4 = u.reshape(u.shape[0], n1h, n2, c)
    z4 = z_arr.reshape((-1, n1h, n2, c))
    pblk = pl.BlockSpec((k1n, sub, cb), lambda j, ci: (0, j, ci))
    out = pl.pallas_call(
        _ifft3_kernel,
        grid=(n2 // sub, c // cb),
        in_specs=[pblk, pl.BlockSpec(b3.shape, lambda j, ci: (0, 0)),
                  pl.BlockSpec((None, n1h, sub, cb), lambda j, ci: (0, 0, j, ci)),
                  pl.BlockSpec((None, n1h, sub, cb), lambda j, ci: (gate_idx, 0, j, ci)),
                  pl.BlockSpec((1, cb), lambda j, ci: (0, ci))],
        out_specs=pl.BlockSpec((n1h, sub, cb), lambda j, ci: (0, j, ci)),
        out_shape=jax.ShapeDtypeStruct((n1h, n2, c), F32),
        compiler_params=_params("parallel", "parallel"),
        name="ifft_stage3_gate",
    )(p, b3, z4, u4, skip.reshape(1, c).astype(F32))
    return out.reshape(1, length, c)


def hyena_branch(proj, tables, conv_w, conv_b, f_w1, f_b1, f_w2, f_b2, f_w3, freq, decay, skip, norm, col_block):
    length = proj.shape[0]
    c = norm.shape[0]
    f1, tw, f2, f2i, b3 = tables
    u = hyena_short_conv(proj, conv_w, conv_b, col_block, c)
    a_filt = filter_fft_stage1(length, f_w1, f_b1, f_w2, f_b2, f_w3, freq, decay, f1, tw)
    g = filter_spectrum(a_filt, f2, f2i, c)
    z = u
    for o in range(skip.shape[0]):
        p = slab_conv(fft_stage1(z, f1, tw, length), g, tw, f2, f2i, o)
        z = ifft_stage3_gate(p, b3, u, z, 1 + o, skip[o], length)
    return group_norm(z.reshape(length, c), norm, HYENA_GROUPS, BF16)


def kernel(x, norm_mix, w_in, lru_conv_w, lru_conv_b, lru_wr, lru_br, lru_wi, lru_bi, lru_lambda, lru_norm, hy_conv_w, hy_conv_b, hy_f_w1, hy_f_b1, hy_f_w2, hy_f_b2, hy_f_w3, hy_freq, hy_decay, hy_skip, hy_norm, ret_norm, w_out, norm_ffn, w_gate, w_up, w_down, norm_final):
    b, s, d = x.shape
    assert b == 1
    depth = w_in.shape[0]
    d_lru = lru_conv_w.shape[-1]
    d_hy = hy_norm.shape[-1]
    tables = _fft_tables(s)
    xs = x.reshape(s, d)
    for l in range(depth):
        h = rmsnorm(xs, norm_mix[l], BF16)
        proj = matmul_fullk([h], w_in, l)
        y_a = lru_branch(proj, lru_conv_w[l], lru_conv_b[l], lru_wr[l], lru_br[l], lru_wi[l], lru_bi[l],
                         lru_lambda[l], lru_norm[l])
        y_b = hyena_branch(proj, tables, hy_conv_w[l], hy_conv_b[l], hy_f_w1[l], hy_f_b1[l], hy_f_w2[l],
                           hy_f_b2[l], hy_f_w3[l], hy_freq[l], hy_decay[l], hy_skip[l], hy_norm[l],
                           (2 * d_lru) // d_hy)
        y_c = retention_branch(proj, ret_norm[l], 2 * d_lru + 3 * d_hy)
        xs = matmul_fullk([y_a, y_b, y_c], w_out, l, residual=xs, tm=1024, tn=512)
        h = rmsnorm(xs, norm_ffn[l], BF16)
        act = ffn_up(h, w_gate, w_up, l)
        xs = matmul_fullk([act], cast_bf16(w_down, l), residual=xs, tm=512, tn=512)
    return rmsnorm(xs, norm_final, x.dtype).reshape(b, s, d)
```

```python
import functools
import math

import numpy as np
import jax
import jax.numpy as jnp
from jax import lax
from jax.experimental import pallas as pl
from jax.experimental.pallas import tpu as pltpu

EPS = 1e-6
LRU_HEADS = 8
LRU_C = 8.0
HYENA_GROUPS = 8
HYENA_BANDS = 16
RET_HEADS = 8
RET_KEY_DIM = 128
ROPE_BASE = 10000.0

V7X_SUBLANES = 8
V7X_LANES = 128
VMEM_LIMIT_BYTES = 56 * 1024 * 1024
FFT_N2 = 256
F32 = jnp.float32
BF16 = jnp.bfloat16


def _params(*sem):
    return pltpu.CompilerParams(dimension_semantics=sem, vmem_limit_bytes=VMEM_LIMIT_BYTES)


def _dot(a, b):
    return jnp.dot(a, b, preferred_element_type=F32)


def _rmsnorm_kernel(x_ref, g_ref, o_ref):
    x = x_ref[...]
    y = x * lax.rsqrt(jnp.mean(x * x, axis=-1, keepdims=True) + EPS)
    o_ref[...] = (y * g_ref[...]).astype(o_ref.dtype)


def rmsnorm(x, gain, out_dtype, tm=256):
    s, d = x.shape
    tm = min(tm, s)
    return pl.pallas_call(
        _rmsnorm_kernel,
        grid=(s // tm,),
        in_specs=[pl.BlockSpec((tm, d), lambda i: (i, 0)), pl.BlockSpec((1, d), lambda i: (0, 0))],
        out_specs=pl.BlockSpec((tm, d), lambda i: (i, 0)),
        out_shape=jax.ShapeDtypeStruct((s, d), out_dtype),
        compiler_params=_params("parallel"),
        name="rmsnorm",
    )(x, gain.reshape(1, d).astype(F32))


def _layer_spec(layer, block, index_map):
    return pl.BlockSpec((None,) + block, lambda *g: (layer,) + index_map(*g))


def _mm_fullk_kernel(*refs, n_a, has_res):
    a_refs, b_ref, o_ref = refs[:n_a], refs[n_a], refs[-1]
    acc, off = None, 0
    for a_ref in a_refs:
        kw = a_ref.shape[1]
        part = _dot(a_ref[...].astype(BF16), b_ref[off:off + kw, :].astype(BF16))
        acc = part if acc is None else acc + part
        off += kw
    if has_res:
        acc = refs[n_a + 1][...] + acc
    o_ref[...] = acc.astype(o_ref.dtype)


def matmul_fullk(a_parts, w, layer=None, residual=None, out_dtype=F32, tm=2048, tn=256):
    m = a_parts[0].shape[0]
    kd, n = w.shape[-2:]
    assert sum(a.shape[1] for a in a_parts) == kd
    tm, tn = min(tm, m), min(tn, n)
    has_res = residual is not None
    in_specs = [pl.BlockSpec((tm, a.shape[1]), lambda i, j: (i, 0)) for a in a_parts]
    if layer is None:
        in_specs.append(pl.BlockSpec((kd, tn), lambda i, j: (0, j)))
    else:
        in_specs.append(_layer_spec(layer, (kd, tn), lambda i, j: (0, j)))
    args = list(a_parts) + [w]
    if has_res:
        in_specs.append(pl.BlockSpec((tm, tn), lambda i, j: (i, j)))
        args.append(residual)
    return pl.pallas_call(
        functools.partial(_mm_fullk_kernel, n_a=len(a_parts), has_res=has_res),
        grid=(m // tm, n // tn),
        in_specs=in_specs,
        out_specs=pl.BlockSpec((tm, tn), lambda i, j: (i, j)),
        out_shape=jax.ShapeDtypeStruct((m, n), out_dtype),
        compiler_params=_params("parallel", "parallel"),
        name="matmul_fullk_res" if has_res else "matmul_fullk",
    )(*args)


def _ffn_up_kernel(h_ref, wg_ref, wu_ref, o_ref):
    h = h_ref[...]
    g = _dot(h, wg_ref[...].astype(BF16))
    u = _dot(h, wu_ref[...].astype(BF16))
    o_ref[...] = (jax.nn.silu(g) * u).astype(o_ref.dtype)


def ffn_up(h, wg, wu, layer, tm=2048, tn=256):
    m, kd = h.shape
    n = wg.shape[2]
    tm, tn = min(tm, m), min(tn, n)
    assert n % tn == 0
    wspec = _layer_spec(layer, (kd, tn), lambda i, j: (0, j))
    return pl.pallas_call(
        _ffn_up_kernel,
        grid=(m // tm, n // tn),
        in_specs=[pl.BlockSpec((tm, kd), lambda i, j: (i, 0)), wspec, wspec],
        out_specs=pl.BlockSpec((tm, tn), lambda i, j: (i, j)),
        out_shape=jax.ShapeDtypeStruct((m, n), BF16),
        compiler_params=_params("parallel", "parallel"),
        name="ffn_up",
    )(h, wg, wu)


def _cast_kernel(w_ref, o_ref):
    o_ref[...] = w_ref[...].astype(o_ref.dtype)


def cast_bf16(w, layer, tb=256):
    _, r, n = w.shape
    tb = min(tb, r)
    assert r % tb == 0
    return pl.pallas_call(
        _cast_kernel,
        grid=(r // tb,),
        in_specs=[_layer_spec(layer, (tb, n), lambda i: (i, 0))],
        out_specs=pl.BlockSpec((tb, n), lambda i: (i, 0)),
        out_shape=jax.ShapeDtypeStruct((r, n), BF16),
        compiler_params=_params("parallel"),
        name="cast_bf16",
    )(w)


def _halo_specs(tb, width, col, nb):
    r = tb // V7X_SUBLANES
    last = nb * r - 1
    return [
        pl.BlockSpec((V7X_SUBLANES, width), lambda i: (jnp.maximum(i * r - 1, 0), col)),
        pl.BlockSpec((tb, width), lambda i: (i, col)),
        pl.BlockSpec((V7X_SUBLANES, width), lambda i: (jnp.minimum((i + 1) * r, last), col)),
    ]


def _dwconv_block(prev_ref, x_ref, next_ref, w_ref, b_ref, ext_ref, ti, nb, tb, width):
    left = width // 2
    prev = jnp.where(ti == 0, 0.0, prev_ref[...])
    nxt = jnp.where(ti == nb - 1, 0.0, next_ref[...])
    ext_ref[0:V7X_SUBLANES, :] = prev
    ext_ref[V7X_SUBLANES:V7X_SUBLANES + tb, :] = x_ref[...]
    ext_ref[V7X_SUBLANES + tb:2 * V7X_SUBLANES + tb, :] = nxt
    y = b_ref[...]
    for j in range(width):
        y = y + ext_ref[pl.ds(V7X_SUBLANES - left + j, tb), :] * w_ref[j:j + 1, :]
    return y


def _softplus(x):
    return jnp.maximum(x, 0.0) + jnp.log1p(jnp.exp(-jnp.abs(x)))


def _lru_kernel(*refs, backward, nb, tb):
    reverse = final = backward
    if final:
        xc_ref, w_ref, gb_ref, lam_ref, hf_ref, gate_ref, norm_ref, o_ref, a_ref, b_ref, carry_ref = refs
    else:
        (xp_ref, x_ref, xn_ref, cw_ref, cb_ref, w_ref, gb_ref, lam_ref,
         o_ref, xc_ref, ext_ref, a_ref, b_ref, carry_ref) = refs
    i = pl.program_id(0)
    hd = V7X_LANES
    width = o_ref.shape[-1]

    @pl.when(i == 0)
    def _():
        carry_ref[...] = jnp.zeros_like(carry_ref)

    if final:
        xc = xc_ref[...]
    else:
        xc = _dwconv_block(xp_ref, x_ref, xn_ref, cw_ref, cb_ref, ext_ref, i, nb, tb, 4)
        xc_ref[...] = xc
    sp = _softplus(-lam_ref[...])
    for h in range(LRU_HEADS):
        sl = slice(h * hd, (h + 1) * hd)
        xh = xc[:, sl]
        z = _dot(xh.astype(BF16), w_ref[h]) + gb_ref[h]
        r = jax.nn.sigmoid(z[:, :hd])
        ig = jax.nn.sigmoid(z[:, hd:])
        log_a = -LRU_C * r * sp[:, sl]
        a = jnp.exp(log_a)
        a_ref[:, sl] = a
        b_ref[:, sl] = jnp.sqrt(-jnp.tanh(log_a) * (a * a + 1.0)) * (ig * xh)

    ng = tb // V7X_SUBLANES
    row = lax.broadcasted_iota(jnp.int32, (V7X_SUBLANES, width), 0)

    def body(g, carry):
        gi = (ng - 1 - g) if reverse else g
        off = pl.multiple_of(gi * V7X_SUBLANES, V7X_SUBLANES)
        a = a_ref[pl.ds(off, V7X_SUBLANES), :]
        b = b_ref[pl.ds(off, V7X_SUBLANES), :]
        for s in (1, 2, 4):
            if reverse:
                shift, m = V7X_SUBLANES - s, row < V7X_SUBLANES - s
            else:
                shift, m = s, row >= s
            b = jnp.where(m, a * pltpu.roll(b, shift, 0) + b, b)
            a = jnp.where(m, a * pltpu.roll(a, shift, 0), a)
        hcur = a * carry + b
        b_ref[pl.ds(off, V7X_SUBLANES), :] = hcur
        return hcur[0:1, :] if reverse else hcur[V7X_SUBLANES - 1:V7X_SUBLANES, :]

    carry_ref[...] = lax.fori_loop(0, ng, body, carry_ref[...])

    if not final:
        o_ref[...] = b_ref[...]
    else:
        y = jax.nn.gelu(gate_ref[...], approximate=True) * (hf_ref[...] + b_ref[...])
        for h in range(LRU_HEADS):
            sl = slice(h * hd, (h + 1) * hd)
            yh = y[:, sl]
            yh = yh * lax.rsqrt(jnp.mean(yh * yh, axis=-1, keepdims=True) + EPS)
            o_ref[:, sl] = (yh * norm_ref[:, sl]).astype(o_ref.dtype)


def lru_branch(proj, conv_w, conv_b, wr, br, wi, bi, lam, norm, tb=512):
    s = proj.shape[0]
    c = conv_w.shape[1]
    hd = c // LRU_HEADS
    tb = min(tb, s)
    nb = s // tb
    w = jnp.concatenate([wr, wi], axis=-1).astype(BF16)
    gb = jnp.concatenate([br.reshape(2, LRU_HEADS, 1, hd), bi.reshape(2, LRU_HEADS, 1, hd)], axis=-1).astype(F32)
    full = lambda shape: pl.BlockSpec(shape, lambda i: (0,) * len(shape))
    scratch = [pltpu.VMEM((tb, c), F32), pltpu.VMEM((tb, c), F32), pltpu.VMEM((1, c), F32)]
    gates = [full((LRU_HEADS, hd, 2 * hd)), full((LRU_HEADS, 1, 2 * hd)), full((1, c))]
    fwd = pl.BlockSpec((tb, c), lambda i: (i, 0))
    h_fwd, xc = pl.pallas_call(
        functools.partial(_lru_kernel, backward=False, nb=nb, tb=tb),
        grid=(nb,),
        in_specs=_halo_specs(tb, c, 1, nb) + [full((4, c)), full((1, c))] + gates,
        out_specs=[fwd, fwd],
        out_shape=[jax.ShapeDtypeStruct((s, c), F32)] * 2,
        scratch_shapes=[pltpu.VMEM((tb + 2 * V7X_SUBLANES, c), F32)] + scratch,
        compiler_params=_params("arbitrary"),
        name="lru_fwd",
    )(proj, proj, proj, conv_w.astype(F32), conv_b.reshape(1, c).astype(F32), w[0], gb[0],
      lam[0].reshape(1, c).astype(F32))
    rev = pl.BlockSpec((tb, c), lambda i: (nb - 1 - i, 0))
    return pl.pallas_call(
        functools.partial(_lru_kernel, backward=True, nb=nb, tb=tb),
        grid=(nb,),
        in_specs=[rev] + gates + [rev, rev, full((1, c))],
        out_specs=rev,
        out_shape=jax.ShapeDtypeStruct((s, c), BF16),
        scratch_shapes=scratch,
        compiler_params=_params("arbitrary"),
        name="lru_bwd",
    )(xc, w[1], gb[1], lam[1].reshape(1, c).astype(F32), h_fwd, proj, norm.reshape(1, c).astype(F32))


def _ret_log_gamma():
    return [float(np.log1p(-np.exp2(np.float32(-5.0 - h)), dtype=np.float32)) for h in range(RET_HEADS)]


def _ret_kernel(*refs, reverse, nc, c):
    if reverse:
        (q_ref, k_ref, v0_ref, v1_ref, inv_ref, y1_ref, g0_ref, g1_ref, norm_ref,
         o_ref, state_ref, dm_ref, rope_ref, dec_ref) = refs
    else:
        q_ref, k_ref, v0_ref, v1_ref, inv_ref, o_ref, state_ref, dm_ref, rope_ref, dec_ref = refs
    i = pl.program_id(0)
    ti = (nc - 1 - i) if reverse else i
    dk = RET_KEY_DIM
    dv = v0_ref.shape[-1] * 2 // RET_HEADS
    log_g = _ret_log_gamma()
    idx = lax.broadcasted_iota(jnp.int32, (c, 1), 0).astype(F32)
    inv = inv_ref[...]

    @pl.when(i == 0)
    def _():
        state_ref[...] = jnp.zeros_like(state_ref)
        rope_ref[0] = jnp.cos(idx * inv)
        rope_ref[1] = jnp.sin(idx * inv)
        for h in range(RET_HEADS):
            q_pow, k_pow = (c - idx, idx) if reverse else (idx + 1.0, c - 1.0 - idx)
            dec_ref[h, 0] = jnp.broadcast_to(jnp.exp(log_g[h] * q_pow), (c, dk))
            dec_ref[h, 1] = jnp.broadcast_to(jnp.exp(log_g[h] * k_pow), (c, dk))
        if not reverse:
            d = jnp.abs(lax.broadcasted_iota(jnp.int32, (c, c), 0)
                        - lax.broadcasted_iota(jnp.int32, (c, c), 1)).astype(F32)
            for h in range(RET_HEADS):
                dm_ref[h] = jnp.exp(log_g[h] * d)

    start = (ti * c).astype(F32) * inv
    cos_s, sin_s = jnp.cos(start), jnp.sin(start)
    cos_o, sin_o = rope_ref[0], rope_ref[1]
    lane = lax.broadcasted_iota(jnp.int32, (c, dk), 1)
    cos_t = cos_s * cos_o - sin_s * sin_o
    sin_t = jnp.where(lane < dk // 2, -1.0, 1.0) * (sin_s * cos_o + cos_s * sin_o)

    def rot(x):
        return x * cos_t + pltpu.roll(x, dk // 2, 1) * sin_t

    hpb = RET_HEADS // 2
    for h in range(RET_HEADS):
        qh = rot(q_ref[:, h * dk:(h + 1) * dk])
        kh = rot(k_ref[:, h * dk:(h + 1) * dk]) * (dk ** -0.5)
        v_ref = v0_ref if h < hpb else v1_ref
        vs = slice((h % hpb) * dv, (h % hpb + 1) * dv)
        vh = v_ref[:, vs].astype(BF16)
        lg = log_g[h]
        q_dec = qh * dec_ref[h, 0]
        k_dec = kh * dec_ref[h, 1]
        if reverse:
            y = y1_ref[:, h * dv:(h + 1) * dv]
        else:
            scores = lax.dot_general(qh.astype(BF16), kh.astype(BF16), (((1,), (1,)), ((), ())),
                                     preferred_element_type=F32) * dm_ref[h]
            y = _dot(scores.astype(BF16), vh)
        st = state_ref[h]
        y = y + _dot(q_dec.astype(BF16), st.astype(BF16))
        kv = lax.dot_general(k_dec.astype(BF16), vh, (((0,), (0,)), ((), ())), preferred_element_type=F32)
        state_ref[h] = math.exp(lg * c) * st + kv
        if reverse:
            g_ref = g0_ref if h < hpb else g1_ref
            yn = y * lax.rsqrt(jnp.mean(y * y, axis=-1, keepdims=True) + EPS) * norm_ref[:, h * dv:(h + 1) * dv]
            o_ref[:, h * dv:(h + 1) * dv] = (jax.nn.silu(g_ref[:, vs]) * yn).astype(o_ref.dtype)
        else:
            o_ref[:, h * dv:(h + 1) * dv] = y


def retention_branch(proj, norm, col0, c=256):
    s = proj.shape[0]
    d_ret = norm.shape[0]
    qk = RET_HEADS * RET_KEY_DIM
    c = min(c, s)
    nc = s // c
    half = RET_KEY_DIM // 2
    inv = ROPE_BASE ** (-jnp.arange(half, dtype=F32) / half)
    inv = jnp.concatenate([inv, inv]).reshape(1, RET_KEY_DIM)
    vw = d_ret // 2
    qb, kb = col0 // qk, (col0 + qk) // qk
    vb = (col0 + 2 * qk) // vw
    gb = (col0 + 2 * qk + d_ret) // vw
    assert col0 % qk == 0 and (col0 + 2 * qk) % vw == 0
    scratch = [pltpu.VMEM((RET_HEADS, RET_KEY_DIM, d_ret // RET_HEADS), F32), pltpu.VMEM((RET_HEADS, c, c), F32),
               pltpu.VMEM((2, c, RET_KEY_DIM), F32), pltpu.VMEM((RET_HEADS, 2, c, RET_KEY_DIM), F32)]

    def specs(ti):
        return [pl.BlockSpec((c, qk), lambda i: (ti(i), qb)), pl.BlockSpec((c, qk), lambda i: (ti(i), kb)),
                pl.BlockSpec((c, vw), lambda i: (ti(i), vb)), pl.BlockSpec((c, vw), lambda i: (ti(i), vb + 1)),
                pl.BlockSpec((1, RET_KEY_DIM), lambda i: (0, 0))]

    fwd = lambda i: i
    y1 = pl.pallas_call(
        functools.partial(_ret_kernel, reverse=False, nc=nc, c=c),
        grid=(nc,),
        in_specs=specs(fwd),
        out_specs=pl.BlockSpec((c, d_ret), lambda i: (i, 0)),
        out_shape=jax.ShapeDtypeStruct((s, d_ret), F32),
        scratch_shapes=scratch,
        compiler_params=_params("arbitrary"),
        name="ret_fwd",
    )(proj, proj, proj, proj, inv)
    rev = lambda i: nc - 1 - i
    return pl.pallas_call(
        functools.partial(_ret_kernel, reverse=True, nc=nc, c=c),
        grid=(nc,),
        in_specs=specs(rev) + [pl.BlockSpec((c, d_ret), lambda i: (rev(i), 0)),
                               pl.BlockSpec((c, vw), lambda i: (rev(i), gb)),
                               pl.BlockSpec((c, vw), lambda i: (rev(i), gb + 1)),
                               pl.BlockSpec((1, d_ret), lambda i: (0, 0))],
        out_specs=pl.BlockSpec((c, d_ret), lambda i: (rev(i), 0)),
        out_shape=jax.ShapeDtypeStruct((s, d_ret), BF16),
        scratch_shapes=scratch,
        compiler_params=_params("arbitrary"),
        name="ret_bwd",
    )(proj, proj, proj, proj, inv, y1, proj, proj, norm.reshape(1, d_ret).astype(F32))


def _hy_conv_kernel(xp_ref, x_ref, xn_ref, w_ref, b_ref, o_ref, ext_ref, *, nb, tb):
    ti = pl.program_id(1)
    o_ref[...] = _dwconv_block(xp_ref, x_ref, xn_ref, w_ref, b_ref, ext_ref, ti, nb, tb, 3)


def hyena_short_conv(proj, conv_w, conv_b, col_block, c, tb=512):
    s = proj.shape[0]
    tb = min(tb, s)
    nb = s // tb
    r = tb // V7X_SUBLANES
    last = nb * r - 1
    return pl.pallas_call(
        functools.partial(_hy_conv_kernel, nb=nb, tb=tb),
        grid=(3, nb),
        in_specs=[
            pl.BlockSpec((V7X_SUBLANES, c), lambda j, i: (jnp.maximum(i * r - 1, 0), col_block + j)),
            pl.BlockSpec((tb, c), lambda j, i: (i, col_block + j)),
            pl.BlockSpec((V7X_SUBLANES, c), lambda j, i: (jnp.minimum((i + 1) * r, last), col_block + j)),
            pl.BlockSpec((3, c), lambda j, i: (0, j)),
            pl.BlockSpec((1, c), lambda j, i: (0, j)),
        ],
        out_specs=pl.BlockSpec((None, tb, c), lambda j, i: (j, i, 0)),
        out_shape=jax.ShapeDtypeStruct((3, s, c), F32),
        scratch_shapes=[pltpu.VMEM((tb + 2 * V7X_SUBLANES, c), F32)],
        compiler_params=_params("parallel", "parallel"),
        name="hyena_short_conv",
    )(proj, proj, proj, conv_w.astype(F32), conv_b.reshape(1, -1).astype(F32))


def _fft_sizes(length):
    n = 2 * length
    n2 = min(FFT_N2, n // 4)
    n1 = n // n2
    return n, n1, n2, n1 // 2, n1 // 2 + 1


def _fft_tables(length):
    n, n1, n2, n1h, k1n = _fft_sizes(length)
    eye = np.eye(V7X_SUBLANES)
    k1 = np.arange(k1n)[:, None]
    a1 = np.arange(n1h)[None, :]
    ang1 = 2.0 * np.pi * ((k1 * a1) % n1) / n1
    f1 = np.concatenate([np.kron(np.cos(ang1), eye), np.kron(-np.sin(ang1), eye)], axis=0)
    wgt = np.full((k1n,), 2.0)
    wgt[0] = wgt[-1] = 1.0
    cw = (np.cos(ang1) * wgt[:, None] / n).T
    sw = (np.sin(ang1) * wgt[:, None] / n).T
    b3 = np.concatenate([np.kron(cw, eye), np.kron(-sw, eye)], axis=1)
    a2 = np.arange(n2)
    angt = 2.0 * np.pi * (k1 * a2[None, :]) / n
    tw = jnp.asarray(np.stack([np.cos(angt), np.sin(angt)]), F32)
    tw = jnp.broadcast_to(tw[..., None], tw.shape + (V7X_LANES,))
    ang2 = 2.0 * np.pi * ((a2[:, None] * a2[None, :]) % n2) / n2
    c2, s2 = np.cos(ang2), np.sin(ang2)
    f2 = np.block([[c2, s2], [-s2, c2]])
    f2i = np.block([[c2, -s2], [s2, c2]])
    return jnp.asarray(f1, BF16), tw, jnp.asarray(f2, BF16), jnp.asarray(f2i, BF16), jnp.asarray(b3, BF16)


def _pack_pair(re, im):
    hi = lax.bitcast_convert_type(re.astype(BF16).astype(F32), jnp.uint32)
    lo = lax.bitcast_convert_type(im.astype(BF16).astype(F32), jnp.uint32)
    return hi | (lo >> 16)


def _unpack_pair(w):
    re = lax.bitcast_convert_type(w & jnp.uint32(0xFFFF0000), F32)
    im = lax.bitcast_convert_type(w << 16, F32)
    return re, im


def _twiddle(c, s, xr, xi):
    out_r, out_i = [], []
    for g in range(xr.shape[-1] // V7X_LANES):
        sl = slice(g * V7X_LANES, (g + 1) * V7X_LANES)
        out_r.append(xr[..., sl] * c + xi[..., sl] * s)
        out_i.append(xi[..., sl] * c - xr[..., sl] * s)
    return jnp.concatenate(out_r, axis=-1), jnp.concatenate(out_i, axis=-1)


def _stage1(z, f1_ref, tw_ref, k1n):
    sub = V7X_SUBLANES
    cb = z.shape[-1]
    a = _dot(f1_ref[...], z.astype(BF16))
    ar = a[:k1n * sub].reshape(k1n, sub, cb)
    ai = a[k1n * sub:].reshape(k1n, sub, cb)
    return _pack_pair(*_twiddle(tw_ref[0], tw_ref[1], ar, ai))


def _fft1_kernel(z_ref, f1_ref, tw_ref, a_ref):
    n1h, sub, cb = z_ref.shape
    a_ref[...] = _stage1(z_ref[...].reshape(n1h * sub, cb), f1_ref, tw_ref, a_ref.shape[0])


def fft_stage1(z, f1, tw, length, cb=1024):
    _, _, n2, n1h, k1n = _fft_sizes(length)
    c = z.shape[-1]
    cb = min(cb, c)
    lead = z.shape[:-2]
    z4 = z.reshape(lead + (n1h, n2, c))
    nl = len(lead)
    sub = V7X_SUBLANES
    return pl.pallas_call(
        _fft1_kernel,
        grid=(n2 // sub, c // cb),
        in_specs=[pl.BlockSpec((None,) * nl + (n1h, sub, cb), lambda j, ci: (0,) * nl + (0, j, ci)),
                  pl.BlockSpec(f1.shape, lambda j, ci: (0, 0)),
                  pl.BlockSpec((2, k1n, sub, V7X_LANES), lambda j, ci: (0, 0, j, 0))],
        out_specs=pl.BlockSpec((k1n, sub, cb), lambda j, ci: (0, j, ci)),
        out_shape=jax.ShapeDtypeStruct((k1n, n2, c), jnp.uint32),
        compiler_params=_params("parallel", "parallel"),
        name="fft_stage1",
    )(z4, f1, tw)


def _filter_fft1_kernel(fb_ref, w1_ref, b1_ref, w2_ref, b2_ref, w3_ref, freq_ref, decay_ref, f1_ref, tw_ref,
                        a_ref, hdn_ref, *, length, n2):
    k1n, sub, cb = a_ref.shape
    rows = hdn_ref.shape[0]
    j = pl.program_id(0)
    hi = lax.Precision.HIGHEST
    rho = lax.broadcasted_iota(jnp.int32, (rows, 1), 0)
    idx = ((rho // sub) * n2 + j * sub + rho % sub).astype(F32)
    t = idx / (length - 1.0)

    @pl.when(pl.program_id(1) == 0)
    def _():
        omega = (2.0 * math.pi / length) * idx
        lane = lax.broadcasted_iota(jnp.int32, (rows, V7X_LANES), 1)
        phase = fb_ref[...] * omega
        feats = jnp.where(lane == 0, t, jnp.where(lane <= HYENA_BANDS, jnp.cos(phase),
                                                  jnp.where(lane <= 2 * HYENA_BANDS, -jnp.sin(phase), 0.0)))
        freq = freq_ref[...]
        hdn = jnp.sin(freq * (jnp.dot(feats, w1_ref[...], precision=hi, preferred_element_type=F32) + b1_ref[...]))
        hdn_ref[...] = jnp.sin(freq * (jnp.dot(hdn, w2_ref[...], precision=hi, preferred_element_type=F32)
                                       + b2_ref[...]))

    filt = _dot(hdn_ref[...].astype(BF16), w3_ref[...]) * jnp.exp(-t * jnp.abs(decay_ref[...]))
    a_ref[...] = _stage1(filt, f1_ref, tw_ref, k1n)


def filter_fft_stage1(length, w1, b1, w2, b2, w3, freq, decay, f1, tw, cb=1024):
    _, _, n2, n1h, k1n = _fft_sizes(length)
    emb, hid = w1.shape
    bands = (emb - 1) // 2
    assert bands == HYENA_BANDS
    n_out = w3.shape[1]
    cb = min(cb, n_out)
    sub = V7X_SUBLANES
    f = jnp.linspace(1e-4, bands - 1, bands, dtype=F32)
    fb = jnp.zeros((1, V7X_LANES), F32).at[0, 1:1 + bands].set(f).at[0, 1 + bands:1 + 2 * bands].set(f)
    w1p = jnp.zeros((V7X_LANES, hid), F32).at[:emb].set(w1.astype(F32))
    full = lambda shape: pl.BlockSpec(shape, lambda j, ci: (0,) * len(shape))
    return pl.pallas_call(
        functools.partial(_filter_fft1_kernel, length=length, n2=n2),
        grid=(n2 // sub, n_out // cb),
        in_specs=[full((1, V7X_LANES)), full((V7X_LANES, hid)), full((1, hid)), full((hid, hid)), full((1, hid)),
                  pl.BlockSpec((hid, cb), lambda j, ci: (0, ci)), full((1, hid)),
                  pl.BlockSpec((1, cb), lambda j, ci: (0, ci)), full(f1.shape),
                  pl.BlockSpec((2, k1n, sub, V7X_LANES), lambda j, ci: (0, 0, j, 0))],
        out_specs=pl.BlockSpec((k1n, sub, cb), lambda j, ci: (0, j, ci)),
        out_shape=jax.ShapeDtypeStruct((k1n, n2, n_out), jnp.uint32),
        scratch_shapes=[pltpu.VMEM((n1h * sub, hid), F32)],
        compiler_params=_params("parallel", "arbitrary"),
        name="hyena_filter_fft1",
    )(fb, w1p, b1.reshape(1, hid).astype(F32), w2.astype(F32), b2.reshape(1, hid).astype(F32), w3.astype(BF16),
      freq.reshape(1, hid).astype(F32), decay.reshape(1, n_out).astype(F32), f1, tw)


def _slab_dft(t_ref, xr, xi):
    n2 = xr.shape[0]
    x = jnp.concatenate([xr.astype(BF16), xi.astype(BF16)], axis=0)
    y = _dot(t_ref[...], x)
    return y[:n2], y[n2:]


def _slabs_per_step(k1n, most=5):
    return max(d for d in range(1, most + 1) if k1n % d == 0)


def _filter_spec_kernel(af_ref, ab_ref, f2_ref, f2i_ref, g_ref):
    n2 = af_ref.shape[1]
    for k in range(af_ref.shape[0]):
        ar, ai = _unpack_pair(af_ref[k])
        br, bi = _unpack_pair(ab_ref[k])
        top = jnp.concatenate([(ar + br).astype(BF16), (ai + bi).astype(BF16)], axis=0)
        bot = jnp.concatenate([(ai - bi).astype(BF16), (ar - br).astype(BF16)], axis=0)
        g_ref[k] = _pack_pair(_dot(f2_ref[:n2, :], top), _dot(f2i_ref[:n2, :], bot))


def filter_spectrum(a, f2, f2i, c, cb=1024):
    k1n, n2, cf = a.shape
    orders = cf // (2 * c)
    cb = min(cb, c)
    per = c // cb
    kb = _slabs_per_step(k1n)
    fcol = lambda k, j: (k, 0, (j // per) * 2 * per + j % per)
    bcol = lambda k, j: (k, 0, (j // per) * 2 * per + per + j % per)
    blk = (kb, n2, cb)
    f2_spec = pl.BlockSpec((2 * n2, 2 * n2), lambda k, j: (0, 0))
    return pl.pallas_call(
        _filter_spec_kernel,
        grid=(k1n // kb, orders * per),
        in_specs=[pl.BlockSpec(blk, fcol), pl.BlockSpec(blk, bcol), f2_spec, f2_spec],
        out_specs=pl.BlockSpec(blk, lambda k, j: (k, 0, j)),
        out_shape=jax.ShapeDtypeStruct((k1n, n2, orders * c), jnp.uint32),
        compiler_params=_params("parallel", "parallel"),
        name="hyena_filter_spectrum",
    )(a, a, f2, f2i)


def _slab_conv_kernel(a_ref, g_ref, tw_ref, f2_ref, f2i_ref, p_ref):
    for k in range(a_ref.shape[0]):
        xr, xi = _slab_dft(f2_ref, *_unpack_pair(a_ref[k]))
        gr, gi = _unpack_pair(g_ref[k])
        qr, qi = _slab_dft(f2i_ref, xr * gr - xi * gi, xr * gi + xi * gr)
        p_ref[k] = _pack_pair(*_twiddle(tw_ref[0, k], -tw_ref[1, k], qr, qi))


def slab_conv(a, g, tw, f2, f2i, order, cb=1024):
    k1n, n2, c = a.shape
    cb = min(cb, c)
    per = c // cb
    kb = _slabs_per_step(k1n)
    blk = (kb, n2, cb)
    dcol = lambda k, j: (k, 0, j)
    gcol = lambda k, j: (k, 0, order * per + j)
    f2_spec = pl.BlockSpec((2 * n2, 2 * n2), lambda k, j: (0, 0))
    return pl.pallas_call(
        _slab_conv_kernel,
        grid=(k1n // kb, per),
        in_specs=[pl.BlockSpec(blk, dcol), pl.BlockSpec(blk, gcol),
                  pl.BlockSpec((2, kb, n2, V7X_LANES), lambda k, j: (0, k, 0, 0)), f2_spec, f2_spec],
        out_specs=pl.BlockSpec(blk, dcol),
        out_shape=jax.ShapeDtypeStruct((k1n, n2, c), jnp.uint32),
        compiler_params=_params("parallel", "parallel"),
        name="hyena_slab_conv",
    )(a, g, tw, f2, f2i)


def _ifft3_kernel(*refs, last, groups):
    if last:
        p_ref, b3_ref, z_ref, gate_ref, skip_ref, norm_ref, o_ref = refs
    else:
        p_ref, b3_ref, z_ref, gate_ref, skip_ref, f1_ref, tw_ref, o_ref, a_ref = refs
    k1n, sub, cb = p_ref.shape
    n1h = o_ref.shape[0]
    pr, pi = _unpack_pair(p_ref[...])
    p = jnp.concatenate([pr.reshape(k1n * sub, cb).astype(BF16), pi.reshape(k1n * sub, cb).astype(BF16)], axis=0)
    y = _dot(b3_ref[...], p).reshape(n1h, sub, cb)
    z = gate_ref[...] * (y + skip_ref[...] * z_ref[...])
    if last:
        w = cb // groups
        for h in range(groups):
            zh = z[..., h * w:(h + 1) * w]
            zh = zh * lax.rsqrt(jnp.mean(zh * zh, axis=-1, keepdims=True) + EPS)
            o_ref[:, :, h * w:(h + 1) * w] = zh * norm_ref[:, h * w:(h + 1) * w]
    else:
        o_ref[...] = z
        a_ref[...] = _stage1(z.reshape(n1h * sub, cb), f1_ref, tw_ref, k1n)


def ifft_stage3_gate(p, b3, u, z_arr, gate_idx, skip, length, f1=None, tw=None, norm=None):
    _, _, n2, n1h, k1n = _fft_sizes(length)
    c = p.shape[-1]
    last = norm is not None
    sub = V7X_SUBLANES
    u4 = u.reshape(u.shape[0], n1h, n2, c)
    z4 = z_arr.reshape((-1, n1h, n2, c))
    pblk = pl.BlockSpec((k1n, sub, c), lambda j: (0, j, 0))
    zblk = pl.BlockSpec((n1h, sub, c), lambda j: (0, j, 0))
    full = lambda shape: pl.BlockSpec(shape, lambda j: (0,) * len(shape))
    in_specs = [pblk, full(b3.shape), pl.BlockSpec((None, n1h, sub, c), lambda j: (0, 0, j, 0)),
                pl.BlockSpec((None, n1h, sub, c), lambda j: (gate_idx, 0, j, 0)), full((1, c))]
    args = [p, b3, z4, u4, skip.reshape(1, c).astype(F32)]
    z_shape = jax.ShapeDtypeStruct((n1h, n2, c), F32)
    if last:
        in_specs.append(full((1, c)))
        args.append(norm.reshape(1, c).astype(F32))
        out_specs, out_shape = zblk, z_shape
    else:
        in_specs += [full(f1.shape), pl.BlockSpec((2, k1n, sub, V7X_LANES), lambda j: (0, 0, j, 0))]
        args += [f1, tw]
        out_specs, out_shape = [zblk, pblk], [z_shape, jax.ShapeDtypeStruct((k1n, n2, c), jnp.uint32)]
    out = pl.pallas_call(
        functools.partial(_ifft3_kernel, last=last, groups=HYENA_GROUPS),
        grid=(n2 // sub,),
        in_specs=in_specs,
        out_specs=out_specs,
        out_shape=out_shape,
        compiler_params=_params("parallel"),
        name="ifft_stage3_norm" if last else "ifft_stage3_fft1",
    )(*args)
    if last:
        return out.reshape(length, c)
    return out[0].reshape(1, length, c), out[1]


def hyena_branch(proj, tables, conv_w, conv_b, f_w1, f_b1, f_w2, f_b2, f_w3, freq, decay, skip, norm, col_block):
    length = proj.shape[0]
    c = norm.shape[0]
    orders = skip.shape[0]
    f1, tw, f2, f2i, b3 = tables
    u = hyena_short_conv(proj, conv_w, conv_b, col_block, c)
    a_filt = filter_fft_stage1(length, f_w1, f_b1, f_w2, f_b2, f_w3, freq, decay, f1, tw)
    g = filter_spectrum(a_filt, f2, f2i, c)
    z, a = u, fft_stage1(u, f1, tw, length)
    for o in range(orders - 1):
        p = slab_conv(a, g, tw, f2, f2i, o)
        z, a = ifft_stage3_gate(p, b3, u, z, 1 + o, skip[o], length, f1=f1, tw=tw)
    p = slab_conv(a, g, tw, f2, f2i, orders - 1)
    return ifft_stage3_gate(p, b3, u, z, orders, skip[orders - 1], length, norm=norm)


def kernel(x, norm_mix, w_in, lru_conv_w, lru_conv_b, lru_wr, lru_br, lru_wi, lru_bi, lru_lambda, lru_norm, hy_conv_w, hy_conv_b, hy_f_w1, hy_f_b1, hy_f_w2, hy_f_b2, hy_f_w3, hy_freq, hy_decay, hy_skip, hy_norm, ret_norm, w_out, norm_ffn, w_gate, w_up, w_down, norm_final):
    b, s, d = x.shape
    assert b == 1
    depth = w_in.shape[0]
    d_lru = lru_conv_w.shape[-1]
    d_hy = hy_norm.shape[-1]
    tables = _fft_tables(s)
    xs = x.reshape(s, d)
    for l in range(depth):
        h = rmsnorm(xs, norm_mix[l], BF16)
        proj = matmul_fullk([h], w_in, l)
        y_a = lru_branch(proj, lru_conv_w[l], lru_conv_b[l], lru_wr[l], lru_br[l], lru_wi[l], lru_bi[l],
                         lru_lambda[l], lru_norm[l])
        y_b = hyena_branch(proj, tables, hy_conv_w[l], hy_conv_b[l], hy_f_w1[l], hy_f_b1[l], hy_f_w2[l],
                           hy_f_b2[l], hy_f_w3[l], hy_freq[l], hy_decay[l], hy_skip[l], hy_norm[l],
                           (2 * d_lru) // d_hy)
        y_c = retention_branch(proj, ret_norm[l], 2 * d_lru + 3 * d_hy)
        xs = matmul_fullk([y_a, y_b, y_c], w_out, l, residual=xs, tm=1024, tn=512)
        h = rmsnorm(xs, norm_ffn[l], BF16)
        act = ffn_up(h, w_gate, w_up, l)
        xs = matmul_fullk([act], cast_bf16(w_down, l), residual=xs, tm=512, tn=512)
    return rmsnorm(xs, norm_final, x.dtype).reshape(b, s, d)
```

```python
import functools
import math

import numpy as np
import jax
import jax.numpy as jnp
from jax import lax
from jax.experimental import pallas as pl
from jax.experimental.pallas import tpu as pltpu

EPS = 1e-6
LRU_HEADS = 8
LRU_C = 8.0
HYENA_GROUPS = 8
HYENA_BANDS = 16
RET_HEADS = 8
RET_KEY_DIM = 128
ROPE_BASE = 10000.0

V7X_SUBLANES = 8
V7X_LANES = 128
VMEM_LIMIT_BYTES = 56 * 1024 * 1024
FFT_N2 = 256
FFT_SLABS_PER_STEP = 5
FFT_COL_BLOCK = 1024
NORM_ROWS = 512
LRU_ROWS = 512
CONV_ROWS = 1024
RET_CHUNK = 256
CAST_ROWS = 256
MATMUL_TILES = ((2048, 256), (1024, 512), (512, 512), (256, 256))
F32 = jnp.float32
BF16 = jnp.bfloat16


def _params(*sem):
    return pltpu.CompilerParams(dimension_semantics=sem, vmem_limit_bytes=VMEM_LIMIT_BYTES)


def _dot(a, b):
    return jnp.dot(a, b, preferred_element_type=F32)


def _rmsnorm_kernel(x_ref, g_ref, o_ref):
    x = x_ref[...]
    y = x * lax.rsqrt(jnp.mean(x * x, axis=-1, keepdims=True) + EPS)
    o_ref[...] = (y * g_ref[...]).astype(o_ref.dtype)


def rmsnorm(x, gain, out_dtype, tm=NORM_ROWS):
    s, d = x.shape
    tm = min(tm, s)
    return pl.pallas_call(
        _rmsnorm_kernel,
        grid=(s // tm,),
        in_specs=[pl.BlockSpec((tm, d), lambda i: (i, 0)), pl.BlockSpec((1, d), lambda i: (0, 0))],
        out_specs=pl.BlockSpec((tm, d), lambda i: (i, 0)),
        out_shape=jax.ShapeDtypeStruct((s, d), out_dtype),
        compiler_params=_params("parallel"),
        name="rmsnorm",
    )(x, gain.reshape(1, d).astype(F32))


def _layer_spec(layer, block, index_map):
    return pl.BlockSpec((None,) + block, lambda *g: (layer,) + index_map(*g))


def _matmul_tiles(m, n, a_row_bytes, w_col_bytes, io_bytes, n_results):
    for tm, tn in MATMUL_TILES:
        tm, tn = min(tm, m), min(tn, n)
        blocks = 2 * (tm * a_row_bytes + tn * w_col_bytes + tm * tn * io_bytes)
        if m % tm == 0 and n % tn == 0 and blocks + n_results * tm * tn * 4 <= VMEM_LIMIT_BYTES:
            return tm, tn
    raise ValueError("no matmul tile fits VMEM")


def _mm_fullk_kernel(*refs, n_a, has_res):
    a_refs, b_ref, o_ref = refs[:n_a], refs[n_a], refs[-1]
    acc, off = None, 0
    for a_ref in a_refs:
        kw = a_ref.shape[1]
        part = _dot(a_ref[...].astype(BF16), b_ref[off:off + kw, :].astype(BF16))
        acc = part if acc is None else acc + part
        off += kw
    if has_res:
        acc = refs[n_a + 1][...] + acc
    o_ref[...] = acc.astype(o_ref.dtype)


def matmul_fullk(a_parts, w, layer=None, residual=None, out_dtype=F32, tiles=None):
    m = a_parts[0].shape[0]
    kd, n = w.shape[-2:]
    assert sum(a.shape[1] for a in a_parts) == kd
    has_res = residual is not None
    tm, tn = tiles or _matmul_tiles(
        m, n, sum(a.shape[1] * a.dtype.itemsize for a in a_parts), kd * w.dtype.itemsize,
        jnp.dtype(out_dtype).itemsize + (residual.dtype.itemsize if has_res else 0), min(len(a_parts), 2))
    in_specs = [pl.BlockSpec((tm, a.shape[1]), lambda i, j: (i, 0)) for a in a_parts]
    if layer is None:
        in_specs.append(pl.BlockSpec((kd, tn), lambda i, j: (0, j)))
    else:
        in_specs.append(_layer_spec(layer, (kd, tn), lambda i, j: (0, j)))
    args = list(a_parts) + [w]
    if has_res:
        in_specs.append(pl.BlockSpec((tm, tn), lambda i, j: (i, j)))
        args.append(residual)
    return pl.pallas_call(
        functools.partial(_mm_fullk_kernel, n_a=len(a_parts), has_res=has_res),
        grid=(m // tm, n // tn),
        in_specs=in_specs,
        out_specs=pl.BlockSpec((tm, tn), lambda i, j: (i, j)),
        out_shape=jax.ShapeDtypeStruct((m, n), out_dtype),
        compiler_params=_params("parallel", "parallel"),
        name="matmul_fullk_res" if has_res else "matmul_fullk",
    )(*args)


def _ffn_up_kernel(h_ref, wg_ref, wu_ref, o_ref):
    h = h_ref[...]
    g = _dot(h, wg_ref[...].astype(BF16))
    u = _dot(h, wu_ref[...].astype(BF16))
    o_ref[...] = (jax.nn.silu(g) * u).astype(o_ref.dtype)


def ffn_up(h, wg, wu, layer, tiles=None):
    m, kd = h.shape
    n = wg.shape[2]
    tm, tn = tiles or _matmul_tiles(m, n, kd * h.dtype.itemsize, kd * (wg.dtype.itemsize + wu.dtype.itemsize),
                                    jnp.dtype(BF16).itemsize, 2)
    wspec = _layer_spec(layer, (kd, tn), lambda i, j: (0, j))
    return pl.pallas_call(
        _ffn_up_kernel,
        grid=(m // tm, n // tn),
        in_specs=[pl.BlockSpec((tm, kd), lambda i, j: (i, 0)), wspec, wspec],
        out_specs=pl.BlockSpec((tm, tn), lambda i, j: (i, j)),
        out_shape=jax.ShapeDtypeStruct((m, n), BF16),
        compiler_params=_params("parallel", "parallel"),
        name="ffn_up",
    )(h, wg, wu)


def _cast_kernel(w_ref, o_ref):
    o_ref[...] = w_ref[...].astype(o_ref.dtype)


def cast_bf16(w, layer, tb=CAST_ROWS):
    _, r, n = w.shape
    tb = min(tb, r)
    assert r % tb == 0
    return pl.pallas_call(
        _cast_kernel,
        grid=(r // tb,),
        in_specs=[_layer_spec(layer, (tb, n), lambda i: (i, 0))],
        out_specs=pl.BlockSpec((tb, n), lambda i: (i, 0)),
        out_shape=jax.ShapeDtypeStruct((r, n), BF16),
        compiler_params=_params("parallel"),
        name="cast_bf16",
    )(w)


def _halo_specs(tb, width, col, nb):
    r = tb // V7X_SUBLANES
    last = nb * r - 1
    return [
        pl.BlockSpec((V7X_SUBLANES, width), lambda i: (jnp.maximum(i * r - 1, 0), col)),
        pl.BlockSpec((tb, width), lambda i: (i, col)),
        pl.BlockSpec((V7X_SUBLANES, width), lambda i: (jnp.minimum((i + 1) * r, last), col)),
    ]


def _dwconv_block(prev_ref, x_ref, next_ref, w_ref, b_ref, ext_ref, ti, nb, tb, width):
    left = width // 2
    prev = jnp.where(ti == 0, 0.0, prev_ref[...])
    nxt = jnp.where(ti == nb - 1, 0.0, next_ref[...])
    ext_ref[0:V7X_SUBLANES, :] = prev
    ext_ref[V7X_SUBLANES:V7X_SUBLANES + tb, :] = x_ref[...]
    ext_ref[V7X_SUBLANES + tb:2 * V7X_SUBLANES + tb, :] = nxt
    y = b_ref[...]
    for j in range(width):
        y = y + ext_ref[pl.ds(V7X_SUBLANES - left + j, tb), :] * w_ref[j:j + 1, :]
    return y


def _softplus(x):
    return jnp.maximum(x, 0.0) + jnp.log1p(jnp.exp(-jnp.abs(x)))


def _lru_kernel(*refs, backward, nb, tb):
    reverse = final = backward
    if final:
        xc_ref, w_ref, gb_ref, lam_ref, hf_ref, gate_ref, norm_ref, o_ref, a_ref, b_ref, carry_ref = refs
    else:
        (xp_ref, x_ref, xn_ref, cw_ref, cb_ref, w_ref, gb_ref, lam_ref,
         o_ref, xc_ref, ext_ref, a_ref, b_ref, carry_ref) = refs
    i = pl.program_id(0)
    hd = V7X_LANES
    width = o_ref.shape[-1]

    @pl.when(i == 0)
    def _():
        carry_ref[...] = jnp.zeros_like(carry_ref)

    if final:
        xc = xc_ref[...]
    else:
        xc = _dwconv_block(xp_ref, x_ref, xn_ref, cw_ref, cb_ref, ext_ref, i, nb, tb, 4)
        xc_ref[...] = xc
    sp = _softplus(-lam_ref[...])
    for h in range(LRU_HEADS):
        sl = slice(h * hd, (h + 1) * hd)
        xh = xc[:, sl]
        z = _dot(xh.astype(BF16), w_ref[h]) + gb_ref[h]
        r = jax.nn.sigmoid(z[:, :hd])
        ig = jax.nn.sigmoid(z[:, hd:])
        log_a = -LRU_C * r * sp[:, sl]
        a = jnp.exp(log_a)
        a_ref[:, sl] = a
        b_ref[:, sl] = jnp.sqrt(-jnp.tanh(log_a) * (a * a + 1.0)) * (ig * xh)

    ng = tb // V7X_SUBLANES
    row = lax.broadcasted_iota(jnp.int32, (V7X_SUBLANES, width), 0)

    def body(g, carry):
        gi = (ng - 1 - g) if reverse else g
        off = pl.multiple_of(gi * V7X_SUBLANES, V7X_SUBLANES)
        a = a_ref[pl.ds(off, V7X_SUBLANES), :]
        b = b_ref[pl.ds(off, V7X_SUBLANES), :]
        for s in (1, 2, 4):
            if reverse:
                shift, m = V7X_SUBLANES - s, row < V7X_SUBLANES - s
            else:
                shift, m = s, row >= s
            b = jnp.where(m, a * pltpu.roll(b, shift, 0) + b, b)
            a = jnp.where(m, a * pltpu.roll(a, shift, 0), a)
        hcur = a * carry + b
        b_ref[pl.ds(off, V7X_SUBLANES), :] = hcur
        return hcur[0:1, :] if reverse else hcur[V7X_SUBLANES - 1:V7X_SUBLANES, :]

    carry_ref[...] = lax.fori_loop(0, ng, body, carry_ref[...])

    if not final:
        o_ref[...] = b_ref[...]
    else:
        y = jax.nn.gelu(gate_ref[...], approximate=True) * (hf_ref[...] + b_ref[...])
        for h in range(LRU_HEADS):
            sl = slice(h * hd, (h + 1) * hd)
            yh = y[:, sl]
            yh = yh * lax.rsqrt(jnp.mean(yh * yh, axis=-1, keepdims=True) + EPS)
            o_ref[:, sl] = (yh * norm_ref[:, sl]).astype(o_ref.dtype)


def lru_branch(proj, conv_w, conv_b, wr, br, wi, bi, lam, norm, tb=LRU_ROWS):
    s = proj.shape[0]
    c = conv_w.shape[1]
    hd = c // LRU_HEADS
    tb = min(tb, s)
    nb = s // tb
    w = jnp.concatenate([wr, wi], axis=-1).astype(BF16)
    gb = jnp.concatenate([br.reshape(2, LRU_HEADS, 1, hd), bi.reshape(2, LRU_HEADS, 1, hd)], axis=-1).astype(F32)
    full = lambda shape: pl.BlockSpec(shape, lambda i: (0,) * len(shape))
    scratch = [pltpu.VMEM((tb, c), F32), pltpu.VMEM((tb, c), F32), pltpu.VMEM((1, c), F32)]
    gates = [full((LRU_HEADS, hd, 2 * hd)), full((LRU_HEADS, 1, 2 * hd)), full((1, c))]
    fwd = pl.BlockSpec((tb, c), lambda i: (i, 0))
    h_fwd, xc = pl.pallas_call(
        functools.partial(_lru_kernel, backward=False, nb=nb, tb=tb),
        grid=(nb,),
        in_specs=_halo_specs(tb, c, 1, nb) + [full((4, c)), full((1, c))] + gates,
        out_specs=[fwd, fwd],
        out_shape=[jax.ShapeDtypeStruct((s, c), F32)] * 2,
        scratch_shapes=[pltpu.VMEM((tb + 2 * V7X_SUBLANES, c), F32)] + scratch,
        compiler_params=_params("arbitrary"),
        name="lru_fwd",
    )(proj, proj, proj, conv_w.astype(F32), conv_b.reshape(1, c).astype(F32), w[0], gb[0],
      lam[0].reshape(1, c).astype(F32))
    rev = pl.BlockSpec((tb, c), lambda i: (nb - 1 - i, 0))
    return pl.pallas_call(
        functools.partial(_lru_kernel, backward=True, nb=nb, tb=tb),
        grid=(nb,),
        in_specs=[rev] + gates + [rev, rev, full((1, c))],
        out_specs=rev,
        out_shape=jax.ShapeDtypeStruct((s, c), BF16),
        scratch_shapes=scratch,
        compiler_params=_params("arbitrary"),
        name="lru_bwd",
    )(xc, w[1], gb[1], lam[1].reshape(1, c).astype(F32), h_fwd, proj, norm.reshape(1, c).astype(F32))


def _ret_log_gamma():
    return [float(np.log1p(-np.exp2(np.float32(-5.0 - h)), dtype=np.float32)) for h in range(RET_HEADS)]


def _ret_kernel(*refs, reverse, nc, c):
    if reverse:
        (q_ref, k_ref, v0_ref, v1_ref, inv_ref, y1_ref, g0_ref, g1_ref, norm_ref,
         o_ref, state_ref, dm_ref, rope_ref, dec_ref) = refs
    else:
        q_ref, k_ref, v0_ref, v1_ref, inv_ref, o_ref, state_ref, dm_ref, rope_ref, dec_ref = refs
    i = pl.program_id(0)
    ti = (nc - 1 - i) if reverse else i
    dk = RET_KEY_DIM
    dv = v0_ref.shape[-1] * 2 // RET_HEADS
    log_g = _ret_log_gamma()
    idx = lax.broadcasted_iota(jnp.int32, (c, 1), 0).astype(F32)
    inv = inv_ref[...]

    @pl.when(i == 0)
    def _():
        state_ref[...] = jnp.zeros_like(state_ref)
        rope_ref[0] = jnp.cos(idx * inv)
        rope_ref[1] = jnp.sin(idx * inv)
        for h in range(RET_HEADS):
            q_pow, k_pow = (c - idx, idx) if reverse else (idx + 1.0, c - 1.0 - idx)
            dec_ref[h, 0] = jnp.broadcast_to(jnp.exp(log_g[h] * q_pow), (c, dk))
            dec_ref[h, 1] = jnp.broadcast_to(jnp.exp(log_g[h] * k_pow), (c, dk))
        if not reverse:
            d = jnp.abs(lax.broadcasted_iota(jnp.int32, (c, c), 0)
                        - lax.broadcasted_iota(jnp.int32, (c, c), 1)).astype(F32)
            for h in range(RET_HEADS):
                dm_ref[h] = jnp.exp(log_g[h] * d)

    start = (ti * c).astype(F32) * inv
    cos_s, sin_s = jnp.cos(start), jnp.sin(start)
    cos_o, sin_o = rope_ref[0], rope_ref[1]
    lane = lax.broadcasted_iota(jnp.int32, (c, dk), 1)
    cos_t = cos_s * cos_o - sin_s * sin_o
    sin_t = jnp.where(lane < dk // 2, -1.0, 1.0) * (sin_s * cos_o + cos_s * sin_o)

    def rot(x):
        return x * cos_t + pltpu.roll(x, dk // 2, 1) * sin_t

    hpb = RET_HEADS // 2
    for h in range(RET_HEADS):
        qh = rot(q_ref[:, h * dk:(h + 1) * dk])
        kh = rot(k_ref[:, h * dk:(h + 1) * dk]) * (dk ** -0.5)
        v_ref = v0_ref if h < hpb else v1_ref
        vs = slice((h % hpb) * dv, (h % hpb + 1) * dv)
        vh = v_ref[:, vs].astype(BF16)
        lg = log_g[h]
        q_dec = qh * dec_ref[h, 0]
        k_dec = kh * dec_ref[h, 1]
        if reverse:
            y = y1_ref[:, h * dv:(h + 1) * dv]
        else:
            scores = lax.dot_general(qh.astype(BF16), kh.astype(BF16), (((1,), (1,)), ((), ())),
                                     preferred_element_type=F32) * dm_ref[h]
            y = _dot(scores.astype(BF16), vh)
        st = state_ref[h]
        y = y + _dot(q_dec.astype(BF16), st.astype(BF16))
        kv = lax.dot_general(k_dec.astype(BF16), vh, (((0,), (0,)), ((), ())), preferred_element_type=F32)
        state_ref[h] = math.exp(lg * c) * st + kv
        if reverse:
            g_ref = g0_ref if h < hpb else g1_ref
            yn = y * lax.rsqrt(jnp.mean(y * y, axis=-1, keepdims=True) + EPS) * norm_ref[:, h * dv:(h + 1) * dv]
            o_ref[:, h * dv:(h + 1) * dv] = (jax.nn.silu(g_ref[:, vs]) * yn).astype(o_ref.dtype)
        else:
            o_ref[:, h * dv:(h + 1) * dv] = y


def retention_branch(proj, norm, col0, c=RET_CHUNK):
    s = proj.shape[0]
    d_ret = norm.shape[0]
    qk = RET_HEADS * RET_KEY_DIM
    c = min(c, s)
    nc = s // c
    half = RET_KEY_DIM // 2
    inv = ROPE_BASE ** (-jnp.arange(half, dtype=F32) / half)
    inv = jnp.concatenate([inv, inv]).reshape(1, RET_KEY_DIM)
    vw = d_ret // 2
    qb, kb = col0 // qk, (col0 + qk) // qk
    vb = (col0 + 2 * qk) // vw
    gb = (col0 + 2 * qk + d_ret) // vw
    assert col0 % qk == 0 and (col0 + 2 * qk) % vw == 0
    scratch = [pltpu.VMEM((RET_HEADS, RET_KEY_DIM, d_ret // RET_HEADS), F32), pltpu.VMEM((RET_HEADS, c, c), F32),
               pltpu.VMEM((2, c, RET_KEY_DIM), F32), pltpu.VMEM((RET_HEADS, 2, c, RET_KEY_DIM), F32)]

    def specs(ti):
        return [pl.BlockSpec((c, qk), lambda i: (ti(i), qb)), pl.BlockSpec((c, qk), lambda i: (ti(i), kb)),
                pl.BlockSpec((c, vw), lambda i: (ti(i), vb)), pl.BlockSpec((c, vw), lambda i: (ti(i), vb + 1)),
                pl.BlockSpec((1, RET_KEY_DIM), lambda i: (0, 0))]

    fwd = lambda i: i
    y1 = pl.pallas_call(
        functools.partial(_ret_kernel, reverse=False, nc=nc, c=c),
        grid=(nc,),
        in_specs=specs(fwd),
        out_specs=pl.BlockSpec((c, d_ret), lambda i: (i, 0)),
        out_shape=jax.ShapeDtypeStruct((s, d_ret), F32),
        scratch_shapes=scratch,
        compiler_params=_params("arbitrary"),
        name="ret_fwd",
    )(proj, proj, proj, proj, inv)
    rev = lambda i: nc - 1 - i
    return pl.pallas_call(
        functools.partial(_ret_kernel, reverse=True, nc=nc, c=c),
        grid=(nc,),
        in_specs=specs(rev) + [pl.BlockSpec((c, d_ret), lambda i: (rev(i), 0)),
                               pl.BlockSpec((c, vw), lambda i: (rev(i), gb)),
                               pl.BlockSpec((c, vw), lambda i: (rev(i), gb + 1)),
                               pl.BlockSpec((1, d_ret), lambda i: (0, 0))],
        out_specs=pl.BlockSpec((c, d_ret), lambda i: (rev(i), 0)),
        out_shape=jax.ShapeDtypeStruct((s, d_ret), BF16),
        scratch_shapes=scratch,
        compiler_params=_params("arbitrary"),
        name="ret_bwd",
    )(proj, proj, proj, proj, inv, y1, proj, proj, norm.reshape(1, d_ret).astype(F32))


def _hy_conv_kernel(xp_ref, x_ref, xn_ref, w_ref, b_ref, o_ref, ext_ref, *, nb, tb):
    ti = pl.program_id(1)
    o_ref[...] = _dwconv_block(xp_ref, x_ref, xn_ref, w_ref, b_ref, ext_ref, ti, nb, tb, 3)


def hyena_short_conv(proj, conv_w, conv_b, col_block, c, tb=CONV_ROWS):
    s = proj.shape[0]
    tb = min(tb, s)
    nb = s // tb
    r = tb // V7X_SUBLANES
    last = nb * r - 1
    return pl.pallas_call(
        functools.partial(_hy_conv_kernel, nb=nb, tb=tb),
        grid=(3, nb),
        in_specs=[
            pl.BlockSpec((V7X_SUBLANES, c), lambda j, i: (jnp.maximum(i * r - 1, 0), col_block + j)),
            pl.BlockSpec((tb, c), lambda j, i: (i, col_block + j)),
            pl.BlockSpec((V7X_SUBLANES, c), lambda j, i: (jnp.minimum((i + 1) * r, last), col_block + j)),
            pl.BlockSpec((3, c), lambda j, i: (0, j)),
            pl.BlockSpec((1, c), lambda j, i: (0, j)),
        ],
        out_specs=pl.BlockSpec((None, tb, c), lambda j, i: (j, i, 0)),
        out_shape=jax.ShapeDtypeStruct((3, s, c), F32),
        scratch_shapes=[pltpu.VMEM((tb + 2 * V7X_SUBLANES, c), F32)],
        compiler_params=_params("parallel", "parallel"),
        name="hyena_short_conv",
    )(proj, proj, proj, conv_w.astype(F32), conv_b.reshape(1, -1).astype(F32))


def _fft_sizes(length):
    n = 2 * length
    n2 = min(FFT_N2, n // 4)
    n1 = n // n2
    return n, n1, n2, n1 // 2, n1 // 2 + 1


def _fft_tables(length):
    n, n1, n2, n1h, k1n = _fft_sizes(length)
    eye = np.eye(V7X_SUBLANES)
    k1 = np.arange(k1n)[:, None]
    a1 = np.arange(n1h)[None, :]
    ang1 = 2.0 * np.pi * ((k1 * a1) % n1) / n1
    f1 = np.concatenate([np.kron(np.cos(ang1), eye), np.kron(-np.sin(ang1), eye)], axis=0)
    wgt = np.full((k1n,), 2.0)
    wgt[0] = wgt[-1] = 1.0
    cw = (np.cos(ang1) * wgt[:, None] / n).T
    sw = (np.sin(ang1) * wgt[:, None] / n).T
    b3 = np.concatenate([np.kron(cw, eye), np.kron(-sw, eye)], axis=1)
    a2 = np.arange(n2)
    angt = 2.0 * np.pi * (k1 * a2[None, :]) / n
    tw = jnp.asarray(np.stack([np.cos(angt), np.sin(angt)]), F32)
    tw = jnp.broadcast_to(tw[..., None], tw.shape + (V7X_LANES,))
    ang2 = 2.0 * np.pi * ((a2[:, None] * a2[None, :]) % n2) / n2
    c2, s2 = np.cos(ang2), np.sin(ang2)
    f2 = np.block([[c2, s2], [-s2, c2]])
    f2i = np.block([[c2, -s2], [s2, c2]])
    return jnp.asarray(f1, BF16), tw, jnp.asarray(f2, BF16), jnp.asarray(f2i, BF16), jnp.asarray(b3, BF16)


def _pack_pair(re, im):
    hi = lax.bitcast_convert_type(re.astype(BF16).astype(F32), jnp.uint32)
    lo = lax.bitcast_convert_type(im.astype(BF16).astype(F32), jnp.uint32)
    return hi | (lo >> 16)


def _unpack_pair(w):
    re = lax.bitcast_convert_type(w & jnp.uint32(0xFFFF0000), F32)
    im = lax.bitcast_convert_type(w << 16, F32)
    return re, im


def _twiddle(c, s, xr, xi):
    out_r, out_i = [], []
    for g in range(xr.shape[-1] // V7X_LANES):
        sl = slice(g * V7X_LANES, (g + 1) * V7X_LANES)
        out_r.append(xr[..., sl] * c + xi[..., sl] * s)
        out_i.append(xi[..., sl] * c - xr[..., sl] * s)
    return jnp.concatenate(out_r, axis=-1), jnp.concatenate(out_i, axis=-1)


def _stage1(z, f1_ref, tw_ref, k1n):
    sub = V7X_SUBLANES
    cb = z.shape[-1]
    a = _dot(f1_ref[...], z.astype(BF16))
    ar = a[:k1n * sub].reshape(k1n, sub, cb)
    ai = a[k1n * sub:].reshape(k1n, sub, cb)
    return _pack_pair(*_twiddle(tw_ref[0], tw_ref[1], ar, ai))


def _fft1_kernel(z_ref, f1_ref, tw_ref, a_ref):
    n1h, sub, cb = z_ref.shape
    a_ref[...] = _stage1(z_ref[...].reshape(n1h * sub, cb), f1_ref, tw_ref, a_ref.shape[0])


def fft_stage1(z, f1, tw, length, cb=FFT_COL_BLOCK):
    _, _, n2, n1h, k1n = _fft_sizes(length)
    c = z.shape[-1]
    cb = min(cb, c)
    lead = z.shape[:-2]
    z4 = z.reshape(lead + (n1h, n2, c))
    nl = len(lead)
    sub = V7X_SUBLANES
    return pl.pallas_call(
        _fft1_kernel,
        grid=(n2 // sub, c // cb),
        in_specs=[pl.BlockSpec((None,) * nl + (n1h, sub, cb), lambda j, ci: (0,) * nl + (0, j, ci)),
                  pl.BlockSpec(f1.shape, lambda j, ci: (0, 0)),
                  pl.BlockSpec((2, k1n, sub, V7X_LANES), lambda j, ci: (0, 0, j, 0))],
        out_specs=pl.BlockSpec((k1n, sub, cb), lambda j, ci: (0, j, ci)),
        out_shape=jax.ShapeDtypeStruct((k1n, n2, c), jnp.uint32),
        compiler_params=_params("parallel", "parallel"),
        name="fft_stage1",
    )(z4, f1, tw)


def _filter_fft1_kernel(fb_ref, w1_ref, b1_ref, w2_ref, b2_ref, w3_ref, freq_ref, decay_ref, f1_ref, tw_ref,
                        a_ref, hdn_ref, *, length, n2):
    k1n, sub, cb = a_ref.shape
    rows = hdn_ref.shape[0]
    j = pl.program_id(0)
    hi = lax.Precision.HIGHEST
    rho = lax.broadcasted_iota(jnp.int32, (rows, 1), 0)
    idx = ((rho // sub) * n2 + j * sub + rho % sub).astype(F32)
    t = idx / (length - 1.0)

    @pl.when(pl.program_id(1) == 0)
    def _():
        omega = (2.0 * math.pi / length) * idx
        lane = lax.broadcasted_iota(jnp.int32, (rows, V7X_LANES), 1)
        phase = fb_ref[...] * omega
        feats = jnp.where(lane == 0, t, jnp.where(lane <= HYENA_BANDS, jnp.cos(phase),
                                                  jnp.where(lane <= 2 * HYENA_BANDS, -jnp.sin(phase), 0.0)))
        freq = freq_ref[...]
        hdn = jnp.sin(freq * (jnp.dot(feats, w1_ref[...], precision=hi, preferred_element_type=F32) + b1_ref[...]))
        hdn_ref[...] = jnp.sin(freq * (jnp.dot(hdn, w2_ref[...], precision=hi, preferred_element_type=F32)
                                       + b2_ref[...]))

    filt = _dot(hdn_ref[...].astype(BF16), w3_ref[...]) * jnp.exp(-t * jnp.abs(decay_ref[...]))
    a_ref[...] = _stage1(filt, f1_ref, tw_ref, k1n)


def filter_fft_stage1(length, w1, b1, w2, b2, w3, freq, decay, f1, tw, cb=FFT_COL_BLOCK):
    _, _, n2, n1h, k1n = _fft_sizes(length)
    emb, hid = w1.shape
    bands = (emb - 1) // 2
    assert bands == HYENA_BANDS
    n_out = w3.shape[1]
    cb = min(cb, n_out)
    sub = V7X_SUBLANES
    f = jnp.linspace(1e-4, bands - 1, bands, dtype=F32)
    fb = jnp.zeros((1, V7X_LANES), F32).at[0, 1:1 + bands].set(f).at[0, 1 + bands:1 + 2 * bands].set(f)
    w1p = jnp.zeros((V7X_LANES, hid), F32).at[:emb].set(w1.astype(F32))
    full = lambda shape: pl.BlockSpec(shape, lambda j, ci: (0,) * len(shape))
    return pl.pallas_call(
        functools.partial(_filter_fft1_kernel, length=length, n2=n2),
        grid=(n2 // sub, n_out // cb),
        in_specs=[full((1, V7X_LANES)), full((V7X_LANES, hid)), full((1, hid)), full((hid, hid)), full((1, hid)),
                  pl.BlockSpec((hid, cb), lambda j, ci: (0, ci)), full((1, hid)),
                  pl.BlockSpec((1, cb), lambda j, ci: (0, ci)), full(f1.shape),
                  pl.BlockSpec((2, k1n, sub, V7X_LANES), lambda j, ci: (0, 0, j, 0))],
        out_specs=pl.BlockSpec((k1n, sub, cb), lambda j, ci: (0, j, ci)),
        out_shape=jax.ShapeDtypeStruct((k1n, n2, n_out), jnp.uint32),
        scratch_shapes=[pltpu.VMEM((n1h * sub, hid), F32)],
        compiler_params=_params("parallel", "arbitrary"),
        name="hyena_filter_fft1",
    )(fb, w1p, b1.reshape(1, hid).astype(F32), w2.astype(F32), b2.reshape(1, hid).astype(F32), w3.astype(BF16),
      freq.reshape(1, hid).astype(F32), decay.reshape(1, n_out).astype(F32), f1, tw)


def _slab_dft(t_ref, xr, xi):
    n2 = xr.shape[0]
    x = jnp.concatenate([xr.astype(BF16), xi.astype(BF16)], axis=0)
    y = _dot(t_ref[...], x)
    return y[:n2], y[n2:]


def _slabs_per_step(k1n, most=FFT_SLABS_PER_STEP):
    return max(d for d in range(1, most + 1) if k1n % d == 0)


def _filter_spec_kernel(af_ref, ab_ref, f2_ref, f2i_ref, g_ref):
    n2 = af_ref.shape[1]
    for k in range(af_ref.shape[0]):
        ar, ai = _unpack_pair(af_ref[k])
        br, bi = _unpack_pair(ab_ref[k])
        top = jnp.concatenate([(ar + br).astype(BF16), (ai + bi).astype(BF16)], axis=0)
        bot = jnp.concatenate([(ai - bi).astype(BF16), (ar - br).astype(BF16)], axis=0)
        g_ref[k] = _pack_pair(_dot(f2_ref[:n2, :], top), _dot(f2i_ref[:n2, :], bot))


def filter_spectrum(a, f2, f2i, c, cb=FFT_COL_BLOCK):
    k1n, n2, cf = a.shape
    orders = cf // (2 * c)
    cb = min(cb, c)
    per = c // cb
    kb = _slabs_per_step(k1n)
    fcol = lambda k, j: (k, 0, (j // per) * 2 * per + j % per)
    bcol = lambda k, j: (k, 0, (j // per) * 2 * per + per + j % per)
    blk = (kb, n2, cb)
    f2_spec = pl.BlockSpec((2 * n2, 2 * n2), lambda k, j: (0, 0))
    return pl.pallas_call(
        _filter_spec_kernel,
        grid=(k1n // kb, orders * per),
        in_specs=[pl.BlockSpec(blk, fcol), pl.BlockSpec(blk, bcol), f2_spec, f2_spec],
        out_specs=pl.BlockSpec(blk, lambda k, j: (k, 0, j)),
        out_shape=jax.ShapeDtypeStruct((k1n, n2, orders * c), jnp.uint32),
        compiler_params=_params("parallel", "parallel"),
        name="hyena_filter_spectrum",
    )(a, a, f2, f2i)


def _slab_conv_kernel(a_ref, g_ref, tw_ref, f2_ref, f2i_ref, p_ref):
    for k in range(a_ref.shape[0]):
        xr, xi = _slab_dft(f2_ref, *_unpack_pair(a_ref[k]))
        gr, gi = _unpack_pair(g_ref[k])
        qr, qi = _slab_dft(f2i_ref, xr * gr - xi * gi, xr * gi + xi * gr)
        p_ref[k] = _pack_pair(*_twiddle(tw_ref[0, k], -tw_ref[1, k], qr, qi))


def slab_conv(a, g, tw, f2, f2i, order, cb=FFT_COL_BLOCK):
    k1n, n2, c = a.shape
    cb = min(cb, c)
    per = c // cb
    kb = _slabs_per_step(k1n)
    blk = (kb, n2, cb)
    dcol = lambda k, j: (k, 0, j)
    gcol = lambda k, j: (k, 0, order * per + j)
    f2_spec = pl.BlockSpec((2 * n2, 2 * n2), lambda k, j: (0, 0))
    return pl.pallas_call(
        _slab_conv_kernel,
        grid=(k1n // kb, per),
        in_specs=[pl.BlockSpec(blk, dcol), pl.BlockSpec(blk, gcol),
                  pl.BlockSpec((2, kb, n2, V7X_LANES), lambda k, j: (0, k, 0, 0)), f2_spec, f2_spec],
        out_specs=pl.BlockSpec(blk, dcol),
        out_shape=jax.ShapeDtypeStruct((k1n, n2, c), jnp.uint32),
        compiler_params=_params("parallel", "parallel"),
        name="hyena_slab_conv",
    )(a, g, tw, f2, f2i)


def _ifft3_kernel(*refs, last, groups):
    if last:
        p_ref, b3_ref, z_ref, gate_ref, skip_ref, norm_ref, o_ref = refs
    else:
        p_ref, b3_ref, z_ref, gate_ref, skip_ref, f1_ref, tw_ref, o_ref, a_ref = refs
    k1n, sub, cb = p_ref.shape
    n1h = o_ref.shape[0]
    pr, pi = _unpack_pair(p_ref[...])
    p = jnp.concatenate([pr.reshape(k1n * sub, cb).astype(BF16), pi.reshape(k1n * sub, cb).astype(BF16)], axis=0)
    y = _dot(b3_ref[...], p).reshape(n1h, sub, cb)
    z = gate_ref[...] * (y + skip_ref[...] * z_ref[...])
    if last:
        w = cb // groups
        for h in range(groups):
            zh = z[..., h * w:(h + 1) * w]
            zh = zh * lax.rsqrt(jnp.mean(zh * zh, axis=-1, keepdims=True) + EPS)
            o_ref[:, :, h * w:(h + 1) * w] = zh * norm_ref[:, h * w:(h + 1) * w]
    else:
        o_ref[...] = z
        a_ref[...] = _stage1(z.reshape(n1h * sub, cb), f1_ref, tw_ref, k1n)


def ifft_stage3_gate(p, b3, u, z_arr, gate_idx, skip, length, f1=None, tw=None, norm=None):
    _, _, n2, n1h, k1n = _fft_sizes(length)
    c = p.shape[-1]
    last = norm is not None
    sub = V7X_SUBLANES
    u4 = u.reshape(u.shape[0], n1h, n2, c)
    z4 = z_arr.reshape((-1, n1h, n2, c))
    pblk = pl.BlockSpec((k1n, sub, c), lambda j: (0, j, 0))
    zblk = pl.BlockSpec((n1h, sub, c), lambda j: (0, j, 0))
    full = lambda shape: pl.BlockSpec(shape, lambda j: (0,) * len(shape))
    in_specs = [pblk, full(b3.shape), pl.BlockSpec((None, n1h, sub, c), lambda j: (0, 0, j, 0)),
                pl.BlockSpec((None, n1h, sub, c), lambda j: (gate_idx, 0, j, 0)), full((1, c))]
    args = [p, b3, z4, u4, skip.reshape(1, c).astype(F32)]
    z_shape = jax.ShapeDtypeStruct((n1h, n2, c), F32)
    if last:
        in_specs.append(full((1, c)))
        args.append(norm.reshape(1, c).astype(F32))
        out_specs, out_shape = zblk, z_shape
    else:
        in_specs += [full(f1.shape), pl.BlockSpec((2, k1n, sub, V7X_LANES), lambda j: (0, 0, j, 0))]
        args += [f1, tw]
        out_specs, out_shape = [zblk, pblk], [z_shape, jax.ShapeDtypeStruct((k1n, n2, c), jnp.uint32)]
    out = pl.pallas_call(
        functools.partial(_ifft3_kernel, last=last, groups=HYENA_GROUPS),
        grid=(n2 // sub,),
        in_specs=in_specs,
        out_specs=out_specs,
        out_shape=out_shape,
        compiler_params=_params("parallel"),
        name="ifft_stage3_norm" if last else "ifft_stage3_fft1",
    )(*args)
    if last:
        return out.reshape(length, c)
    return out[0].reshape(1, length, c), out[1]


def hyena_branch(proj, tables, conv_w, conv_b, f_w1, f_b1, f_w2, f_b2, f_w3, freq, decay, skip, norm, col_block):
    length = proj.shape[0]
    c = norm.shape[0]
    orders = skip.shape[0]
    f1, tw, f2, f2i, b3 = tables
    u = hyena_short_conv(proj, conv_w, conv_b, col_block, c)
    a_filt = filter_fft_stage1(length, f_w1, f_b1, f_w2, f_b2, f_w3, freq, decay, f1, tw)
    g = filter_spectrum(a_filt, f2, f2i, c)
    z, a = u, fft_stage1(u, f1, tw, length)
    for o in range(orders - 1):
        p = slab_conv(a, g, tw, f2, f2i, o)
        z, a = ifft_stage3_gate(p, b3, u, z, 1 + o, skip[o], length, f1=f1, tw=tw)
    p = slab_conv(a, g, tw, f2, f2i, orders - 1)
    return ifft_stage3_gate(p, b3, u, z, orders, skip[orders - 1], length, norm=norm)


def kernel(x, norm_mix, w_in, lru_conv_w, lru_conv_b, lru_wr, lru_br, lru_wi, lru_bi, lru_lambda, lru_norm, hy_conv_w, hy_conv_b, hy_f_w1, hy_f_b1, hy_f_w2, hy_f_b2, hy_f_w3, hy_freq, hy_decay, hy_skip, hy_norm, ret_norm, w_out, norm_ffn, w_gate, w_up, w_down, norm_final):
    b, s, d = x.shape
    assert b == 1
    depth = w_in.shape[0]
    d_lru = lru_conv_w.shape[-1]
    d_hy = hy_norm.shape[-1]
    tables = _fft_tables(s)
    xs = x.reshape(s, d)
    for l in range(depth):
        h = rmsnorm(xs, norm_mix[l], BF16)
        proj = matmul_fullk([h], w_in, l)
        y_a = lru_branch(proj, lru_conv_w[l], lru_conv_b[l], lru_wr[l], lru_br[l], lru_wi[l], lru_bi[l],
                         lru_lambda[l], lru_norm[l])
        y_b = hyena_branch(proj, tables, hy_conv_w[l], hy_conv_b[l], hy_f_w1[l], hy_f_b1[l], hy_f_w2[l],
                           hy_f_b2[l], hy_f_w3[l], hy_freq[l], hy_decay[l], hy_skip[l], hy_norm[l],
                           (2 * d_lru) // d_hy)
        y_c = retention_branch(proj, ret_norm[l], 2 * d_lru + 3 * d_hy)
        xs = matmul_fullk([y_a, y_b, y_c], w_out, l, residual=xs)
        h = rmsnorm(xs, norm_ffn[l], BF16)
        act = ffn_up(h, w_gate, w_up, l)
        xs = matmul_fullk([act], cast_bf16(w_down, l), residual=xs)
    return rmsnorm(xs, norm_final, x.dtype).reshape(b, s, d)
```

```python
import functools
import math

import numpy as np
import jax
import jax.numpy as jnp
from jax import lax
from jax.experimental import pallas as pl
from jax.experimental.pallas import tpu as pltpu

EPS = 1e-6
LRU_HEADS = 8
LRU_C = 8.0
HYENA_GROUPS = 8
HYENA_BANDS = 16
RET_HEADS = 8
RET_KEY_DIM = 128
ROPE_BASE = 10000.0

V7X_SUBLANES = 8
V7X_LANES = 128
VMEM_LIMIT_BYTES = 56 * 1024 * 1024
FFT_N2 = 256
FFT_SLABS_PER_STEP = 5
FFT_COL_BLOCK = 1024
NORM_ROWS = 512
LRU_ROWS = 512
CONV_ROWS = 1024
RET_CHUNK = 256
CAST_ROWS = 256
MATMUL_TILES = ((2048, 256), (1024, 512), (512, 512), (256, 256))
IN_PROJ_TILES = (1024, 1024)
F32 = jnp.float32
BF16 = jnp.bfloat16


def _params(*sem):
    return pltpu.CompilerParams(dimension_semantics=sem, vmem_limit_bytes=VMEM_LIMIT_BYTES)


def _dot(a, b):
    return jnp.dot(a, b, preferred_element_type=F32)


def _rmsnorm_kernel(x_ref, g_ref, o_ref):
    x = x_ref[...]
    y = x * lax.rsqrt(jnp.mean(x * x, axis=-1, keepdims=True) + EPS)
    o_ref[...] = (y * g_ref[...]).astype(o_ref.dtype)


def rmsnorm(x, gain, out_dtype, tm=NORM_ROWS):
    s, d = x.shape
    tm = min(tm, s)
    return pl.pallas_call(
        _rmsnorm_kernel,
        grid=(s // tm,),
        in_specs=[pl.BlockSpec((tm, d), lambda i: (i, 0)), pl.BlockSpec((1, d), lambda i: (0, 0))],
        out_specs=pl.BlockSpec((tm, d), lambda i: (i, 0)),
        out_shape=jax.ShapeDtypeStruct((s, d), out_dtype),
        compiler_params=_params("parallel"),
        name="rmsnorm",
    )(x, gain.reshape(1, d).astype(F32))


def _layer_spec(layer, block, index_map):
    return pl.BlockSpec((None,) + block, lambda *g: (layer,) + index_map(*g))


def _matmul_tiles(m, n, a_row_bytes, w_col_bytes, io_bytes, n_results):
    for tm, tn in MATMUL_TILES:
        tm, tn = min(tm, m), min(tn, n)
        blocks = 2 * (tm * a_row_bytes + tn * w_col_bytes + tm * tn * io_bytes)
        if m % tm == 0 and n % tn == 0 and blocks + n_results * tm * tn * 4 <= VMEM_LIMIT_BYTES:
            return tm, tn
    raise ValueError("no matmul tile fits VMEM")


def _mm_fullk_kernel(*refs, n_a, has_res):
    a_refs, b_ref, o_ref = refs[:n_a], refs[n_a], refs[-1]
    acc, off = None, 0
    for a_ref in a_refs:
        kw = a_ref.shape[1]
        part = _dot(a_ref[...].astype(BF16), b_ref[off:off + kw, :].astype(BF16))
        acc = part if acc is None else acc + part
        off += kw
    if has_res:
        acc = refs[n_a + 1][...] + acc
    o_ref[...] = acc.astype(o_ref.dtype)


def matmul_fullk(a_parts, w, layer=None, residual=None, out_dtype=F32, tiles=None):
    m = a_parts[0].shape[0]
    kd, n = w.shape[-2:]
    assert sum(a.shape[1] for a in a_parts) == kd
    has_res = residual is not None
    tm, tn = tiles or _matmul_tiles(
        m, n, sum(a.shape[1] * a.dtype.itemsize for a in a_parts), kd * w.dtype.itemsize,
        jnp.dtype(out_dtype).itemsize + (residual.dtype.itemsize if has_res else 0), min(len(a_parts), 2))
    in_specs = [pl.BlockSpec((tm, a.shape[1]), lambda i, j: (i, 0)) for a in a_parts]
    if layer is None:
        in_specs.append(pl.BlockSpec((kd, tn), lambda i, j: (0, j)))
    else:
        in_specs.append(_layer_spec(layer, (kd, tn), lambda i, j: (0, j)))
    args = list(a_parts) + [w]
    if has_res:
        in_specs.append(pl.BlockSpec((tm, tn), lambda i, j: (i, j)))
        args.append(residual)
    return pl.pallas_call(
        functools.partial(_mm_fullk_kernel, n_a=len(a_parts), has_res=has_res),
        grid=(m // tm, n // tn),
        in_specs=in_specs,
        out_specs=pl.BlockSpec((tm, tn), lambda i, j: (i, j)),
        out_shape=jax.ShapeDtypeStruct((m, n), out_dtype),
        compiler_params=_params("parallel", "parallel"),
        name="matmul_fullk_res" if has_res else "matmul_fullk",
    )(*args)


def _ffn_up_kernel(h_ref, wg_ref, wu_ref, o_ref):
    h = h_ref[...]
    g = _dot(h, wg_ref[...].astype(BF16))
    u = _dot(h, wu_ref[...].astype(BF16))
    o_ref[...] = (jax.nn.silu(g) * u).astype(o_ref.dtype)


def ffn_up(h, wg, wu, layer, tiles=None):
    m, kd = h.shape
    n = wg.shape[2]
    tm, tn = tiles or _matmul_tiles(m, n, kd * h.dtype.itemsize, kd * (wg.dtype.itemsize + wu.dtype.itemsize),
                                    jnp.dtype(BF16).itemsize, 2)
    wspec = _layer_spec(layer, (kd, tn), lambda i, j: (0, j))
    return pl.pallas_call(
        _ffn_up_kernel,
        grid=(m // tm, n // tn),
        in_specs=[pl.BlockSpec((tm, kd), lambda i, j: (i, 0)), wspec, wspec],
        out_specs=pl.BlockSpec((tm, tn), lambda i, j: (i, j)),
        out_shape=jax.ShapeDtypeStruct((m, n), BF16),
        compiler_params=_params("parallel", "parallel"),
        name="ffn_up",
    )(h, wg, wu)


def _cast_kernel(w_ref, o_ref):
    o_ref[...] = w_ref[...].astype(o_ref.dtype)


def cast_bf16(w, layer, tb=CAST_ROWS):
    _, r, n = w.shape
    tb = min(tb, r)
    assert r % tb == 0
    return pl.pallas_call(
        _cast_kernel,
        grid=(r // tb,),
        in_specs=[_layer_spec(layer, (tb, n), lambda i: (i, 0))],
        out_specs=pl.BlockSpec((tb, n), lambda i: (i, 0)),
        out_shape=jax.ShapeDtypeStruct((r, n), BF16),
        compiler_params=_params("parallel"),
        name="cast_bf16",
    )(w)


def _halo_specs(tb, width, col, nb):
    r = tb // V7X_SUBLANES
    last = nb * r - 1
    return [
        pl.BlockSpec((V7X_SUBLANES, width), lambda i: (jnp.maximum(i * r - 1, 0), col)),
        pl.BlockSpec((tb, width), lambda i: (i, col)),
        pl.BlockSpec((V7X_SUBLANES, width), lambda i: (jnp.minimum((i + 1) * r, last), col)),
    ]


def _dwconv_block(prev_ref, x_ref, next_ref, w_ref, b_ref, ext_ref, ti, nb, tb, width):
    left = width // 2
    prev = jnp.where(ti == 0, 0.0, prev_ref[...])
    nxt = jnp.where(ti == nb - 1, 0.0, next_ref[...])
    ext_ref[0:V7X_SUBLANES, :] = prev
    ext_ref[V7X_SUBLANES:V7X_SUBLANES + tb, :] = x_ref[...]
    ext_ref[V7X_SUBLANES + tb:2 * V7X_SUBLANES + tb, :] = nxt
    y = b_ref[...]
    for j in range(width):
        y = y + ext_ref[pl.ds(V7X_SUBLANES - left + j, tb), :] * w_ref[j:j + 1, :]
    return y


def _softplus(x):
    return jnp.maximum(x, 0.0) + jnp.log1p(jnp.exp(-jnp.abs(x)))


def _lru_kernel(*refs, backward, nb, tb):
    reverse = final = backward
    if final:
        xc_ref, w_ref, gb_ref, lam_ref, hf_ref, gate_ref, norm_ref, o_ref, a_ref, b_ref, carry_ref = refs
    else:
        (xp_ref, x_ref, xn_ref, cw_ref, cb_ref, w_ref, gb_ref, lam_ref,
         o_ref, xc_ref, ext_ref, a_ref, b_ref, carry_ref) = refs
    i = pl.program_id(0)
    hd = V7X_LANES
    width = o_ref.shape[-1]

    @pl.when(i == 0)
    def _():
        carry_ref[...] = jnp.zeros_like(carry_ref)

    if final:
        xc = xc_ref[...]
    else:
        xc = _dwconv_block(xp_ref, x_ref, xn_ref, cw_ref, cb_ref, ext_ref, i, nb, tb, 4)
        xc_ref[...] = xc
    sp = _softplus(-lam_ref[...])
    for h in range(LRU_HEADS):
        sl = slice(h * hd, (h + 1) * hd)
        xh = xc[:, sl]
        z = _dot(xh.astype(BF16), w_ref[h]) + gb_ref[h]
        r = jax.nn.sigmoid(z[:, :hd])
        ig = jax.nn.sigmoid(z[:, hd:])
        log_a = -LRU_C * r * sp[:, sl]
        a = jnp.exp(log_a)
        a_ref[:, sl] = a
        b_ref[:, sl] = jnp.sqrt(-jnp.tanh(log_a) * (a * a + 1.0)) * (ig * xh)

    ng = tb // V7X_SUBLANES
    row = lax.broadcasted_iota(jnp.int32, (V7X_SUBLANES, width), 0)

    def body(g, carry):
        gi = (ng - 1 - g) if reverse else g
        off = pl.multiple_of(gi * V7X_SUBLANES, V7X_SUBLANES)
        a = a_ref[pl.ds(off, V7X_SUBLANES), :]
        b = b_ref[pl.ds(off, V7X_SUBLANES), :]
        for s in (1, 2, 4):
            if reverse:
                shift, m = V7X_SUBLANES - s, row < V7X_SUBLANES - s
            else:
                shift, m = s, row >= s
            b = jnp.where(m, a * pltpu.roll(b, shift, 0) + b, b)
            a = jnp.where(m, a * pltpu.roll(a, shift, 0), a)
        hcur = a * carry + b
        b_ref[pl.ds(off, V7X_SUBLANES), :] = hcur
        return hcur[0:1, :] if reverse else hcur[V7X_SUBLANES - 1:V7X_SUBLANES, :]

    carry_ref[...] = lax.fori_loop(0, ng, body, carry_ref[...])

    if not final:
        o_ref[...] = b_ref[...]
    else:
        y = jax.nn.gelu(gate_ref[...], approximate=True) * (hf_ref[...] + b_ref[...])
        for h in range(LRU_HEADS):
            sl = slice(h * hd, (h + 1) * hd)
            yh = y[:, sl]
            yh = yh * lax.rsqrt(jnp.mean(yh * yh, axis=-1, keepdims=True) + EPS)
            o_ref[:, sl] = (yh * norm_ref[:, sl]).astype(o_ref.dtype)


def lru_branch(proj, conv_w, conv_b, wr, br, wi, bi, lam, norm, tb=LRU_ROWS):
    s = proj.shape[0]
    c = conv_w.shape[1]
    hd = c // LRU_HEADS
    tb = min(tb, s)
    nb = s // tb
    w = jnp.concatenate([wr, wi], axis=-1).astype(BF16)
    gb = jnp.concatenate([br.reshape(2, LRU_HEADS, 1, hd), bi.reshape(2, LRU_HEADS, 1, hd)], axis=-1).astype(F32)
    full = lambda shape: pl.BlockSpec(shape, lambda i: (0,) * len(shape))
    scratch = [pltpu.VMEM((tb, c), F32), pltpu.VMEM((tb, c), F32), pltpu.VMEM((1, c), F32)]
    gates = [full((LRU_HEADS, hd, 2 * hd)), full((LRU_HEADS, 1, 2 * hd)), full((1, c))]
    fwd = pl.BlockSpec((tb, c), lambda i: (i, 0))
    h_fwd, xc = pl.pallas_call(
        functools.partial(_lru_kernel, backward=False, nb=nb, tb=tb),
        grid=(nb,),
        in_specs=_halo_specs(tb, c, 1, nb) + [full((4, c)), full((1, c))] + gates,
        out_specs=[fwd, fwd],
        out_shape=[jax.ShapeDtypeStruct((s, c), F32)] * 2,
        scratch_shapes=[pltpu.VMEM((tb + 2 * V7X_SUBLANES, c), F32)] + scratch,
        compiler_params=_params("arbitrary"),
        name="lru_fwd",
    )(proj, proj, proj, conv_w.astype(F32), conv_b.reshape(1, c).astype(F32), w[0], gb[0],
      lam[0].reshape(1, c).astype(F32))
    rev = pl.BlockSpec((tb, c), lambda i: (nb - 1 - i, 0))
    return pl.pallas_call(
        functools.partial(_lru_kernel, backward=True, nb=nb, tb=tb),
        grid=(nb,),
        in_specs=[rev] + gates + [rev, rev, full((1, c))],
        out_specs=rev,
        out_shape=jax.ShapeDtypeStruct((s, c), BF16),
        scratch_shapes=scratch,
        compiler_params=_params("arbitrary"),
        name="lru_bwd",
    )(xc, w[1], gb[1], lam[1].reshape(1, c).astype(F32), h_fwd, proj, norm.reshape(1, c).astype(F32))


def _ret_log_gamma():
    return [float(np.log1p(-np.exp2(np.float32(-5.0 - h)), dtype=np.float32)) for h in range(RET_HEADS)]


def _ret_kernel(*refs, reverse, nc, c):
    if reverse:
        (q_ref, k_ref, v0_ref, v1_ref, inv_ref, y1_ref, g0_ref, g1_ref, norm_ref,
         o_ref, state_ref, rope_ref, dec_ref) = refs
    else:
        q_ref, k_ref, v0_ref, v1_ref, inv_ref, o_ref, state_ref, rope_ref, dec_ref, dm_ref = refs
    i = pl.program_id(0)
    ti = (nc - 1 - i) if reverse else i
    dk = RET_KEY_DIM
    dv = v0_ref.shape[-1] * 2 // RET_HEADS
    log_g = _ret_log_gamma()
    idx = lax.broadcasted_iota(jnp.int32, (c, 1), 0).astype(F32)
    inv = inv_ref[...]

    @pl.when(i == 0)
    def _():
        state_ref[...] = jnp.zeros_like(state_ref)
        rope_ref[0] = jnp.cos(idx * inv)
        rope_ref[1] = jnp.sin(idx * inv)
        for h in range(RET_HEADS):
            q_pow, k_pow = (c - idx, idx) if reverse else (idx + 1.0, c - 1.0 - idx)
            dec_ref[h, 0] = jnp.broadcast_to(jnp.exp(log_g[h] * q_pow), (c, dk))
            dec_ref[h, 1] = jnp.broadcast_to(jnp.exp(log_g[h] * k_pow), (c, dk))
        if not reverse:
            d = jnp.abs(lax.broadcasted_iota(jnp.int32, (c, c), 0)
                        - lax.broadcasted_iota(jnp.int32, (c, c), 1)).astype(F32)
            for h in range(RET_HEADS):
                dm_ref[h] = jnp.exp(log_g[h] * d)

    start = (ti * c).astype(F32) * inv
    cos_s, sin_s = jnp.cos(start), jnp.sin(start)
    cos_o, sin_o = rope_ref[0], rope_ref[1]
    lane = lax.broadcasted_iota(jnp.int32, (c, dk), 1)
    cos_t = cos_s * cos_o - sin_s * sin_o
    sin_t = jnp.where(lane < dk // 2, -1.0, 1.0) * (sin_s * cos_o + cos_s * sin_o)

    def rot(x):
        return x * cos_t + pltpu.roll(x, dk // 2, 1) * sin_t

    hpb = RET_HEADS // 2
    for h in range(RET_HEADS):
        qh = rot(q_ref[:, h * dk:(h + 1) * dk])
        kh = rot(k_ref[:, h * dk:(h + 1) * dk]) * (dk ** -0.5)
        v_ref = v0_ref if h < hpb else v1_ref
        vs = slice((h % hpb) * dv, (h % hpb + 1) * dv)
        vh = v_ref[:, vs].astype(BF16)
        lg = log_g[h]
        q_dec = qh * dec_ref[h, 0]
        k_dec = kh * dec_ref[h, 1]
        if reverse:
            y = y1_ref[:, h * dv:(h + 1) * dv]
        else:
            scores = lax.dot_general(qh.astype(BF16), kh.astype(BF16), (((1,), (1,)), ((), ())),
                                     preferred_element_type=F32) * dm_ref[h]
            y = _dot(scores.astype(BF16), vh)
        st = state_ref[h]
        y = y + _dot(q_dec.astype(BF16), st.astype(BF16))
        kv = lax.dot_general(k_dec.astype(BF16), vh, (((0,), (0,)), ((), ())), preferred_element_type=F32)
        state_ref[h] = math.exp(lg * c) * st + kv
        if reverse:
            g_ref = g0_ref if h < hpb else g1_ref
            yn = y * lax.rsqrt(jnp.mean(y * y, axis=-1, keepdims=True) + EPS) * norm_ref[:, h * dv:(h + 1) * dv]
            o_ref[:, h * dv:(h + 1) * dv] = (jax.nn.silu(g_ref[:, vs]) * yn).astype(o_ref.dtype)
        else:
            o_ref[:, h * dv:(h + 1) * dv] = y


def retention_branch(proj, norm, col0, c=RET_CHUNK):
    s = proj.shape[0]
    d_ret = norm.shape[0]
    qk = RET_HEADS * RET_KEY_DIM
    half = RET_KEY_DIM // 2
    inv = ROPE_BASE ** (-jnp.arange(half, dtype=F32) / half)
    inv = jnp.concatenate([inv, inv]).reshape(1, RET_KEY_DIM)
    vw = d_ret // 2
    qb, kb = col0 // qk, (col0 + qk) // qk
    vb = (col0 + 2 * qk) // vw
    gb = (col0 + 2 * qk + d_ret) // vw
    assert col0 % qk == 0 and (col0 + 2 * qk) % vw == 0

    c = min(c, s)
    nc = s // c

    def one_pass(reverse, extra, out_dtype):
        ti = (lambda i: nc - 1 - i) if reverse else (lambda i: i)
        rows = lambda width, col: pl.BlockSpec((c, width), lambda i: (ti(i), col))
        full = lambda width: pl.BlockSpec((1, width), lambda i: (0, 0))
        scratch = [pltpu.VMEM((RET_HEADS, RET_KEY_DIM, d_ret // RET_HEADS), F32), pltpu.VMEM((2, c, RET_KEY_DIM), F32),
                   pltpu.VMEM((RET_HEADS, 2, c, RET_KEY_DIM), F32)]
        in_specs = [rows(qk, qb), rows(qk, kb), rows(vw, vb), rows(vw, vb + 1), full(RET_KEY_DIM)]
        args = [proj, proj, proj, proj, inv]
        if reverse:
            in_specs += [rows(d_ret, 0), rows(vw, gb), rows(vw, gb + 1), full(d_ret)]
            args += [extra, proj, proj, norm.reshape(1, d_ret).astype(F32)]
        else:
            scratch.append(pltpu.VMEM((RET_HEADS, c, c), F32))
        return pl.pallas_call(
            functools.partial(_ret_kernel, reverse=reverse, nc=nc, c=c),
            grid=(nc,),
            in_specs=in_specs,
            out_specs=rows(d_ret, 0),
            out_shape=jax.ShapeDtypeStruct((s, d_ret), out_dtype),
            scratch_shapes=scratch,
            compiler_params=_params("arbitrary"),
            name="ret_bwd" if reverse else "ret_fwd",
        )(*args)

    return one_pass(True, one_pass(False, None, F32), BF16)


def _hy_conv_kernel(xp_ref, x_ref, xn_ref, w_ref, b_ref, o_ref, ext_ref, *, nb, tb):
    ti = pl.program_id(1)
    o_ref[...] = _dwconv_block(xp_ref, x_ref, xn_ref, w_ref, b_ref, ext_ref, ti, nb, tb, 3)


def hyena_short_conv(proj, conv_w, conv_b, col_block, c, tb=CONV_ROWS):
    s = proj.shape[0]
    tb = min(tb, s)
    nb = s // tb
    r = tb // V7X_SUBLANES
    last = nb * r - 1
    return pl.pallas_call(
        functools.partial(_hy_conv_kernel, nb=nb, tb=tb),
        grid=(3, nb),
        in_specs=[
            pl.BlockSpec((V7X_SUBLANES, c), lambda j, i: (jnp.maximum(i * r - 1, 0), col_block + j)),
            pl.BlockSpec((tb, c), lambda j, i: (i, col_block + j)),
            pl.BlockSpec((V7X_SUBLANES, c), lambda j, i: (jnp.minimum((i + 1) * r, last), col_block + j)),
            pl.BlockSpec((3, c), lambda j, i: (0, j)),
            pl.BlockSpec((1, c), lambda j, i: (0, j)),
        ],
        out_specs=pl.BlockSpec((None, tb, c), lambda j, i: (j, i, 0)),
        out_shape=jax.ShapeDtypeStruct((3, s, c), F32),
        scratch_shapes=[pltpu.VMEM((tb + 2 * V7X_SUBLANES, c), F32)],
        compiler_params=_params("parallel", "parallel"),
        name="hyena_short_conv",
    )(proj, proj, proj, conv_w.astype(F32), conv_b.reshape(1, -1).astype(F32))


def _fft_sizes(length):
    n = 2 * length
    n2 = min(FFT_N2, n // 4)
    n1 = n // n2
    return n, n1, n2, n1 // 2, n1 // 2 + 1


def _fft_tables(length):
    n, n1, n2, n1h, k1n = _fft_sizes(length)
    eye = np.eye(V7X_SUBLANES)
    k1 = np.arange(k1n)[:, None]
    a1 = np.arange(n1h)[None, :]
    ang1 = 2.0 * np.pi * ((k1 * a1) % n1) / n1
    f1 = np.concatenate([np.kron(np.cos(ang1), eye), np.kron(-np.sin(ang1), eye)], axis=0)
    wgt = np.full((k1n,), 2.0)
    wgt[0] = wgt[-1] = 1.0
    cw = (np.cos(ang1) * wgt[:, None] / n).T
    sw = (np.sin(ang1) * wgt[:, None] / n).T
    b3 = np.concatenate([np.kron(cw, eye), np.kron(-sw, eye)], axis=1)
    a2 = np.arange(n2)
    angt = 2.0 * np.pi * (k1 * a2[None, :]) / n
    tw = jnp.asarray(np.stack([np.cos(angt), np.sin(angt)]), F32)
    tw = jnp.broadcast_to(tw[..., None], tw.shape + (V7X_LANES,))
    ang2 = 2.0 * np.pi * ((a2[:, None] * a2[None, :]) % n2) / n2
    c2, s2 = np.cos(ang2), np.sin(ang2)
    f2 = np.block([[c2, s2], [-s2, c2]])
    f2i = np.block([[c2, -s2], [s2, c2]])
    return jnp.asarray(f1, BF16), tw, jnp.asarray(f2, BF16), jnp.asarray(f2i, BF16), jnp.asarray(b3, BF16)


def _pack_pair(re, im):
    hi = lax.bitcast_convert_type(re.astype(BF16).astype(F32), jnp.uint32)
    lo = lax.bitcast_convert_type(im.astype(BF16).astype(F32), jnp.uint32)
    return hi | (lo >> 16)


def _unpack_pair(w):
    re = lax.bitcast_convert_type(w & jnp.uint32(0xFFFF0000), F32)
    im = lax.bitcast_convert_type(w << 16, F32)
    return re, im


def _twiddle(c, s, xr, xi):
    out_r, out_i = [], []
    for g in range(xr.shape[-1] // V7X_LANES):
        sl = slice(g * V7X_LANES, (g + 1) * V7X_LANES)
        out_r.append(xr[..., sl] * c + xi[..., sl] * s)
        out_i.append(xi[..., sl] * c - xr[..., sl] * s)
    return jnp.concatenate(out_r, axis=-1), jnp.concatenate(out_i, axis=-1)


def _stage1(z, f1_ref, tw_ref, k1n):
    sub = V7X_SUBLANES
    cb = z.shape[-1]
    a = _dot(f1_ref[...], z.astype(BF16))
    ar = a[:k1n * sub].reshape(k1n, sub, cb)
    ai = a[k1n * sub:].reshape(k1n, sub, cb)
    return _pack_pair(*_twiddle(tw_ref[0], tw_ref[1], ar, ai))


def _fft1_kernel(z_ref, f1_ref, tw_ref, a_ref):
    n1h, sub, cb = z_ref.shape
    a_ref[...] = _stage1(z_ref[...].reshape(n1h * sub, cb), f1_ref, tw_ref, a_ref.shape[0])


def fft_stage1(z, f1, tw, length, cb=FFT_COL_BLOCK):
    _, _, n2, n1h, k1n = _fft_sizes(length)
    c = z.shape[-1]
    cb = min(cb, c)
    lead = z.shape[:-2]
    z4 = z.reshape(lead + (n1h, n2, c))
    nl = len(lead)
    sub = V7X_SUBLANES
    return pl.pallas_call(
        _fft1_kernel,
        grid=(n2 // sub, c // cb),
        in_specs=[pl.BlockSpec((None,) * nl + (n1h, sub, cb), lambda j, ci: (0,) * nl + (0, j, ci)),
                  pl.BlockSpec(f1.shape, lambda j, ci: (0, 0)),
                  pl.BlockSpec((2, k1n, sub, V7X_LANES), lambda j, ci: (0, 0, j, 0))],
        out_specs=pl.BlockSpec((k1n, sub, cb), lambda j, ci: (0, j, ci)),
        out_shape=jax.ShapeDtypeStruct((k1n, n2, c), jnp.uint32),
        compiler_params=_params("parallel", "parallel"),
        name="fft_stage1",
    )(z4, f1, tw)


def _filter_fft1_kernel(fb_ref, w1_ref, b1_ref, w2_ref, b2_ref, w3_ref, freq_ref, decay_ref, f1_ref, tw_ref,
                        a_ref, hdn_ref, trig_ref, *, length, n2):
    k1n, sub, cb = a_ref.shape
    rows = hdn_ref.shape[0]
    j = pl.program_id(0)
    hi = lax.Precision.HIGHEST
    rho = lax.broadcasted_iota(jnp.int32, (rows, 1), 0)
    base = ((rho // sub) * n2).astype(F32)
    idx = base + (j * sub + rho % sub).astype(F32)
    t = idx / (length - 1.0)
    lane = lax.broadcasted_iota(jnp.int32, (rows, V7X_LANES), 1)
    w_band = (2.0 * math.pi / length) * fb_ref[...]

    @pl.when((j == 0) & (pl.program_id(1) == 0))
    def _():
        shift = jnp.where(lane <= HYENA_BANDS, 0.5 * math.pi, math.pi)
        trig_ref[0] = jnp.sin(w_band * base + shift)
        trig_ref[1] = jnp.cos(w_band * base + shift)

    @pl.when(pl.program_id(1) == 0)
    def _():
        off = (j * sub).astype(F32) + lax.broadcasted_iota(jnp.int32, (sub, 1), 0).astype(F32)
        sin_b, cos_b = jnp.sin(w_band * off), jnp.cos(w_band * off)
        trig = (trig_ref[0].reshape(rows // sub, sub, V7X_LANES) * cos_b
                + trig_ref[1].reshape(rows // sub, sub, V7X_LANES) * sin_b).reshape(rows, V7X_LANES)
        feats = jnp.where(lane == 0, t, jnp.where(lane <= 2 * HYENA_BANDS, trig, 0.0))
        freq = freq_ref[...]
        hdn = jnp.sin(freq * (jnp.dot(feats, w1_ref[...], precision=hi, preferred_element_type=F32) + b1_ref[...]))
        hdn_ref[...] = jnp.sin(freq * (jnp.dot(hdn, w2_ref[...], precision=hi, preferred_element_type=F32)
                                       + b2_ref[...]))

    filt = _dot(hdn_ref[...].astype(BF16), w3_ref[...]) * jnp.exp(-t * jnp.abs(decay_ref[...]))
    a_ref[...] = _stage1(filt, f1_ref, tw_ref, k1n)


def filter_fft_stage1(length, w1, b1, w2, b2, w3, freq, decay, f1, tw, cb=FFT_COL_BLOCK):
    _, _, n2, n1h, k1n = _fft_sizes(length)
    emb, hid = w1.shape
    bands = (emb - 1) // 2
    assert bands == HYENA_BANDS
    n_out = w3.shape[1]
    cb = min(cb, n_out)
    sub = V7X_SUBLANES
    f = jnp.linspace(1e-4, bands - 1, bands, dtype=F32)
    fb = jnp.zeros((1, V7X_LANES), F32).at[0, 1:1 + bands].set(f).at[0, 1 + bands:1 + 2 * bands].set(f)
    w1p = jnp.zeros((V7X_LANES, hid), F32).at[:emb].set(w1.astype(F32))
    full = lambda shape: pl.BlockSpec(shape, lambda j, ci: (0,) * len(shape))
    return pl.pallas_call(
        functools.partial(_filter_fft1_kernel, length=length, n2=n2),
        grid=(n2 // sub, n_out // cb),
        in_specs=[full((1, V7X_LANES)), full((V7X_LANES, hid)), full((1, hid)), full((hid, hid)), full((1, hid)),
                  pl.BlockSpec((hid, cb), lambda j, ci: (0, ci)), full((1, hid)),
                  pl.BlockSpec((1, cb), lambda j, ci: (0, ci)), full(f1.shape),
                  pl.BlockSpec((2, k1n, sub, V7X_LANES), lambda j, ci: (0, 0, j, 0))],
        out_specs=pl.BlockSpec((k1n, sub, cb), lambda j, ci: (0, j, ci)),
        out_shape=jax.ShapeDtypeStruct((k1n, n2, n_out), jnp.uint32),
        scratch_shapes=[pltpu.VMEM((n1h * sub, hid), F32), pltpu.VMEM((2, n1h * sub, V7X_LANES), F32)],
        compiler_params=_params("arbitrary", "arbitrary"),
        name="hyena_filter_fft1",
    )(fb, w1p, b1.reshape(1, hid).astype(F32), w2.astype(F32), b2.reshape(1, hid).astype(F32), w3.astype(BF16),
      freq.reshape(1, hid).astype(F32), decay.reshape(1, n_out).astype(F32), f1, tw)


def _slab_dft(t_ref, xr, xi):
    n2 = xr.shape[0]
    x = jnp.concatenate([xr.astype(BF16), xi.astype(BF16)], axis=0)
    y = _dot(t_ref[...], x)
    return y[:n2], y[n2:]


def _slabs_per_step(k1n, most=FFT_SLABS_PER_STEP):
    return max(d for d in range(1, most + 1) if k1n % d == 0)


def _filter_spec_kernel(af_ref, ab_ref, f2_ref, f2i_ref, g_ref):
    n2 = af_ref.shape[1]
    for k in range(af_ref.shape[0]):
        ar, ai = _unpack_pair(af_ref[k])
        br, bi = _unpack_pair(ab_ref[k])
        top = jnp.concatenate([(ar + br).astype(BF16), (ai + bi).astype(BF16)], axis=0)
        bot = jnp.concatenate([(ai - bi).astype(BF16), (ar - br).astype(BF16)], axis=0)
        g_ref[k] = _pack_pair(_dot(f2_ref[:n2, :], top), _dot(f2i_ref[:n2, :], bot))


def filter_spectrum(a, f2, f2i, c, cb=FFT_COL_BLOCK):
    k1n, n2, cf = a.shape
    orders = cf // (2 * c)
    cb = min(cb, c)
    per = c // cb
    kb = _slabs_per_step(k1n)
    fcol = lambda k, j: (k, 0, (j // per) * 2 * per + j % per)
    bcol = lambda k, j: (k, 0, (j // per) * 2 * per + per + j % per)
    blk = (kb, n2, cb)
    f2_spec = pl.BlockSpec((2 * n2, 2 * n2), lambda k, j: (0, 0))
    return pl.pallas_call(
        _filter_spec_kernel,
        grid=(k1n // kb, orders * per),
        in_specs=[pl.BlockSpec(blk, fcol), pl.BlockSpec(blk, bcol), f2_spec, f2_spec],
        out_specs=pl.BlockSpec(blk, lambda k, j: (k, 0, j)),
        out_shape=jax.ShapeDtypeStruct((k1n, n2, orders * c), jnp.uint32),
        compiler_params=_params("parallel", "parallel"),
        name="hyena_filter_spectrum",
    )(a, a, f2, f2i)


def _slab_conv_kernel(a_ref, g_ref, tw_ref, f2_ref, f2i_ref, p_ref):
    for k in range(a_ref.shape[0]):
        xr, xi = _slab_dft(f2_ref, *_unpack_pair(a_ref[k]))
        gr, gi = _unpack_pair(g_ref[k])
        qr, qi = _slab_dft(f2i_ref, xr * gr - xi * gi, xr * gi + xi * gr)
        p_ref[k] = _pack_pair(*_twiddle(tw_ref[0, k], -tw_ref[1, k], qr, qi))


def slab_conv(a, g, tw, f2, f2i, order, cb=FFT_COL_BLOCK):
    k1n, n2, c = a.shape
    cb = min(cb, c)
    per = c // cb
    kb = _slabs_per_step(k1n)
    blk = (kb, n2, cb)
    dcol = lambda k, j: (k, 0, j)
    gcol = lambda k, j: (k, 0, order * per + j)
    f2_spec = pl.BlockSpec((2 * n2, 2 * n2), lambda k, j: (0, 0))
    return pl.pallas_call(
        _slab_conv_kernel,
        grid=(k1n // kb, per),
        in_specs=[pl.BlockSpec(blk, dcol), pl.BlockSpec(blk, gcol),
                  pl.BlockSpec((2, kb, n2, V7X_LANES), lambda k, j: (0, k, 0, 0)), f2_spec, f2_spec],
        out_specs=pl.BlockSpec(blk, dcol),
        out_shape=jax.ShapeDtypeStruct((k1n, n2, c), jnp.uint32),
        compiler_params=_params("parallel", "parallel"),
        name="hyena_slab_conv",
    )(a, g, tw, f2, f2i)


def _ifft3_kernel(*refs, last, groups):
    if last:
        p_ref, b3_ref, z_ref, gate_ref, skip_ref, norm_ref, o_ref = refs
    else:
        p_ref, b3_ref, z_ref, gate_ref, skip_ref, f1_ref, tw_ref, o_ref, a_ref = refs
    k1n, sub, cb = p_ref.shape
    n1h = o_ref.shape[0]
    pr, pi = _unpack_pair(p_ref[...])
    p = jnp.concatenate([pr.reshape(k1n * sub, cb).astype(BF16), pi.reshape(k1n * sub, cb).astype(BF16)], axis=0)
    y = _dot(b3_ref[...], p).reshape(n1h, sub, cb)
    z = gate_ref[...] * (y + skip_ref[...] * z_ref[...])
    if last:
        w = cb // groups
        for h in range(groups):
            zh = z[..., h * w:(h + 1) * w]
            zh = zh * lax.rsqrt(jnp.mean(zh * zh, axis=-1, keepdims=True) + EPS)
            o_ref[:, :, h * w:(h + 1) * w] = zh * norm_ref[:, h * w:(h + 1) * w]
    else:
        o_ref[...] = z
        a_ref[...] = _stage1(z.reshape(n1h * sub, cb), f1_ref, tw_ref, k1n)


def ifft_stage3_gate(p, b3, u, z_arr, gate_idx, skip, length, f1=None, tw=None, norm=None):
    _, _, n2, n1h, k1n = _fft_sizes(length)
    c = p.shape[-1]
    last = norm is not None
    sub = V7X_SUBLANES
    u4 = u.reshape(u.shape[0], n1h, n2, c)
    z4 = z_arr.reshape((-1, n1h, n2, c))
    pblk = pl.BlockSpec((k1n, sub, c), lambda j: (0, j, 0))
    zblk = pl.BlockSpec((n1h, sub, c), lambda j: (0, j, 0))
    full = lambda shape: pl.BlockSpec(shape, lambda j: (0,) * len(shape))
    in_specs = [pblk, full(b3.shape), pl.BlockSpec((None, n1h, sub, c), lambda j: (0, 0, j, 0)),
                pl.BlockSpec((None, n1h, sub, c), lambda j: (gate_idx, 0, j, 0)), full((1, c))]
    args = [p, b3, z4, u4, skip.reshape(1, c).astype(F32)]
    z_shape = jax.ShapeDtypeStruct((n1h, n2, c), F32)
    if last:
        in_specs.append(full((1, c)))
        args.append(norm.reshape(1, c).astype(F32))
        out_specs, out_shape = zblk, z_shape
    else:
        in_specs += [full(f1.shape), pl.BlockSpec((2, k1n, sub, V7X_LANES), lambda j: (0, 0, j, 0))]
        args += [f1, tw]
        out_specs, out_shape = [zblk, pblk], [z_shape, jax.ShapeDtypeStruct((k1n, n2, c), jnp.uint32)]
    out = pl.pallas_call(
        functools.partial(_ifft3_kernel, last=last, groups=HYENA_GROUPS),
        grid=(n2 // sub,),
        in_specs=in_specs,
        out_specs=out_specs,
        out_shape=out_shape,
        compiler_params=_params("parallel"),
        name="ifft_stage3_norm" if last else "ifft_stage3_fft1",
    )(*args)
    if last:
        return out.reshape(length, c)
    return out[0].reshape(1, length, c), out[1]


def hyena_branch(proj, tables, conv_w, conv_b, f_w1, f_b1, f_w2, f_b2, f_w3, freq, decay, skip, norm, col_block):
    length = proj.shape[0]
    c = norm.shape[0]
    orders = skip.shape[0]
    f1, tw, f2, f2i, b3 = tables
    u = hyena_short_conv(proj, conv_w, conv_b, col_block, c)
    a_filt = filter_fft_stage1(length, f_w1, f_b1, f_w2, f_b2, f_w3, freq, decay, f1, tw)
    g = filter_spectrum(a_filt, f2, f2i, c)
    z, a = u, fft_stage1(u, f1, tw, length)
    for o in range(orders - 1):
        p = slab_conv(a, g, tw, f2, f2i, o)
        z, a = ifft_stage3_gate(p, b3, u, z, 1 + o, skip[o], length, f1=f1, tw=tw)
    p = slab_conv(a, g, tw, f2, f2i, orders - 1)
    return ifft_stage3_gate(p, b3, u, z, orders, skip[orders - 1], length, norm=norm)


def kernel(x, norm_mix, w_in, lru_conv_w, lru_conv_b, lru_wr, lru_br, lru_wi, lru_bi, lru_lambda, lru_norm, hy_conv_w, hy_conv_b, hy_f_w1, hy_f_b1, hy_f_w2, hy_f_b2, hy_f_w3, hy_freq, hy_decay, hy_skip, hy_norm, ret_norm, w_out, norm_ffn, w_gate, w_up, w_down, norm_final):
    b, s, d = x.shape
    assert b == 1
    depth = w_in.shape[0]
    d_lru = lru_conv_w.shape[-1]
    d_hy = hy_norm.shape[-1]
    tables = _fft_tables(s)
    xs = x.reshape(s, d)
    for l in range(depth):
        h = rmsnorm(xs, norm_mix[l], BF16)
        proj = matmul_fullk([h], cast_bf16(w_in, l), tiles=IN_PROJ_TILES)
        y_a = lru_branch(proj, lru_conv_w[l], lru_conv_b[l], lru_wr[l], lru_br[l], lru_wi[l], lru_bi[l],
                         lru_lambda[l], lru_norm[l])
        y_b = hyena_branch(proj, tables, hy_conv_w[l], hy_conv_b[l], hy_f_w1[l], hy_f_b1[l], hy_f_w2[l],
                           hy_f_b2[l], hy_f_w3[l], hy_freq[l], hy_decay[l], hy_skip[l], hy_norm[l],
                           (2 * d_lru) // d_hy)
        y_c = retention_branch(proj, ret_norm[l], 2 * d_lru + 3 * d_hy)
        xs = matmul_fullk([y_a, y_b, y_c], w_out, l, residual=xs)
        h = rmsnorm(xs, norm_ffn[l], BF16)
        act = ffn_up(h, w_gate, w_up, l)
        xs = matmul_fullk([act], cast_bf16(w_down, l), residual=xs)
    return rmsnorm(xs, norm_final, x.dtype).reshape(b, s, d)
```

```python
import functools
import math

import numpy as np
import jax
import jax.numpy as jnp
from jax import lax
from jax.experimental import pallas as pl
from jax.experimental.pallas import tpu as pltpu

EPS = 1e-6
LRU_HEADS = 8
LRU_C = 8.0
HYENA_GROUPS = 8
HYENA_BANDS = 16
RET_HEADS = 8
RET_KEY_DIM = 128
ROPE_BASE = 10000.0

V7X_SUBLANES = 8
V7X_LANES = 128
VMEM_LIMIT_BYTES = 56 * 1024 * 1024
FFT_N2 = 256
FFT_SLABS_PER_STEP = 5
FFT_COL_BLOCK = 1024
NORM_ROWS = 512
LRU_ROWS = 512
CONV_ROWS = 1024
RET_CHUNK = 256
CAST_ROWS = 256
MATMUL_TILES = ((2048, 256), (1024, 512), (512, 512), (256, 256))
IN_PROJ_TILES = (1024, 1024)
F32 = jnp.float32
BF16 = jnp.bfloat16


def _params(*sem):
    return pltpu.CompilerParams(dimension_semantics=sem, vmem_limit_bytes=VMEM_LIMIT_BYTES)


def _dot(a, b):
    return jnp.dot(a, b, preferred_element_type=F32)


def _rmsnorm_kernel(x_ref, g_ref, o_ref):
    x = x_ref[...]
    y = x * lax.rsqrt(jnp.mean(x * x, axis=-1, keepdims=True) + EPS)
    o_ref[...] = (y * g_ref[...]).astype(o_ref.dtype)


def rmsnorm(x, gain, out_dtype, tm=NORM_ROWS):
    s, d = x.shape
    tm = min(tm, s)
    return pl.pallas_call(
        _rmsnorm_kernel,
        grid=(s // tm,),
        in_specs=[pl.BlockSpec((tm, d), lambda i: (i, 0)), pl.BlockSpec((1, d), lambda i: (0, 0))],
        out_specs=pl.BlockSpec((tm, d), lambda i: (i, 0)),
        out_shape=jax.ShapeDtypeStruct((s, d), out_dtype),
        compiler_params=_params("parallel"),
        name="rmsnorm",
    )(x, gain.reshape(1, d).astype(F32))


def _layer_spec(layer, block, index_map):
    return pl.BlockSpec((None,) + block, lambda *g: (layer,) + index_map(*g))


def _matmul_tiles(m, n, a_row_bytes, w_col_bytes, io_bytes, n_results):
    for tm, tn in MATMUL_TILES:
        tm, tn = min(tm, m), min(tn, n)
        blocks = 2 * (tm * a_row_bytes + tn * w_col_bytes + tm * tn * io_bytes)
        if m % tm == 0 and n % tn == 0 and blocks + n_results * tm * tn * 4 <= VMEM_LIMIT_BYTES:
            return tm, tn
    raise ValueError("no matmul tile fits VMEM")


def _mm_fullk_kernel(*refs, n_a, has_res):
    a_refs, b_ref, o_ref = refs[:n_a], refs[n_a], refs[-1]
    acc, off = None, 0
    for a_ref in a_refs:
        kw = a_ref.shape[1]
        part = _dot(a_ref[...].astype(BF16), b_ref[off:off + kw, :].astype(BF16))
        acc = part if acc is None else acc + part
        off += kw
    if has_res:
        acc = refs[n_a + 1][...] + acc
    o_ref[...] = acc.astype(o_ref.dtype)


def matmul_fullk(a_parts, w, layer=None, residual=None, out_dtype=F32, tiles=None):
    m = a_parts[0].shape[0]
    kd, n = w.shape[-2:]
    assert sum(a.shape[1] for a in a_parts) == kd
    has_res = residual is not None
    tm, tn = tiles or _matmul_tiles(
        m, n, sum(a.shape[1] * a.dtype.itemsize for a in a_parts), kd * w.dtype.itemsize,
        jnp.dtype(out_dtype).itemsize + (residual.dtype.itemsize if has_res else 0), min(len(a_parts), 2))
    in_specs = [pl.BlockSpec((tm, a.shape[1]), lambda i, j: (i, 0)) for a in a_parts]
    if layer is None:
        in_specs.append(pl.BlockSpec((kd, tn), lambda i, j: (0, j)))
    else:
        in_specs.append(_layer_spec(layer, (kd, tn), lambda i, j: (0, j)))
    args = list(a_parts) + [w]
    if has_res:
        in_specs.append(pl.BlockSpec((tm, tn), lambda i, j: (i, j)))
        args.append(residual)
    return pl.pallas_call(
        functools.partial(_mm_fullk_kernel, n_a=len(a_parts), has_res=has_res),
        grid=(m // tm, n // tn),
        in_specs=in_specs,
        out_specs=pl.BlockSpec((tm, tn), lambda i, j: (i, j)),
        out_shape=jax.ShapeDtypeStruct((m, n), out_dtype),
        compiler_params=_params("parallel", "parallel"),
        name="matmul_fullk_res" if has_res else "matmul_fullk",
    )(*args)


def _ffn_up_kernel(h_ref, wg_ref, wu_ref, o_ref):
    h = h_ref[...]
    g = _dot(h, wg_ref[...].astype(BF16))
    u = _dot(h, wu_ref[...].astype(BF16))
    o_ref[...] = (jax.nn.silu(g) * u).astype(o_ref.dtype)


def ffn_up(h, wg, wu, layer, tiles=None):
    m, kd = h.shape
    n = wg.shape[2]
    tm, tn = tiles or _matmul_tiles(m, n, kd * h.dtype.itemsize, kd * (wg.dtype.itemsize + wu.dtype.itemsize),
                                    jnp.dtype(BF16).itemsize, 2)
    wspec = _layer_spec(layer, (kd, tn), lambda i, j: (0, j))
    return pl.pallas_call(
        _ffn_up_kernel,
        grid=(m // tm, n // tn),
        in_specs=[pl.BlockSpec((tm, kd), lambda i, j: (i, 0)), wspec, wspec],
        out_specs=pl.BlockSpec((tm, tn), lambda i, j: (i, j)),
        out_shape=jax.ShapeDtypeStruct((m, n), BF16),
        compiler_params=_params("parallel", "parallel"),
        name="ffn_up",
    )(h, wg, wu)


def _cast_kernel(w_ref, o_ref):
    o_ref[...] = w_ref[...].astype(o_ref.dtype)


def cast_bf16(w, layer, tb=CAST_ROWS):
    _, r, n = w.shape
    tb = min(tb, r)
    assert r % tb == 0
    return pl.pallas_call(
        _cast_kernel,
        grid=(r // tb,),
        in_specs=[_layer_spec(layer, (tb, n), lambda i: (i, 0))],
        out_specs=pl.BlockSpec((tb, n), lambda i: (i, 0)),
        out_shape=jax.ShapeDtypeStruct((r, n), BF16),
        compiler_params=_params("parallel"),
        name="cast_bf16",
    )(w)


def _halo_specs(tb, width, col, nb):
    r = tb // V7X_SUBLANES
    last = nb * r - 1
    return [
        pl.BlockSpec((V7X_SUBLANES, width), lambda i: (jnp.maximum(i * r - 1, 0), col)),
        pl.BlockSpec((tb, width), lambda i: (i, col)),
        pl.BlockSpec((V7X_SUBLANES, width), lambda i: (jnp.minimum((i + 1) * r, last), col)),
    ]


def _dwconv_block(prev_ref, x_ref, next_ref, w_ref, b_ref, ti, nb, tb, width):
    left = width // 2
    prev = jnp.where(ti == 0, 0.0, prev_ref[...])
    nxt = jnp.where(ti == nb - 1, 0.0, next_ref[...])
    ext = jnp.concatenate([prev, x_ref[...], nxt], axis=0)
    n = ext.shape[0]
    y = b_ref[...]
    for j in range(width):
        shift = (left - j) % n
        rolled = ext if shift == 0 else pltpu.roll(ext, shift, 0)
        y = y + rolled[V7X_SUBLANES:V7X_SUBLANES + tb] * w_ref[j:j + 1, :]
    return y


def _softplus(x):
    return jnp.maximum(x, 0.0) + jnp.log1p(jnp.exp(-jnp.abs(x)))


def _lru_kernel(*refs, backward, nb, tb):
    reverse = final = backward
    if final:
        xc_ref, w_ref, gb_ref, lam_ref, hf_ref, gate_ref, norm_ref, o_ref, a_ref, b_ref, carry_ref = refs
    else:
        (xp_ref, x_ref, xn_ref, cw_ref, cb_ref, w_ref, gb_ref, lam_ref,
         o_ref, xc_ref, a_ref, b_ref, carry_ref) = refs
    i = pl.program_id(0)
    hd = V7X_LANES
    width = o_ref.shape[-1]

    @pl.when(i == 0)
    def _():
        carry_ref[...] = jnp.zeros_like(carry_ref)

    if final:
        xc = xc_ref[...]
    else:
        xc = _dwconv_block(xp_ref, x_ref, xn_ref, cw_ref, cb_ref, i, nb, tb, 4)
        xc_ref[...] = xc
    sp = _softplus(-lam_ref[...])
    for h in range(LRU_HEADS):
        sl = slice(h * hd, (h + 1) * hd)
        xh = xc[:, sl]
        z = _dot(xh.astype(BF16), w_ref[h]) + gb_ref[h]
        r = jax.nn.sigmoid(z[:, :hd])
        ig = jax.nn.sigmoid(z[:, hd:])
        log_a = -LRU_C * r * sp[:, sl]
        a = jnp.exp(log_a)
        a_ref[:, sl] = a
        b_ref[:, sl] = jnp.sqrt(-jnp.tanh(log_a) * (a * a + 1.0)) * (ig * xh)

    ng = tb // V7X_SUBLANES
    row = lax.broadcasted_iota(jnp.int32, (V7X_SUBLANES, width), 0)

    def body(g, carry):
        gi = (ng - 1 - g) if reverse else g
        off = pl.multiple_of(gi * V7X_SUBLANES, V7X_SUBLANES)
        a = a_ref[pl.ds(off, V7X_SUBLANES), :]
        b = b_ref[pl.ds(off, V7X_SUBLANES), :]
        for s in (1, 2, 4):
            if reverse:
                shift, m = V7X_SUBLANES - s, row < V7X_SUBLANES - s
            else:
                shift, m = s, row >= s
            b = jnp.where(m, a * pltpu.roll(b, shift, 0) + b, b)
            a = jnp.where(m, a * pltpu.roll(a, shift, 0), a)
        hcur = a * carry + b
        b_ref[pl.ds(off, V7X_SUBLANES), :] = hcur
        return hcur[0:1, :] if reverse else hcur[V7X_SUBLANES - 1:V7X_SUBLANES, :]

    carry_ref[...] = lax.fori_loop(0, ng, body, carry_ref[...])

    if not final:
        o_ref[...] = b_ref[...]
    else:
        y = jax.nn.gelu(gate_ref[...], approximate=True) * (hf_ref[...] + b_ref[...])
        for h in range(LRU_HEADS):
            sl = slice(h * hd, (h + 1) * hd)
            yh = y[:, sl]
            yh = yh * lax.rsqrt(jnp.mean(yh * yh, axis=-1, keepdims=True) + EPS)
            o_ref[:, sl] = (yh * norm_ref[:, sl]).astype(o_ref.dtype)


def lru_branch(proj, conv_w, conv_b, wr, br, wi, bi, lam, norm, tb=LRU_ROWS):
    s = proj.shape[0]
    c = conv_w.shape[1]
    hd = c // LRU_HEADS
    tb = min(tb, s)
    nb = s // tb
    w = jnp.concatenate([wr, wi], axis=-1).astype(BF16)
    gb = jnp.concatenate([br.reshape(2, LRU_HEADS, 1, hd), bi.reshape(2, LRU_HEADS, 1, hd)], axis=-1).astype(F32)
    full = lambda shape: pl.BlockSpec(shape, lambda i: (0,) * len(shape))
    scratch = [pltpu.VMEM((tb, c), F32), pltpu.VMEM((tb, c), F32), pltpu.VMEM((1, c), F32)]
    gates = [full((LRU_HEADS, hd, 2 * hd)), full((LRU_HEADS, 1, 2 * hd)), full((1, c))]
    fwd = pl.BlockSpec((tb, c), lambda i: (i, 0))
    h_fwd, xc = pl.pallas_call(
        functools.partial(_lru_kernel, backward=False, nb=nb, tb=tb),
        grid=(nb,),
        in_specs=_halo_specs(tb, c, 1, nb) + [full((4, c)), full((1, c))] + gates,
        out_specs=[fwd, fwd],
        out_shape=[jax.ShapeDtypeStruct((s, c), F32)] * 2,
        scratch_shapes=scratch,
        compiler_params=_params("arbitrary"),
        name="lru_fwd",
    )(proj, proj, proj, conv_w.astype(F32), conv_b.reshape(1, c).astype(F32), w[0], gb[0],
      lam[0].reshape(1, c).astype(F32))
    rev = pl.BlockSpec((tb, c), lambda i: (nb - 1 - i, 0))
    return pl.pallas_call(
        functools.partial(_lru_kernel, backward=True, nb=nb, tb=tb),
        grid=(nb,),
        in_specs=[rev] + gates + [rev, rev, full((1, c))],
        out_specs=rev,
        out_shape=jax.ShapeDtypeStruct((s, c), BF16),
        scratch_shapes=scratch,
        compiler_params=_params("arbitrary"),
        name="lru_bwd",
    )(xc, w[1], gb[1], lam[1].reshape(1, c).astype(F32), h_fwd, proj, norm.reshape(1, c).astype(F32))


def _ret_log_gamma():
    return [float(np.log1p(-np.exp2(np.float32(-5.0 - h)), dtype=np.float32)) for h in range(RET_HEADS)]


def _ret_kernel(*refs, reverse, nc, c):
    if reverse:
        (q_ref, k_ref, v0_ref, v1_ref, inv_ref, y1_ref, g0_ref, g1_ref, norm_ref,
         o_ref, state_ref, rope_ref, dec_ref) = refs
    else:
        q_ref, k_ref, v0_ref, v1_ref, inv_ref, o_ref, state_ref, rope_ref, dec_ref, dm_ref = refs
    i = pl.program_id(0)
    ti = (nc - 1 - i) if reverse else i
    dk = RET_KEY_DIM
    dv = v0_ref.shape[-1] * 2 // RET_HEADS
    log_g = _ret_log_gamma()
    idx = lax.broadcasted_iota(jnp.int32, (c, 1), 0).astype(F32)
    inv = inv_ref[...]

    @pl.when(i == 0)
    def _():
        state_ref[...] = jnp.zeros_like(state_ref)
        rope_ref[0] = jnp.cos(idx * inv)
        rope_ref[1] = jnp.sin(idx * inv)
        for h in range(RET_HEADS):
            q_pow, k_pow = (c - idx, idx) if reverse else (idx + 1.0, c - 1.0 - idx)
            dec_ref[h, 0] = jnp.broadcast_to(jnp.exp(log_g[h] * q_pow), (c, dk))
            dec_ref[h, 1] = jnp.broadcast_to(jnp.exp(log_g[h] * k_pow), (c, dk))
        if not reverse:
            d = jnp.abs(lax.broadcasted_iota(jnp.int32, (c, c), 0)
                        - lax.broadcasted_iota(jnp.int32, (c, c), 1)).astype(F32)
            for h in range(RET_HEADS):
                dm_ref[h] = jnp.exp(log_g[h] * d)

    start = (ti * c).astype(F32) * inv
    cos_s, sin_s = jnp.cos(start), jnp.sin(start)
    cos_o, sin_o = rope_ref[0], rope_ref[1]
    lane = lax.broadcasted_iota(jnp.int32, (c, dk), 1)
    cos_t = cos_s * cos_o - sin_s * sin_o
    sin_t = jnp.where(lane < dk // 2, -1.0, 1.0) * (sin_s * cos_o + cos_s * sin_o)

    def rot(x):
        return x * cos_t + pltpu.roll(x, dk // 2, 1) * sin_t

    hpb = RET_HEADS // 2
    for h in range(RET_HEADS):
        qh = rot(q_ref[:, h * dk:(h + 1) * dk])
        kh = rot(k_ref[:, h * dk:(h + 1) * dk]) * (dk ** -0.5)
        v_ref = v0_ref if h < hpb else v1_ref
        vs = slice((h % hpb) * dv, (h % hpb + 1) * dv)
        vh = v_ref[:, vs].astype(BF16)
        lg = log_g[h]
        q_dec = qh * dec_ref[h, 0]
        k_dec = kh * dec_ref[h, 1]
        if reverse:
            y = y1_ref[:, h * dv:(h + 1) * dv]
        else:
            scores = lax.dot_general(qh.astype(BF16), kh.astype(BF16), (((1,), (1,)), ((), ())),
                                     preferred_element_type=F32) * dm_ref[h]
            y = _dot(scores.astype(BF16), vh)
        st = state_ref[h]
        y = y + _dot(q_dec.astype(BF16), st.astype(BF16))
        kv = lax.dot_general(k_dec.astype(BF16), vh, (((0,), (0,)), ((), ())), preferred_element_type=F32)
        state_ref[h] = math.exp(lg * c) * st + kv
        if reverse:
            g_ref = g0_ref if h < hpb else g1_ref
            yn = y * lax.rsqrt(jnp.mean(y * y, axis=-1, keepdims=True) + EPS) * norm_ref[:, h * dv:(h + 1) * dv]
            o_ref[:, h * dv:(h + 1) * dv] = (jax.nn.silu(g_ref[:, vs]) * yn).astype(o_ref.dtype)
        else:
            o_ref[:, h * dv:(h + 1) * dv] = y


def retention_branch(proj, norm, col0, c=RET_CHUNK):
    s = proj.shape[0]
    d_ret = norm.shape[0]
    qk = RET_HEADS * RET_KEY_DIM
    half = RET_KEY_DIM // 2
    inv = ROPE_BASE ** (-jnp.arange(half, dtype=F32) / half)
    inv = jnp.concatenate([inv, inv]).reshape(1, RET_KEY_DIM)
    vw = d_ret // 2
    qb, kb = col0 // qk, (col0 + qk) // qk
    vb = (col0 + 2 * qk) // vw
    gb = (col0 + 2 * qk + d_ret) // vw
    assert col0 % qk == 0 and (col0 + 2 * qk) % vw == 0

    c = min(c, s)
    nc = s // c

    def one_pass(reverse, extra, out_dtype):
        ti = (lambda i: nc - 1 - i) if reverse else (lambda i: i)
        rows = lambda width, col: pl.BlockSpec((c, width), lambda i: (ti(i), col))
        full = lambda width: pl.BlockSpec((1, width), lambda i: (0, 0))
        scratch = [pltpu.VMEM((RET_HEADS, RET_KEY_DIM, d_ret // RET_HEADS), F32), pltpu.VMEM((2, c, RET_KEY_DIM), F32),
                   pltpu.VMEM((RET_HEADS, 2, c, RET_KEY_DIM), F32)]
        in_specs = [rows(qk, qb), rows(qk, kb), rows(vw, vb), rows(vw, vb + 1), full(RET_KEY_DIM)]
        args = [proj, proj, proj, proj, inv]
        if reverse:
            in_specs += [rows(d_ret, 0), rows(vw, gb), rows(vw, gb + 1), full(d_ret)]
            args += [extra, proj, proj, norm.reshape(1, d_ret).astype(F32)]
        else:
            scratch.append(pltpu.VMEM((RET_HEADS, c, c), F32))
        return pl.pallas_call(
            functools.partial(_ret_kernel, reverse=reverse, nc=nc, c=c),
            grid=(nc,),
            in_specs=in_specs,
            out_specs=rows(d_ret, 0),
            out_shape=jax.ShapeDtypeStruct((s, d_ret), out_dtype),
            scratch_shapes=scratch,
            compiler_params=_params("arbitrary"),
            name="ret_bwd" if reverse else "ret_fwd",
        )(*args)

    return one_pass(True, one_pass(False, None, F32), BF16)


def _hy_conv_kernel(xp_ref, x_ref, xn_ref, w_ref, b_ref, o_ref, *, nb, tb):
    ti = pl.program_id(1)
    o_ref[...] = _dwconv_block(xp_ref, x_ref, xn_ref, w_ref, b_ref, ti, nb, tb, 3)


def hyena_short_conv(proj, conv_w, conv_b, col_block, c, tb=CONV_ROWS):
    s = proj.shape[0]
    tb = min(tb, s)
    nb = s // tb
    r = tb // V7X_SUBLANES
    last = nb * r - 1
    return pl.pallas_call(
        functools.partial(_hy_conv_kernel, nb=nb, tb=tb),
        grid=(3, nb),
        in_specs=[
            pl.BlockSpec((V7X_SUBLANES, c), lambda j, i: (jnp.maximum(i * r - 1, 0), col_block + j)),
            pl.BlockSpec((tb, c), lambda j, i: (i, col_block + j)),
            pl.BlockSpec((V7X_SUBLANES, c), lambda j, i: (jnp.minimum((i + 1) * r, last), col_block + j)),
            pl.BlockSpec((3, c), lambda j, i: (0, j)),
            pl.BlockSpec((1, c), lambda j, i: (0, j)),
        ],
        out_specs=pl.BlockSpec((None, tb, c), lambda j, i: (j, i, 0)),
        out_shape=jax.ShapeDtypeStruct((3, s, c), F32),
        compiler_params=_params("parallel", "parallel"),
        name="hyena_short_conv",
    )(proj, proj, proj, conv_w.astype(F32), conv_b.reshape(1, -1).astype(F32))


def _fft_sizes(length):
    n = 2 * length
    n2 = min(FFT_N2, n // 4)
    n1 = n // n2
    return n, n1, n2, n1 // 2, n1 // 2 + 1


def _fft_tables(length):
    n, n1, n2, n1h, k1n = _fft_sizes(length)
    eye = np.eye(V7X_SUBLANES)
    k1 = np.arange(k1n)[:, None]
    a1 = np.arange(n1h)[None, :]
    ang1 = 2.0 * np.pi * ((k1 * a1) % n1) / n1
    f1 = np.concatenate([np.kron(np.cos(ang1), eye), np.kron(-np.sin(ang1), eye)], axis=0)
    wgt = np.full((k1n,), 2.0)
    wgt[0] = wgt[-1] = 1.0
    cw = (np.cos(ang1) * wgt[:, None] / n).T
    sw = (np.sin(ang1) * wgt[:, None] / n).T
    b3 = np.concatenate([np.kron(cw, eye), np.kron(-sw, eye)], axis=1)
    a2 = np.arange(n2)
    angt = 2.0 * np.pi * (k1 * a2[None, :]) / n
    tw = jnp.asarray(np.stack([np.cos(angt), np.sin(angt)]), F32)
    tw = jnp.broadcast_to(tw[..., None], tw.shape + (V7X_LANES,))
    ang2 = 2.0 * np.pi * ((a2[:, None] * a2[None, :]) % n2) / n2
    c2, s2 = np.cos(ang2), np.sin(ang2)
    f2 = np.block([[c2, s2], [-s2, c2]])
    f2i = np.block([[c2, -s2], [s2, c2]])
    return jnp.asarray(f1, BF16), tw, jnp.asarray(f2, BF16), jnp.asarray(f2i, BF16), jnp.asarray(b3, BF16)


def _pack_pair(re, im):
    hi = lax.bitcast_convert_type(re.astype(BF16).astype(F32), jnp.uint32)
    lo = lax.bitcast_convert_type(im.astype(BF16).astype(F32), jnp.uint32)
    return hi | (lo >> 16)


def _unpack_pair(w):
    re = lax.bitcast_convert_type(w & jnp.uint32(0xFFFF0000), F32)
    im = lax.bitcast_convert_type(w << 16, F32)
    return re, im


def _twiddle(c, s, xr, xi):
    out_r, out_i = [], []
    for g in range(xr.shape[-1] // V7X_LANES):
        sl = slice(g * V7X_LANES, (g + 1) * V7X_LANES)
        out_r.append(xr[..., sl] * c + xi[..., sl] * s)
        out_i.append(xi[..., sl] * c - xr[..., sl] * s)
    return jnp.concatenate(out_r, axis=-1), jnp.concatenate(out_i, axis=-1)


def _stage1(z, f1_ref, tw_ref, k1n):
    sub = V7X_SUBLANES
    cb = z.shape[-1]
    a = _dot(f1_ref[...], z.astype(BF16))
    ar = a[:k1n * sub].reshape(k1n, sub, cb)
    ai = a[k1n * sub:].reshape(k1n, sub, cb)
    return _pack_pair(*_twiddle(tw_ref[0], tw_ref[1], ar, ai))


def _fft1_kernel(z_ref, f1_ref, tw_ref, a_ref):
    n1h, sub, cb = z_ref.shape
    a_ref[...] = _stage1(z_ref[...].reshape(n1h * sub, cb), f1_ref, tw_ref, a_ref.shape[0])


def fft_stage1(z, f1, tw, length, cb=FFT_COL_BLOCK):
    _, _, n2, n1h, k1n = _fft_sizes(length)
    c = z.shape[-1]
    cb = min(cb, c)
    lead = z.shape[:-2]
    z4 = z.reshape(lead + (n1h, n2, c))
    nl = len(lead)
    sub = V7X_SUBLANES
    return pl.pallas_call(
        _fft1_kernel,
        grid=(n2 // sub, c // cb),
        in_specs=[pl.BlockSpec((None,) * nl + (n1h, sub, cb), lambda j, ci: (0,) * nl + (0, j, ci)),
                  pl.BlockSpec(f1.shape, lambda j, ci: (0, 0)),
                  pl.BlockSpec((2, k1n, sub, V7X_LANES), lambda j, ci: (0, 0, j, 0))],
        out_specs=pl.BlockSpec((k1n, sub, cb), lambda j, ci: (0, j, ci)),
        out_shape=jax.ShapeDtypeStruct((k1n, n2, c), jnp.uint32),
        compiler_params=_params("parallel", "parallel"),
        name="fft_stage1",
    )(z4, f1, tw)


def _filter_fft1_kernel(fb_ref, w1_ref, b1_ref, w2_ref, b2_ref, w3_ref, freq_ref, decay_ref, f1_ref, tw_ref,
                        a_ref, hdn_ref, trig_ref, *, length, n2):
    k1n, sub, cb = a_ref.shape
    rows = hdn_ref.shape[0]
    j = pl.program_id(0)
    hi = lax.Precision.HIGHEST
    rho = lax.broadcasted_iota(jnp.int32, (rows, 1), 0)
    base = ((rho // sub) * n2).astype(F32)
    idx = base + (j * sub + rho % sub).astype(F32)
    t = idx / (length - 1.0)
    lane = lax.broadcasted_iota(jnp.int32, (rows, V7X_LANES), 1)
    w_band = (2.0 * math.pi / length) * fb_ref[...]

    @pl.when((j == 0) & (pl.program_id(1) == 0))
    def _():
        shift = jnp.where(lane <= HYENA_BANDS, 0.5 * math.pi, math.pi)
        trig_ref[0] = jnp.sin(w_band * base + shift)
        trig_ref[1] = jnp.cos(w_band * base + shift)

    @pl.when(pl.program_id(1) == 0)
    def _():
        off = (j * sub).astype(F32) + lax.broadcasted_iota(jnp.int32, (sub, 1), 0).astype(F32)
        sin_b, cos_b = jnp.sin(w_band * off), jnp.cos(w_band * off)
        trig = (trig_ref[0].reshape(rows // sub, sub, V7X_LANES) * cos_b
                + trig_ref[1].reshape(rows // sub, sub, V7X_LANES) * sin_b).reshape(rows, V7X_LANES)
        feats = jnp.where(lane == 0, t, jnp.where(lane <= 2 * HYENA_BANDS, trig, 0.0))
        freq = freq_ref[...]
        hdn = jnp.sin(freq * (jnp.dot(feats, w1_ref[...], precision=hi, preferred_element_type=F32) + b1_ref[...]))
        hdn_ref[...] = jnp.sin(freq * (jnp.dot(hdn, w2_ref[...], precision=hi, preferred_element_type=F32)
                                       + b2_ref[...]))

    filt = _dot(hdn_ref[...].astype(BF16), w3_ref[...]) * jnp.exp(-t * jnp.abs(decay_ref[...]))
    a_ref[...] = _stage1(filt, f1_ref, tw_ref, k1n)


def filter_fft_stage1(length, w1, b1, w2, b2, w3, freq, decay, f1, tw, cb=FFT_COL_BLOCK):
    _, _, n2, n1h, k1n = _fft_sizes(length)
    emb, hid = w1.shape
    bands = (emb - 1) // 2
    assert bands == HYENA_BANDS
    n_out = w3.shape[1]
    cb = min(cb, n_out)
    sub = V7X_SUBLANES
    f = jnp.linspace(1e-4, bands - 1, bands, dtype=F32)
    fb = jnp.zeros((1, V7X_LANES), F32).at[0, 1:1 + bands].set(f).at[0, 1 + bands:1 + 2 * bands].set(f)
    w1p = jnp.zeros((V7X_LANES, hid), F32).at[:emb].set(w1.astype(F32))
    full = lambda shape: pl.BlockSpec(shape, lambda j, ci: (0,) * len(shape))
    return pl.pallas_call(
        functools.partial(_filter_fft1_kernel, length=length, n2=n2),
        grid=(n2 // sub, n_out // cb),
        in_specs=[full((1, V7X_LANES)), full((V7X_LANES, hid)), full((1, hid)), full((hid, hid)), full((1, hid)),
                  pl.BlockSpec((hid, cb), lambda j, ci: (0, ci)), full((1, hid)),
                  pl.BlockSpec((1, cb), lambda j, ci: (0, ci)), full(f1.shape),
                  pl.BlockSpec((2, k1n, sub, V7X_LANES), lambda j, ci: (0, 0, j, 0))],
        out_specs=pl.BlockSpec((k1n, sub, cb), lambda j, ci: (0, j, ci)),
        out_shape=jax.ShapeDtypeStruct((k1n, n2, n_out), jnp.uint32),
        scratch_shapes=[pltpu.VMEM((n1h * sub, hid), F32), pltpu.VMEM((2, n1h * sub, V7X_LANES), F32)],
        compiler_params=_params("arbitrary", "arbitrary"),
        name="hyena_filter_fft1",
    )(fb, w1p, b1.reshape(1, hid).astype(F32), w2.astype(F32), b2.reshape(1, hid).astype(F32), w3.astype(BF16),
      freq.reshape(1, hid).astype(F32), decay.reshape(1, n_out).astype(F32), f1, tw)


def _slab_dft(t_ref, xr, xi):
    n2 = xr.shape[0]
    x = jnp.concatenate([xr.astype(BF16), xi.astype(BF16)], axis=0)
    y = _dot(t_ref[...], x)
    return y[:n2], y[n2:]


def _slabs_per_step(k1n, most=FFT_SLABS_PER_STEP):
    return max(d for d in range(1, most + 1) if k1n % d == 0)


def _filter_spec_kernel(af_ref, ab_ref, f2_ref, f2i_ref, g_ref):
    n2 = af_ref.shape[1]
    for k in range(af_ref.shape[0]):
        ar, ai = _unpack_pair(af_ref[k])
        br, bi = _unpack_pair(ab_ref[k])
        top = jnp.concatenate([(ar + br).astype(BF16), (ai + bi).astype(BF16)], axis=0)
        bot = jnp.concatenate([(ai - bi).astype(BF16), (ar - br).astype(BF16)], axis=0)
        g_ref[k] = _pack_pair(_dot(f2_ref[:n2, :], top), _dot(f2i_ref[:n2, :], bot))


def filter_spectrum(a, f2, f2i, c, cb=FFT_COL_BLOCK):
    k1n, n2, cf = a.shape
    orders = cf // (2 * c)
    cb = min(cb, c)
    per = c // cb
    kb = _slabs_per_step(k1n)
    fcol = lambda k, j: (k, 0, (j // per) * 2 * per + j % per)
    bcol = lambda k, j: (k, 0, (j // per) * 2 * per + per + j % per)
    blk = (kb, n2, cb)
    f2_spec = pl.BlockSpec((2 * n2, 2 * n2), lambda k, j: (0, 0))
    return pl.pallas_call(
        _filter_spec_kernel,
        grid=(k1n // kb, orders * per),
        in_specs=[pl.BlockSpec(blk, fcol), pl.BlockSpec(blk, bcol), f2_spec, f2_spec],
        out_specs=pl.BlockSpec(blk, lambda k, j: (k, 0, j)),
        out_shape=jax.ShapeDtypeStruct((k1n, n2, orders * c), jnp.uint32),
        compiler_params=_params("parallel", "parallel"),
        name="hyena_filter_spectrum",
    )(a, a, f2, f2i)


def _slab_conv_kernel(a_ref, g_ref, tw_ref, f2_ref, f2i_ref, p_ref):
    for k in range(a_ref.shape[0]):
        xr, xi = _slab_dft(f2_ref, *_unpack_pair(a_ref[k]))
        gr, gi = _unpack_pair(g_ref[k])
        qr, qi = _slab_dft(f2i_ref, xr * gr - xi * gi, xr * gi + xi * gr)
        p_ref[k] = _pack_pair(*_twiddle(tw_ref[0, k], -tw_ref[1, k], qr, qi))


def slab_conv(a, g, tw, f2, f2i, order, cb=FFT_COL_BLOCK):
    k1n, n2, c = a.shape
    cb = min(cb, c)
    per = c // cb
    kb = _slabs_per_step(k1n)
    blk = (kb, n2, cb)
    dcol = lambda k, j: (k, 0, j)
    gcol = lambda k, j: (k, 0, order * per + j)
    f2_spec = pl.BlockSpec((2 * n2, 2 * n2), lambda k, j: (0, 0))
    return pl.pallas_call(
        _slab_conv_kernel,
        grid=(k1n // kb, per),
        in_specs=[pl.BlockSpec(blk, dcol), pl.BlockSpec(blk, gcol),
                  pl.BlockSpec((2, kb, n2, V7X_LANES), lambda k, j: (0, k, 0, 0)), f2_spec, f2_spec],
        out_specs=pl.BlockSpec(blk, dcol),
        out_shape=jax.ShapeDtypeStruct((k1n, n2, c), jnp.uint32),
        compiler_params=_params("parallel", "parallel"),
        name="hyena_slab_conv",
    )(a, g, tw, f2, f2i)


def _ifft3_kernel(*refs, last, groups):
    if last:
        p_ref, b3_ref, z_ref, gate_ref, skip_ref, norm_ref, o_ref = refs
    else:
        p_ref, b3_ref, z_ref, gate_ref, skip_ref, f1_ref, tw_ref, o_ref, a_ref = refs
    k1n, sub, cb = p_ref.shape
    n1h = o_ref.shape[0]
    pr, pi = _unpack_pair(p_ref[...])
    p = jnp.concatenate([pr.reshape(k1n * sub, cb).astype(BF16), pi.reshape(k1n * sub, cb).astype(BF16)], axis=0)
    y = _dot(b3_ref[...], p).reshape(n1h, sub, cb)
    z = gate_ref[...] * (y + skip_ref[...] * z_ref[...])
    if last:
        w = cb // groups
        for h in range(groups):
            zh = z[..., h * w:(h + 1) * w]
            zh = zh * lax.rsqrt(jnp.mean(zh * zh, axis=-1, keepdims=True) + EPS)
            o_ref[:, :, h * w:(h + 1) * w] = zh * norm_ref[:, h * w:(h + 1) * w]
    else:
        o_ref[...] = z
        a_ref[...] = _stage1(z.reshape(n1h * sub, cb), f1_ref, tw_ref, k1n)


def ifft_stage3_gate(p, b3, u, z_arr, gate_idx, skip, length, f1=None, tw=None, norm=None):
    _, _, n2, n1h, k1n = _fft_sizes(length)
    c = p.shape[-1]
    last = norm is not None
    sub = V7X_SUBLANES
    u4 = u.reshape(u.shape[0], n1h, n2, c)
    z4 = z_arr.reshape((-1, n1h, n2, c))
    pblk = pl.BlockSpec((k1n, sub, c), lambda j: (0, j, 0))
    zblk = pl.BlockSpec((n1h, sub, c), lambda j: (0, j, 0))
    full = lambda shape: pl.BlockSpec(shape, lambda j: (0,) * len(shape))
    in_specs = [pblk, full(b3.shape), pl.BlockSpec((None, n1h, sub, c), lambda j: (0, 0, j, 0)),
                pl.BlockSpec((None, n1h, sub, c), lambda j: (gate_idx, 0, j, 0)), full((1, c))]
    args = [p, b3, z4, u4, skip.reshape(1, c).astype(F32)]
    z_shape = jax.ShapeDtypeStruct((n1h, n2, c), F32)
    if last:
        in_specs.append(full((1, c)))
        args.append(norm.reshape(1, c).astype(F32))
        out_specs, out_shape = zblk, z_shape
    else:
        in_specs += [full(f1.shape), pl.BlockSpec((2, k1n, sub, V7X_LANES), lambda j: (0, 0, j, 0))]
        args += [f1, tw]
        out_specs, out_shape = [zblk, pblk], [z_shape, jax.ShapeDtypeStruct((k1n, n2, c), jnp.uint32)]
    out = pl.pallas_call(
        functools.partial(_ifft3_kernel, last=last, groups=HYENA_GROUPS),
        grid=(n2 // sub,),
        in_specs=in_specs,
        out_specs=out_specs,
        out_shape=out_shape,
        compiler_params=_params("parallel"),
        name="ifft_stage3_norm" if last else "ifft_stage3_fft1",
    )(*args)
    if last:
        return out.reshape(length, c)
    return out[0].reshape(1, length, c), out[1]


def hyena_branch(proj, tables, conv_w, conv_b, f_w1, f_b1, f_w2, f_b2, f_w3, freq, decay, skip, norm, col_block):
    length = proj.shape[0]
    c = norm.shape[0]
    orders = skip.shape[0]
    f1, tw, f2, f2i, b3 = tables
    u = hyena_short_conv(proj, conv_w, conv_b, col_block, c)
    a_filt = filter_fft_stage1(length, f_w1, f_b1, f_w2, f_b2, f_w3, freq, decay, f1, tw)
    g = filter_spectrum(a_filt, f2, f2i, c)
    z, a = u, fft_stage1(u, f1, tw, length)
    for o in range(orders - 1):
        p = slab_conv(a, g, tw, f2, f2i, o)
        z, a = ifft_stage3_gate(p, b3, u, z, 1 + o, skip[o], length, f1=f1, tw=tw)
    p = slab_conv(a, g, tw, f2, f2i, orders - 1)
    return ifft_stage3_gate(p, b3, u, z, orders, skip[orders - 1], length, norm=norm)


def kernel(x, norm_mix, w_in, lru_conv_w, lru_conv_b, lru_wr, lru_br, lru_wi, lru_bi, lru_lambda, lru_norm, hy_conv_w, hy_conv_b, hy_f_w1, hy_f_b1, hy_f_w2, hy_f_b2, hy_f_w3, hy_freq, hy_decay, hy_skip, hy_norm, ret_norm, w_out, norm_ffn, w_gate, w_up, w_down, norm_final):
    b, s, d = x.shape
    assert b == 1
    depth = w_in.shape[0]
    d_lru = lru_conv_w.shape[-1]
    d_hy = hy_norm.shape[-1]
    tables = _fft_tables(s)
    xs = x.reshape(s, d)
    for l in range(depth):
        h = rmsnorm(xs, norm_mix[l], BF16)
        proj = matmul_fullk([h], cast_bf16(w_in, l), tiles=IN_PROJ_TILES)
        y_a = lru_branch(proj, lru_conv_w[l], lru_conv_b[l], lru_wr[l], lru_br[l], lru_wi[l], lru_bi[l],
                         lru_lambda[l], lru_norm[l])
        y_b = hyena_branch(proj, tables, hy_conv_w[l], hy_conv_b[l], hy_f_w1[l], hy_f_b1[l], hy_f_w2[l],
                           hy_f_b2[l], hy_f_w3[l], hy_freq[l], hy_decay[l], hy_skip[l], hy_norm[l],
                           (2 * d_lru) // d_hy)
        y_c = retention_branch(proj, ret_norm[l], 2 * d_lru + 3 * d_hy)
        xs = matmul_fullk([y_a, y_b, y_c], w_out, l, residual=xs)
        h = rmsnorm(xs, norm_ffn[l], BF16)
        act = ffn_up(h, w_gate, w_up, l)
        xs = matmul_fullk([act], cast_bf16(w_down, l), residual=xs)
    return rmsnorm(xs, norm_final, x.dtype).reshape(b, s, d)
```

```python
import functools
import math

import numpy as np
import jax
import jax.numpy as jnp
from jax import lax
from jax.experimental import pallas as pl
from jax.experimental.pallas import tpu as pltpu

EPS = 1e-6
LRU_HEADS = 8
LRU_C = 8.0
HYENA_GROUPS = 8
HYENA_BANDS = 16
RET_HEADS = 8
RET_KEY_DIM = 128
ROPE_BASE = 10000.0

V7X_SUBLANES = 8
V7X_LANES = 128
VMEM_LIMIT_BYTES = 56 * 1024 * 1024
FFT_N2 = 256
FFT_SLABS_PER_STEP = 5
FFT_COL_BLOCK = 1024
NORM_ROWS = 512
LRU_ROWS = 1024
CONV_ROWS = 1024
RET_CHUNK = 256
CAST_ROWS = 256
MATMUL_TILES = ((2048, 256), (1024, 512), (512, 512), (256, 256))
IN_PROJ_TILES = (1024, 1024)
F32 = jnp.float32
BF16 = jnp.bfloat16


def _params(*sem):
    return pltpu.CompilerParams(dimension_semantics=sem, vmem_limit_bytes=VMEM_LIMIT_BYTES)


def _dot(a, b):
    return jnp.dot(a, b, preferred_element_type=F32)


def _rmsnorm_kernel(x_ref, g_ref, o_ref):
    x = x_ref[...]
    y = x * lax.rsqrt(jnp.mean(x * x, axis=-1, keepdims=True) + EPS)
    o_ref[...] = (y * g_ref[...]).astype(o_ref.dtype)


def rmsnorm(x, gain, out_dtype, tm=NORM_ROWS):
    s, d = x.shape
    tm = min(tm, s)
    return pl.pallas_call(
        _rmsnorm_kernel,
        grid=(s // tm,),
        in_specs=[pl.BlockSpec((tm, d), lambda i: (i, 0)), pl.BlockSpec((1, d), lambda i: (0, 0))],
        out_specs=pl.BlockSpec((tm, d), lambda i: (i, 0)),
        out_shape=jax.ShapeDtypeStruct((s, d), out_dtype),
        compiler_params=_params("parallel"),
        name="rmsnorm",
    )(x, gain.reshape(1, d).astype(F32))


def _layer_spec(layer, block, index_map):
    return pl.BlockSpec((None,) + block, lambda *g: (layer,) + index_map(*g))


def _matmul_tiles(m, n, a_row_bytes, w_col_bytes, io_bytes, n_results):
    for tm, tn in MATMUL_TILES:
        tm, tn = min(tm, m), min(tn, n)
        blocks = 2 * (tm * a_row_bytes + tn * w_col_bytes + tm * tn * io_bytes)
        if m % tm == 0 and n % tn == 0 and blocks + n_results * tm * tn * 4 <= VMEM_LIMIT_BYTES:
            return tm, tn
    raise ValueError("no matmul tile fits VMEM")


def _mm_fullk_kernel(*refs, n_a, has_res):
    a_refs, b_ref, o_ref = refs[:n_a], refs[n_a], refs[-1]
    acc, off = None, 0
    for a_ref in a_refs:
        kw = a_ref.shape[1]
        part = _dot(a_ref[...].astype(BF16), b_ref[off:off + kw, :].astype(BF16))
        acc = part if acc is None else acc + part
        off += kw
    if has_res:
        acc = refs[n_a + 1][...] + acc
    o_ref[...] = acc.astype(o_ref.dtype)


def matmul_fullk(a_parts, w, layer=None, residual=None, out_dtype=F32, tiles=None):
    m = a_parts[0].shape[0]
    kd, n = w.shape[-2:]
    assert sum(a.shape[1] for a in a_parts) == kd
    has_res = residual is not None
    tm, tn = tiles or _matmul_tiles(
        m, n, sum(a.shape[1] * a.dtype.itemsize for a in a_parts), kd * w.dtype.itemsize,
        jnp.dtype(out_dtype).itemsize + (residual.dtype.itemsize if has_res else 0), min(len(a_parts), 2))
    in_specs = [pl.BlockSpec((tm, a.shape[1]), lambda i, j: (i, 0)) for a in a_parts]
    if layer is None:
        in_specs.append(pl.BlockSpec((kd, tn), lambda i, j: (0, j)))
    else:
        in_specs.append(_layer_spec(layer, (kd, tn), lambda i, j: (0, j)))
    args = list(a_parts) + [w]
    if has_res:
        in_specs.append(pl.BlockSpec((tm, tn), lambda i, j: (i, j)))
        args.append(residual)
    return pl.pallas_call(
        functools.partial(_mm_fullk_kernel, n_a=len(a_parts), has_res=has_res),
        grid=(m // tm, n // tn),
        in_specs=in_specs,
        out_specs=pl.BlockSpec((tm, tn), lambda i, j: (i, j)),
        out_shape=jax.ShapeDtypeStruct((m, n), out_dtype),
        compiler_params=_params("parallel", "parallel"),
        name="matmul_fullk_res" if has_res else "matmul_fullk",
    )(*args)


def _ffn_up_kernel(h_ref, wg_ref, wu_ref, o_ref):
    h = h_ref[...]
    g = _dot(h, wg_ref[...].astype(BF16))
    u = _dot(h, wu_ref[...].astype(BF16))
    o_ref[...] = (jax.nn.silu(g) * u).astype(o_ref.dtype)


def ffn_up(h, wg, wu, layer, tiles=None):
    m, kd = h.shape
    n = wg.shape[2]
    tm, tn = tiles or _matmul_tiles(m, n, kd * h.dtype.itemsize, kd * (wg.dtype.itemsize + wu.dtype.itemsize),
                                    jnp.dtype(BF16).itemsize, 2)
    wspec = _layer_spec(layer, (kd, tn), lambda i, j: (0, j))
    return pl.pallas_call(
        _ffn_up_kernel,
        grid=(m // tm, n // tn),
        in_specs=[pl.BlockSpec((tm, kd), lambda i, j: (i, 0)), wspec, wspec],
        out_specs=pl.BlockSpec((tm, tn), lambda i, j: (i, j)),
        out_shape=jax.ShapeDtypeStruct((m, n), BF16),
        compiler_params=_params("parallel", "parallel"),
        name="ffn_up",
    )(h, wg, wu)


def _cast_kernel(w_ref, o_ref):
    o_ref[...] = w_ref[...].astype(o_ref.dtype)


def cast_bf16(w, layer, tb=CAST_ROWS):
    _, r, n = w.shape
    tb = min(tb, r)
    assert r % tb == 0
    return pl.pallas_call(
        _cast_kernel,
        grid=(r // tb,),
        in_specs=[_layer_spec(layer, (tb, n), lambda i: (i, 0))],
        out_specs=pl.BlockSpec((tb, n), lambda i: (i, 0)),
        out_shape=jax.ShapeDtypeStruct((r, n), BF16),
        compiler_params=_params("parallel"),
        name="cast_bf16",
    )(w)


def _halo_specs(tb, width, col, nb):
    r = tb // V7X_SUBLANES
    last = nb * r - 1
    return [
        pl.BlockSpec((V7X_SUBLANES, width), lambda i: (jnp.maximum(i * r - 1, 0), col)),
        pl.BlockSpec((tb, width), lambda i: (i, col)),
        pl.BlockSpec((V7X_SUBLANES, width), lambda i: (jnp.minimum((i + 1) * r, last), col)),
    ]


def _dwconv_block(prev_ref, x_ref, next_ref, w_ref, b_ref, ti, nb, tb, width):
    left = width // 2
    prev = jnp.where(ti == 0, 0.0, prev_ref[...])
    nxt = jnp.where(ti == nb - 1, 0.0, next_ref[...])
    ext = jnp.concatenate([prev, x_ref[...], nxt], axis=0)
    n = ext.shape[0]
    y = b_ref[...]
    for j in range(width):
        shift = (left - j) % n
        rolled = ext if shift == 0 else pltpu.roll(ext, shift, 0)
        y = y + rolled[V7X_SUBLANES:V7X_SUBLANES + tb] * w_ref[j:j + 1, :]
    return y


def _softplus(x):
    return jnp.maximum(x, 0.0) + jnp.log1p(jnp.exp(-jnp.abs(x)))


def _lru_kernel(*refs, backward, nb, tb):
    reverse = final = backward
    if final:
        xc_ref, w_ref, gb_ref, lam_ref, hf_ref, gate_ref, norm_ref, o_ref, a_ref, b_ref, carry_ref = refs
    else:
        (xp_ref, x_ref, xn_ref, cw_ref, cb_ref, w_ref, gb_ref, lam_ref,
         o_ref, xc_ref, a_ref, b_ref, carry_ref) = refs
    i = pl.program_id(0)
    hd = V7X_LANES
    width = o_ref.shape[-1]

    @pl.when(i == 0)
    def _():
        carry_ref[...] = jnp.zeros_like(carry_ref)

    if final:
        xc = xc_ref[...]
    else:
        xc = _dwconv_block(xp_ref, x_ref, xn_ref, cw_ref, cb_ref, i, nb, tb, 4)
        xc_ref[...] = xc
    sp = _softplus(-lam_ref[...])
    for h in range(LRU_HEADS):
        sl = slice(h * hd, (h + 1) * hd)
        xh = xc[:, sl]
        z = _dot(xh.astype(BF16), w_ref[h]) + gb_ref[h]
        r = jax.nn.sigmoid(z[:, :hd])
        ig = jax.nn.sigmoid(z[:, hd:])
        log_a = -LRU_C * r * sp[:, sl]
        a = jnp.exp(log_a)
        a_ref[:, sl] = a
        b_ref[:, sl] = jnp.sqrt(-jnp.tanh(log_a) * (a * a + 1.0)) * (ig * xh)

    ng = tb // V7X_SUBLANES
    row = lax.broadcasted_iota(jnp.int32, (V7X_SUBLANES, width), 0)

    def body(g, carry):
        gi = (ng - 1 - g) if reverse else g
        off = pl.multiple_of(gi * V7X_SUBLANES, V7X_SUBLANES)
        a = a_ref[pl.ds(off, V7X_SUBLANES), :]
        b = b_ref[pl.ds(off, V7X_SUBLANES), :]
        for s in (1, 2, 4):
            if reverse:
                shift, m = V7X_SUBLANES - s, row < V7X_SUBLANES - s
            else:
                shift, m = s, row >= s
            b = jnp.where(m, a * pltpu.roll(b, shift, 0) + b, b)
            a = jnp.where(m, a * pltpu.roll(a, shift, 0), a)
        hcur = a * carry + b
        b_ref[pl.ds(off, V7X_SUBLANES), :] = hcur
        return hcur[0:1, :] if reverse else hcur[V7X_SUBLANES - 1:V7X_SUBLANES, :]

    carry_ref[...] = lax.fori_loop(0, ng, body, carry_ref[...])

    if not final:
        o_ref[...] = b_ref[...]
    else:
        y = jax.nn.gelu(gate_ref[...], approximate=True) * (hf_ref[...] + b_ref[...])
        for h in range(LRU_HEADS):
            sl = slice(h * hd, (h + 1) * hd)
            yh = y[:, sl]
            yh = yh * lax.rsqrt(jnp.mean(yh * yh, axis=-1, keepdims=True) + EPS)
            o_ref[:, sl] = (yh * norm_ref[:, sl]).astype(o_ref.dtype)


def lru_branch(proj, conv_w, conv_b, wr, br, wi, bi, lam, norm, tb=LRU_ROWS):
    s = proj.shape[0]
    c = conv_w.shape[1]
    hd = c // LRU_HEADS
    tb = min(tb, s)
    nb = s // tb
    w = jnp.concatenate([wr, wi], axis=-1).astype(BF16)
    gb = jnp.concatenate([br.reshape(2, LRU_HEADS, 1, hd), bi.reshape(2, LRU_HEADS, 1, hd)], axis=-1).astype(F32)
    full = lambda shape: pl.BlockSpec(shape, lambda i: (0,) * len(shape))
    scratch = [pltpu.VMEM((tb, c), F32), pltpu.VMEM((tb, c), F32), pltpu.VMEM((1, c), F32)]
    gates = [full((LRU_HEADS, hd, 2 * hd)), full((LRU_HEADS, 1, 2 * hd)), full((1, c))]
    fwd = pl.BlockSpec((tb, c), lambda i: (i, 0))
    h_fwd, xc = pl.pallas_call(
        functools.partial(_lru_kernel, backward=False, nb=nb, tb=tb),
        grid=(nb,),
        in_specs=_halo_specs(tb, c, 1, nb) + [full((4, c)), full((1, c))] + gates,
        out_specs=[fwd, fwd],
        out_shape=[jax.ShapeDtypeStruct((s, c), F32)] * 2,
        scratch_shapes=scratch,
        compiler_params=_params("arbitrary"),
        name="lru_fwd",
    )(proj, proj, proj, conv_w.astype(F32), conv_b.reshape(1, c).astype(F32), w[0], gb[0],
      lam[0].reshape(1, c).astype(F32))
    rev = pl.BlockSpec((tb, c), lambda i: (nb - 1 - i, 0))
    return pl.pallas_call(
        functools.partial(_lru_kernel, backward=True, nb=nb, tb=tb),
        grid=(nb,),
        in_specs=[rev] + gates + [rev, rev, full((1, c))],
        out_specs=rev,
        out_shape=jax.ShapeDtypeStruct((s, c), BF16),
        scratch_shapes=scratch,
        compiler_params=_params("arbitrary"),
        name="lru_bwd",
    )(xc, w[1], gb[1], lam[1].reshape(1, c).astype(F32), h_fwd, proj, norm.reshape(1, c).astype(F32))


def _ret_log_gamma():
    return [float(np.log1p(-np.exp2(np.float32(-5.0 - h)), dtype=np.float32)) for h in range(RET_HEADS)]


def _ret_kernel(*refs, reverse, nc, c):
    if reverse:
        (q_ref, k_ref, v0_ref, v1_ref, inv_ref, y1_ref, g0_ref, g1_ref, norm_ref,
         o_ref, state_ref, rope_ref, dec_ref) = refs
    else:
        q_ref, k_ref, v0_ref, v1_ref, inv_ref, o_ref, state_ref, rope_ref, dec_ref, dm_ref = refs
    i = pl.program_id(0)
    ti = (nc - 1 - i) if reverse else i
    dk = RET_KEY_DIM
    dv = v0_ref.shape[-1] * 2 // RET_HEADS
    log_g = _ret_log_gamma()
    idx = lax.broadcasted_iota(jnp.int32, (c, 1), 0).astype(F32)
    inv = inv_ref[...]

    @pl.when(i == 0)
    def _():
        state_ref[...] = jnp.zeros_like(state_ref)
        rope_ref[0] = jnp.cos(idx * inv)
        rope_ref[1] = jnp.sin(idx * inv)
        for h in range(RET_HEADS):
            q_pow, k_pow = (c - idx, idx) if reverse else (idx + 1.0, c - 1.0 - idx)
            dec_ref[h, 0] = jnp.broadcast_to(jnp.exp(log_g[h] * q_pow), (c, dk))
            dec_ref[h, 1] = jnp.broadcast_to(jnp.exp(log_g[h] * k_pow), (c, dk))
        if not reverse:
            d = jnp.abs(lax.broadcasted_iota(jnp.int32, (c, c), 0)
                        - lax.broadcasted_iota(jnp.int32, (c, c), 1)).astype(F32)
            for h in range(RET_HEADS):
                dm_ref[h] = jnp.exp(log_g[h] * d)

    start = (ti * c).astype(F32) * inv
    cos_s, sin_s = jnp.cos(start), jnp.sin(start)
    cos_o, sin_o = rope_ref[0], rope_ref[1]
    lane = lax.broadcasted_iota(jnp.int32, (c, dk), 1)
    cos_t = cos_s * cos_o - sin_s * sin_o
    sin_t = jnp.where(lane < dk // 2, -1.0, 1.0) * (sin_s * cos_o + cos_s * sin_o)

    def rot(x):
        return x * cos_t + pltpu.roll(x, dk // 2, 1) * sin_t

    hpb = RET_HEADS // 2
    for h in range(RET_HEADS):
        qh = rot(q_ref[:, h * dk:(h + 1) * dk])
        kh = rot(k_ref[:, h * dk:(h + 1) * dk]) * (dk ** -0.5)
        v_ref = v0_ref if h < hpb else v1_ref
        vs = slice((h % hpb) * dv, (h % hpb + 1) * dv)
        vh = v_ref[:, vs].astype(BF16)
        lg = log_g[h]
        q_dec = qh * dec_ref[h, 0]
        k_dec = kh * dec_ref[h, 1]
        if reverse:
            y = y1_ref[:, h * dv:(h + 1) * dv]
        else:
            scores = lax.dot_general(qh.astype(BF16), kh.astype(BF16), (((1,), (1,)), ((), ())),
                                     preferred_element_type=F32) * dm_ref[h]
            y = _dot(scores.astype(BF16), vh)
        st = state_ref[h]
        y = y + _dot(q_dec.astype(BF16), st.astype(BF16))
        kv = lax.dot_general(k_dec.astype(BF16), vh, (((0,), (0,)), ((), ())), preferred_element_type=F32)
        state_ref[h] = math.exp(lg * c) * st + kv
        if reverse:
            g_ref = g0_ref if h < hpb else g1_ref
            yn = y * lax.rsqrt(jnp.mean(y * y, axis=-1, keepdims=True) + EPS) * norm_ref[:, h * dv:(h + 1) * dv]
            o_ref[:, h * dv:(h + 1) * dv] = (jax.nn.silu(g_ref[:, vs]) * yn).astype(o_ref.dtype)
        else:
            o_ref[:, h * dv:(h + 1) * dv] = y


def retention_branch(proj, norm, col0, c=RET_CHUNK):
    s = proj.shape[0]
    d_ret = norm.shape[0]
    qk = RET_HEADS * RET_KEY_DIM
    half = RET_KEY_DIM // 2
    inv = ROPE_BASE ** (-jnp.arange(half, dtype=F32) / half)
    inv = jnp.concatenate([inv, inv]).reshape(1, RET_KEY_DIM)
    vw = d_ret // 2
    qb, kb = col0 // qk, (col0 + qk) // qk
    vb = (col0 + 2 * qk) // vw
    gb = (col0 + 2 * qk + d_ret) // vw
    assert col0 % qk == 0 and (col0 + 2 * qk) % vw == 0

    c = min(c, s)
    nc = s // c

    def one_pass(reverse, extra, out_dtype):
        ti = (lambda i: nc - 1 - i) if reverse else (lambda i: i)
        rows = lambda width, col: pl.BlockSpec((c, width), lambda i: (ti(i), col))
        full = lambda width: pl.BlockSpec((1, width), lambda i: (0, 0))
        scratch = [pltpu.VMEM((RET_HEADS, RET_KEY_DIM, d_ret // RET_HEADS), F32), pltpu.VMEM((2, c, RET_KEY_DIM), F32),
                   pltpu.VMEM((RET_HEADS, 2, c, RET_KEY_DIM), F32)]
        in_specs = [rows(qk, qb), rows(qk, kb), rows(vw, vb), rows(vw, vb + 1), full(RET_KEY_DIM)]
        args = [proj, proj, proj, proj, inv]
        if reverse:
            in_specs += [rows(d_ret, 0), rows(vw, gb), rows(vw, gb + 1), full(d_ret)]
            args += [extra, proj, proj, norm.reshape(1, d_ret).astype(F32)]
        else:
            scratch.append(pltpu.VMEM((RET_HEADS, c, c), F32))
        return pl.pallas_call(
            functools.partial(_ret_kernel, reverse=reverse, nc=nc, c=c),
            grid=(nc,),
            in_specs=in_specs,
            out_specs=rows(d_ret, 0),
            out_shape=jax.ShapeDtypeStruct((s, d_ret), out_dtype),
            scratch_shapes=scratch,
            compiler_params=_params("arbitrary"),
            name="ret_bwd" if reverse else "ret_fwd",
        )(*args)

    return one_pass(True, one_pass(False, None, F32), BF16)


def _hy_conv_kernel(xp_ref, x_ref, xn_ref, w_ref, b_ref, o_ref, *, nb, tb):
    ti = pl.program_id(1)
    o_ref[...] = _dwconv_block(xp_ref, x_ref, xn_ref, w_ref, b_ref, ti, nb, tb, 3)


def hyena_short_conv(proj, conv_w, conv_b, col_block, c, tb=CONV_ROWS):
    s = proj.shape[0]
    tb = min(tb, s)
    nb = s // tb
    r = tb // V7X_SUBLANES
    last = nb * r - 1
    return pl.pallas_call(
        functools.partial(_hy_conv_kernel, nb=nb, tb=tb),
        grid=(3, nb),
        in_specs=[
            pl.BlockSpec((V7X_SUBLANES, c), lambda j, i: (jnp.maximum(i * r - 1, 0), col_block + j)),
            pl.BlockSpec((tb, c), lambda j, i: (i, col_block + j)),
            pl.BlockSpec((V7X_SUBLANES, c), lambda j, i: (jnp.minimum((i + 1) * r, last), col_block + j)),
            pl.BlockSpec((3, c), lambda j, i: (0, j)),
            pl.BlockSpec((1, c), lambda j, i: (0, j)),
        ],
        out_specs=pl.BlockSpec((None, tb, c), lambda j, i: (j, i, 0)),
        out_shape=jax.ShapeDtypeStruct((3, s, c), F32),
        compiler_params=_params("parallel", "parallel"),
        name="hyena_short_conv",
    )(proj, proj, proj, conv_w.astype(F32), conv_b.reshape(1, -1).astype(F32))


def _fft_sizes(length):
    n = 2 * length
    n2 = min(FFT_N2, n // 4)
    n1 = n // n2
    return n, n1, n2, n1 // 2, n1 // 2 + 1


def _fft_tables(length):
    n, n1, n2, n1h, k1n = _fft_sizes(length)
    eye = np.eye(V7X_SUBLANES)
    k1 = np.arange(k1n)[:, None]
    a1 = np.arange(n1h)[None, :]
    ang1 = 2.0 * np.pi * ((k1 * a1) % n1) / n1
    f1 = np.concatenate([np.kron(np.cos(ang1), eye), np.kron(-np.sin(ang1), eye)], axis=0)
    wgt = np.full((k1n,), 2.0)
    wgt[0] = wgt[-1] = 1.0
    cw = (np.cos(ang1) * wgt[:, None] / n).T
    sw = (np.sin(ang1) * wgt[:, None] / n).T
    b3 = np.concatenate([np.kron(cw, eye), np.kron(-sw, eye)], axis=1)
    a2 = np.arange(n2)
    angt = 2.0 * np.pi * (k1 * a2[None, :]) / n
    tw = jnp.asarray(np.stack([np.cos(angt), np.sin(angt)]), F32)
    tw = jnp.broadcast_to(tw[..., None], tw.shape + (V7X_LANES,))
    ang2 = 2.0 * np.pi * ((a2[:, None] * a2[None, :]) % n2) / n2
    c2, s2 = np.cos(ang2), np.sin(ang2)
    f2 = np.block([[c2, s2], [-s2, c2]])
    f2i = np.block([[c2, -s2], [s2, c2]])
    return jnp.asarray(f1, BF16), tw, jnp.asarray(f2, BF16), jnp.asarray(f2i, BF16), jnp.asarray(b3, BF16)


def _pack_pair(re, im):
    hi = lax.bitcast_convert_type(re.astype(BF16).astype(F32), jnp.uint32)
    lo = lax.bitcast_convert_type(im.astype(BF16).astype(F32), jnp.uint32)
    return hi | (lo >> 16)


def _unpack_pair(w):
    re = lax.bitcast_convert_type(w & jnp.uint32(0xFFFF0000), F32)
    im = lax.bitcast_convert_type(w << 16, F32)
    return re, im


def _twiddle(c, s, xr, xi):
    out_r, out_i = [], []
    for g in range(xr.shape[-1] // V7X_LANES):
        sl = slice(g * V7X_LANES, (g + 1) * V7X_LANES)
        out_r.append(xr[..., sl] * c + xi[..., sl] * s)
        out_i.append(xi[..., sl] * c - xr[..., sl] * s)
    return jnp.concatenate(out_r, axis=-1), jnp.concatenate(out_i, axis=-1)


def _stage1(z, f1_ref, tw_ref, k1n):
    sub = V7X_SUBLANES
    cb = z.shape[-1]
    a = _dot(f1_ref[...], z.astype(BF16))
    ar = a[:k1n * sub].reshape(k1n, sub, cb)
    ai = a[k1n * sub:].reshape(k1n, sub, cb)
    return _pack_pair(*_twiddle(tw_ref[0], tw_ref[1], ar, ai))


def _fft1_kernel(z_ref, f1_ref, tw_ref, a_ref):
    n1h, sub, cb = z_ref.shape
    a_ref[...] = _stage1(z_ref[...].reshape(n1h * sub, cb), f1_ref, tw_ref, a_ref.shape[0])


def fft_stage1(z, f1, tw, length, cb=FFT_COL_BLOCK):
    _, _, n2, n1h, k1n = _fft_sizes(length)
    c = z.shape[-1]
    cb = min(cb, c)
    lead = z.shape[:-2]
    z4 = z.reshape(lead + (n1h, n2, c))
    nl = len(lead)
    sub = V7X_SUBLANES
    return pl.pallas_call(
        _fft1_kernel,
        grid=(n2 // sub, c // cb),
        in_specs=[pl.BlockSpec((None,) * nl + (n1h, sub, cb), lambda j, ci: (0,) * nl + (0, j, ci)),
                  pl.BlockSpec(f1.shape, lambda j, ci: (0, 0)),
                  pl.BlockSpec((2, k1n, sub, V7X_LANES), lambda j, ci: (0, 0, j, 0))],
        out_specs=pl.BlockSpec((k1n, sub, cb), lambda j, ci: (0, j, ci)),
        out_shape=jax.ShapeDtypeStruct((k1n, n2, c), jnp.uint32),
        compiler_params=_params("parallel", "parallel"),
        name="fft_stage1",
    )(z4, f1, tw)


def _filter_fft1_kernel(fb_ref, w1_ref, b1_ref, w2_ref, b2_ref, w3_ref, freq_ref, decay_ref, f1_ref, tw_ref,
                        a_ref, hdn_ref, trig_ref, *, length, n2):
    k1n, sub, cb = a_ref.shape
    rows = hdn_ref.shape[0]
    j = pl.program_id(0)
    hi = lax.Precision.HIGHEST
    rho = lax.broadcasted_iota(jnp.int32, (rows, 1), 0)
    base = ((rho // sub) * n2).astype(F32)
    idx = base + (j * sub + rho % sub).astype(F32)
    t = idx / (length - 1.0)
    lane = lax.broadcasted_iota(jnp.int32, (rows, V7X_LANES), 1)
    w_band = (2.0 * math.pi / length) * fb_ref[...]

    @pl.when((j == 0) & (pl.program_id(1) == 0))
    def _():
        shift = jnp.where(lane <= HYENA_BANDS, 0.5 * math.pi, math.pi)
        trig_ref[0] = jnp.sin(w_band * base + shift)
        trig_ref[1] = jnp.cos(w_band * base + shift)

    @pl.when(pl.program_id(1) == 0)
    def _():
        off = (j * sub).astype(F32) + lax.broadcasted_iota(jnp.int32, (sub, 1), 0).astype(F32)
        sin_b, cos_b = jnp.sin(w_band * off), jnp.cos(w_band * off)
        trig = (trig_ref[0].reshape(rows // sub, sub, V7X_LANES) * cos_b
                + trig_ref[1].reshape(rows // sub, sub, V7X_LANES) * sin_b).reshape(rows, V7X_LANES)
        feats = jnp.where(lane == 0, t, jnp.where(lane <= 2 * HYENA_BANDS, trig, 0.0))
        freq = freq_ref[...]
        hdn = jnp.sin(freq * (jnp.dot(feats, w1_ref[...], precision=hi, preferred_element_type=F32) + b1_ref[...]))
        hdn_ref[...] = jnp.sin(freq * (jnp.dot(hdn, w2_ref[...], precision=hi, preferred_element_type=F32)
                                       + b2_ref[...]))

    filt = _dot(hdn_ref[...].astype(BF16), w3_ref[...]) * jnp.exp(-t * jnp.abs(decay_ref[...]))
    a_ref[...] = _stage1(filt, f1_ref, tw_ref, k1n)


def filter_fft_stage1(length, w1, b1, w2, b2, w3, freq, decay, f1, tw, cb=2 * FFT_COL_BLOCK):
    _, _, n2, n1h, k1n = _fft_sizes(length)
    emb, hid = w1.shape
    bands = (emb - 1) // 2
    assert bands == HYENA_BANDS
    n_out = w3.shape[1]
    cb = min(cb, n_out)
    sub = V7X_SUBLANES
    f = jnp.linspace(1e-4, bands - 1, bands, dtype=F32)
    fb = jnp.zeros((1, V7X_LANES), F32).at[0, 1:1 + bands].set(f).at[0, 1 + bands:1 + 2 * bands].set(f)
    w1p = jnp.zeros((V7X_LANES, hid), F32).at[:emb].set(w1.astype(F32))
    full = lambda shape: pl.BlockSpec(shape, lambda j, ci: (0,) * len(shape))
    return pl.pallas_call(
        functools.partial(_filter_fft1_kernel, length=length, n2=n2),
        grid=(n2 // sub, n_out // cb),
        in_specs=[full((1, V7X_LANES)), full((V7X_LANES, hid)), full((1, hid)), full((hid, hid)), full((1, hid)),
                  pl.BlockSpec((hid, cb), lambda j, ci: (0, ci)), full((1, hid)),
                  pl.BlockSpec((1, cb), lambda j, ci: (0, ci)), full(f1.shape),
                  pl.BlockSpec((2, k1n, sub, V7X_LANES), lambda j, ci: (0, 0, j, 0))],
        out_specs=pl.BlockSpec((k1n, sub, cb), lambda j, ci: (0, j, ci)),
        out_shape=jax.ShapeDtypeStruct((k1n, n2, n_out), jnp.uint32),
        scratch_shapes=[pltpu.VMEM((n1h * sub, hid), F32), pltpu.VMEM((2, n1h * sub, V7X_LANES), F32)],
        compiler_params=_params("arbitrary", "arbitrary"),
        name="hyena_filter_fft1",
    )(fb, w1p, b1.reshape(1, hid).astype(F32), w2.astype(F32), b2.reshape(1, hid).astype(F32), w3.astype(BF16),
      freq.reshape(1, hid).astype(F32), decay.reshape(1, n_out).astype(F32), f1, tw)


def _slab_dft(t_ref, xr, xi):
    n2 = xr.shape[0]
    x = jnp.concatenate([xr.astype(BF16), xi.astype(BF16)], axis=0)
    y = _dot(t_ref[...], x)
    return y[:n2], y[n2:]


def _slabs_per_step(k1n, most=FFT_SLABS_PER_STEP):
    return max(d for d in range(1, most + 1) if k1n % d == 0)


def _filter_spec_kernel(af_ref, ab_ref, f2_ref, f2i_ref, g_ref):
    n2 = af_ref.shape[1]
    for k in range(af_ref.shape[0]):
        ar, ai = _unpack_pair(af_ref[k])
        br, bi = _unpack_pair(ab_ref[k])
        top = jnp.concatenate([(ar + br).astype(BF16), (ai + bi).astype(BF16)], axis=0)
        bot = jnp.concatenate([(ai - bi).astype(BF16), (ar - br).astype(BF16)], axis=0)
        g_ref[k] = _pack_pair(_dot(f2_ref[:n2, :], top), _dot(f2i_ref[:n2, :], bot))


def filter_spectrum(a, f2, f2i, c, cb=FFT_COL_BLOCK):
    k1n, n2, cf = a.shape
    orders = cf // (2 * c)
    cb = min(cb, c)
    per = c // cb
    kb = _slabs_per_step(k1n)
    fcol = lambda k, j: (k, 0, (j // per) * 2 * per + j % per)
    bcol = lambda k, j: (k, 0, (j // per) * 2 * per + per + j % per)
    blk = (kb, n2, cb)
    f2_spec = pl.BlockSpec((2 * n2, 2 * n2), lambda k, j: (0, 0))
    return pl.pallas_call(
        _filter_spec_kernel,
        grid=(k1n // kb, orders * per),
        in_specs=[pl.BlockSpec(blk, fcol), pl.BlockSpec(blk, bcol), f2_spec, f2_spec],
        out_specs=pl.BlockSpec(blk, lambda k, j: (k, 0, j)),
        out_shape=jax.ShapeDtypeStruct((k1n, n2, orders * c), jnp.uint32),
        compiler_params=_params("parallel", "parallel"),
        name="hyena_filter_spectrum",
    )(a, a, f2, f2i)


def _slab_conv_kernel(a_ref, g_ref, tw_ref, f2_ref, f2i_ref, p_ref):
    for k in range(a_ref.shape[0]):
        xr, xi = _slab_dft(f2_ref, *_unpack_pair(a_ref[k]))
        gr, gi = _unpack_pair(g_ref[k])
        qr, qi = _slab_dft(f2i_ref, xr * gr - xi * gi, xr * gi + xi * gr)
        p_ref[k] = _pack_pair(*_twiddle(tw_ref[0, k], -tw_ref[1, k], qr, qi))


def slab_conv(a, g, tw, f2, f2i, order, cb=FFT_COL_BLOCK):
    k1n, n2, c = a.shape
    cb = min(cb, c)
    per = c // cb
    kb = _slabs_per_step(k1n)
    blk = (kb, n2, cb)
    dcol = lambda k, j: (k, 0, j)
    gcol = lambda k, j: (k, 0, order * per + j)
    f2_spec = pl.BlockSpec((2 * n2, 2 * n2), lambda k, j: (0, 0))
    return pl.pallas_call(
        _slab_conv_kernel,
        grid=(k1n // kb, per),
        in_specs=[pl.BlockSpec(blk, dcol), pl.BlockSpec(blk, gcol),
                  pl.BlockSpec((2, kb, n2, V7X_LANES), lambda k, j: (0, k, 0, 0)), f2_spec, f2_spec],
        out_specs=pl.BlockSpec(blk, dcol),
        out_shape=jax.ShapeDtypeStruct((k1n, n2, c), jnp.uint32),
        compiler_params=_params("parallel", "parallel"),
        name="hyena_slab_conv",
    )(a, g, tw, f2, f2i)


def _ifft3_kernel(*refs, last, groups):
    if last:
        p_ref, b3_ref, z_ref, gate_ref, skip_ref, norm_ref, o_ref = refs
    else:
        p_ref, b3_ref, z_ref, gate_ref, skip_ref, f1_ref, tw_ref, o_ref, a_ref = refs
    k1n, sub, cb = p_ref.shape
    n1h = o_ref.shape[0]
    pr, pi = _unpack_pair(p_ref[...])
    p = jnp.concatenate([pr.reshape(k1n * sub, cb).astype(BF16), pi.reshape(k1n * sub, cb).astype(BF16)], axis=0)
    y = _dot(b3_ref[...], p).reshape(n1h, sub, cb)
    z = gate_ref[...] * (y + skip_ref[...] * z_ref[...])
    if last:
        w = cb // groups
        for h in range(groups):
            zh = z[..., h * w:(h + 1) * w]
            zh = zh * lax.rsqrt(jnp.mean(zh * zh, axis=-1, keepdims=True) + EPS)
            o_ref[:, :, h * w:(h + 1) * w] = zh * norm_ref[:, h * w:(h + 1) * w]
    else:
        o_ref[...] = z
        a_ref[...] = _stage1(z.reshape(n1h * sub, cb), f1_ref, tw_ref, k1n)


def ifft_stage3_gate(p, b3, u, z_arr, gate_idx, skip, length, f1=None, tw=None, norm=None):
    _, _, n2, n1h, k1n = _fft_sizes(length)
    c = p.shape[-1]
    last = norm is not None
    sub = V7X_SUBLANES
    u4 = u.reshape(u.shape[0], n1h, n2, c)
    z4 = z_arr.reshape((-1, n1h, n2, c))
    pblk = pl.BlockSpec((k1n, sub, c), lambda j: (0, j, 0))
    zblk = pl.BlockSpec((n1h, sub, c), lambda j: (0, j, 0))
    full = lambda shape: pl.BlockSpec(shape, lambda j: (0,) * len(shape))
    in_specs = [pblk, full(b3.shape), pl.BlockSpec((None, n1h, sub, c), lambda j: (0, 0, j, 0)),
                pl.BlockSpec((None, n1h, sub, c), lambda j: (gate_idx, 0, j, 0)), full((1, c))]
    args = [p, b3, z4, u4, skip.reshape(1, c).astype(F32)]
    z_shape = jax.ShapeDtypeStruct((n1h, n2, c), F32)
    if last:
        in_specs.append(full((1, c)))
        args.append(norm.reshape(1, c).astype(F32))
        out_specs, out_shape = zblk, z_shape
    else:
        in_specs += [full(f1.shape), pl.BlockSpec((2, k1n, sub, V7X_LANES), lambda j: (0, 0, j, 0))]
        args += [f1, tw]
        out_specs, out_shape = [zblk, pblk], [z_shape, jax.ShapeDtypeStruct((k1n, n2, c), jnp.uint32)]
    out = pl.pallas_call(
        functools.partial(_ifft3_kernel, last=last, groups=HYENA_GROUPS),
        grid=(n2 // sub,),
        in_specs=in_specs,
        out_specs=out_specs,
        out_shape=out_shape,
        compiler_params=_params("parallel"),
        name="ifft_stage3_norm" if last else "ifft_stage3_fft1",
    )(*args)
    if last:
        return out.reshape(length, c)
    return out[0].reshape(1, length, c), out[1]


def hyena_branch(proj, tables, conv_w, conv_b, f_w1, f_b1, f_w2, f_b2, f_w3, freq, decay, skip, norm, col_block):
    length = proj.shape[0]
    c = norm.shape[0]
    orders = skip.shape[0]
    f1, tw, f2, f2i, b3 = tables
    u = hyena_short_conv(proj, conv_w, conv_b, col_block, c)
    a_filt = filter_fft_stage1(length, f_w1, f_b1, f_w2, f_b2, f_w3, freq, decay, f1, tw)
    g = filter_spectrum(a_filt, f2, f2i, c)
    z, a = u, fft_stage1(u, f1, tw, length)
    for o in range(orders - 1):
        p = slab_conv(a, g, tw, f2, f2i, o)
        z, a = ifft_stage3_gate(p, b3, u, z, 1 + o, skip[o], length, f1=f1, tw=tw)
    p = slab_conv(a, g, tw, f2, f2i, orders - 1)
    return ifft_stage3_gate(p, b3, u, z, orders, skip[orders - 1], length, norm=norm)


def kernel(x, norm_mix, w_in, lru_conv_w, lru_conv_b, lru_wr, lru_br, lru_wi, lru_bi, lru_lambda, lru_norm, hy_conv_w, hy_conv_b, hy_f_w1, hy_f_b1, hy_f_w2, hy_f_b2, hy_f_w3, hy_freq, hy_decay, hy_skip, hy_norm, ret_norm, w_out, norm_ffn, w_gate, w_up, w_down, norm_final):
    b, s, d = x.shape
    assert b == 1
    depth = w_in.shape[0]
    d_lru = lru_conv_w.shape[-1]
    d_hy = hy_norm.shape[-1]
    tables = _fft_tables(s)
    xs = x.reshape(s, d)
    for l in range(depth):
        h = rmsnorm(xs, norm_mix[l], BF16)
        proj = matmul_fullk([h], cast_bf16(w_in, l), tiles=IN_PROJ_TILES)
        y_a = lru_branch(proj, lru_conv_w[l], lru_conv_b[l], lru_wr[l], lru_br[l], lru_wi[l], lru_bi[l],
                         lru_lambda[l], lru_norm[l])
        y_b = hyena_branch(proj, tables, hy_conv_w[l], hy_conv_b[l], hy_f_w1[l], hy_f_b1[l], hy_f_w2[l],
                           hy_f_b2[l], hy_f_w3[l], hy_freq[l], hy_decay[l], hy_skip[l], hy_norm[l],
                           (2 * d_lru) // d_hy)
        y_c = retention_branch(proj, ret_norm[l], 2 * d_lru + 3 * d_hy)
        xs = matmul_fullk([y_a, y_b, y_c], w_out, l, residual=xs)
        h = rmsnorm(xs, norm_ffn[l], BF16)
        act = ffn_up(h, w_gate, w_up, l)
        xs = matmul_fullk([act], cast_bf16(w_down, l), residual=xs)
    return rmsnorm(xs, norm_final, x.dtype).reshape(b, s, d)
```
